```python
import math
import jax, jax.numpy as jnp
from jax import lax
import numpy as np

D_MODEL = 1024
BATCH = 16
SEQ = 4096
DEPTH = 4

N_MIXERS = 3
N_A_LAYERS = (DEPTH + 2) // 3
N_B_LAYERS = (DEPTH + 1) // 3
N_C_LAYERS = DEPTH // 3
N_MOD = 9
D_FF = 2816
HEAD_DIM = 64
BLOCK = 128
NORM_EPS = 1e-6
DA_HEADS = D_MODEL // (2 * HEAD_DIM)
DA_V_DIM = 2 * HEAD_DIM
SUBLN_EPS = 1e-5
RW_HEADS = D_MODEL // HEAD_DIM
RW_DECAY_LORA = 64
RW_ICLR_LORA = 64
RW_GATE_LORA = 160
RW_LNX_EPS = 64e-5
SW_Q_HEADS = D_MODEL // HEAD_DIM
SW_KV_HEADS = 2
SW_GROUP = SW_Q_HEADS // SW_KV_HEADS
SW_WINDOW = 128
SW_Q_DIM = SW_Q_HEADS * HEAD_DIM
SW_KV_DIM = SW_KV_HEADS * HEAD_DIM

kernel_name = "hybrid_diffattn_rwkv7_swasink_macaron"


def rms_norm(x, g, eps=NORM_EPS):
    xf = x.astype(jnp.float32)
    y = xf * lax.rsqrt(jnp.mean(xf * xf, axis=-1, keepdims=True) + eps)
    return (y * g.astype(jnp.float32)).astype(x.dtype)


def modulate(h, shift, scale):
    return h * (1 + scale) + shift


def swiglu(h, w_in, w_out):
    gate, up = jnp.split(h @ w_in, 2, axis=-1)
    return (jax.nn.silu(gate) * up) @ w_out


def diff_lambda_init(layer):
    return 0.8 - 0.6 * math.exp(-0.3 * layer)


def differential_attention(h, w_qkv, w_o, q_norm_g, k_norm_g, lambdas, subln_g, lambda_init):
    B, T, _ = h.shape
    nb = T // BLOCK
    q, k, v = jnp.split(h @ w_qkv, 3, axis=-1)
    q = rms_norm(q.reshape(B, T, DA_HEADS, 2, HEAD_DIM), q_norm_g)
    k = rms_norm(k.reshape(B, T, DA_HEADS, 2, HEAD_DIM), k_norm_g)
    v = v.reshape(B, T, DA_HEADS, DA_V_DIM)
    lam = lambdas.astype(jnp.float32)
    lam_full = jnp.exp(jnp.sum(lam[0] * lam[1])) - jnp.exp(jnp.sum(lam[2] * lam[3])) + lambda_init
    scale = HEAD_DIM ** -0.5
    key_pos = jnp.arange(T)
    q_blocks = jnp.moveaxis(q.reshape(B, nb, BLOCK, DA_HEADS, 2, HEAD_DIM), 1, 0)

    def one_block(args):
        q_blk, n = args
        s = jnp.einsum('bqhcd,bkhcd->bhcqk', q_blk, k).astype(jnp.float32) * scale
        q_pos = n * BLOCK + jnp.arange(BLOCK)
        causal = key_pos[None, :] <= q_pos[:, None]
        p = jax.nn.softmax(jnp.where(causal, s, -jnp.inf), axis=-1)
        attn = (p[:, :, 0] - lam_full * p[:, :, 1]).astype(v.dtype)
        return jnp.einsum('bhqk,bkhe->bqhe', attn, v)

    o = lax.map(one_block, (q_blocks, jnp.arange(nb)))
    o = jnp.moveaxis(o, 0, 1).reshape(B, T, DA_HEADS, DA_V_DIM)
    o = rms_norm(o, subln_g, SUBLN_EPS) * (1.0 - lambda_init)
    return o.reshape(B, T, DA_HEADS * DA_V_DIM) @ w_o


def rwkv7_scan(r, w, k, v, a, b):
    B, T, H, N = r.shape

    def step(S, inp):
        r_t, w_t, k_t, v_t, a_t, b_t = inp
        sa = jnp.einsum('bhvk,bhk->bhv', S, a_t)
        S = S * w_t[:, :, None, :] + sa[..., None] * b_t[:, :, None, :] + v_t[..., None] * k_t[:, :, None, :]
        return S, jnp.einsum('bhvk,bhk->bhv', S, r_t)

    S0 = jnp.zeros((B, H, N, N), jnp.float32)
    xs = (jnp.moveaxis(r, 1, 0), jnp.moveaxis(w, 1, 0), jnp.moveaxis(k, 1, 0),
          jnp.moveaxis(v, 1, 0), jnp.moveaxis(a, 1, 0), jnp.moveaxis(b, 1, 0))
    _, o = lax.scan(step, S0, xs)
    return jnp.moveaxis(o, 0, 1)


def rwkv7_time_mix(h, mu, w_rkv, w_o, decay_w0, decay_w1, decay_w2, iclr_a0, iclr_a1, iclr_a2,
                   gate_g1, gate_g2, k_k, k_a, r_k, lnx_g, lnx_b):
    B, T, D = h.shape
    f32 = jnp.float32
    xx = jnp.pad(h[:, :-1], ((0, 0), (1, 0), (0, 0))) - h
    mixed = h[None] + xx[None] * mu[:, None, None, :]
    r, k, v = jnp.einsum('nbtd,nde->nbte', mixed[:3], w_rkv)
    xw, xa, xg = mixed[3], mixed[4], mixed[5]
    w_log = -jax.nn.softplus(-(decay_w0 + jnp.tanh(xw @ decay_w1) @ decay_w2)) - 0.5
    decay = jnp.exp(-jnp.exp(w_log.astype(f32)))
    a = jax.nn.sigmoid(iclr_a0 + (xa @ iclr_a1) @ iclr_a2)
    g = jax.nn.sigmoid(xg @ gate_g1) @ gate_g2
    hd = lambda t: t.reshape(B, T, RW_HEADS, HEAD_DIM).astype(f32)
    kk = hd(k * k_k)
    kk = kk / jnp.maximum(jnp.sqrt(jnp.sum(kk * kk, axis=-1, keepdims=True)), 1e-12)
    k = k * (1 + (a - 1) * k_a)
    r_h, k_h, v_h, a_h = hd(r), hd(k), hd(v), hd(a)
    o = rwkv7_scan(r_h, decay.reshape(B, T, RW_HEADS, HEAD_DIM), k_h, v_h, -kk, kk * a_h)
    mean = jnp.mean(o, axis=-1, keepdims=True)
    var = jnp.mean(jnp.square(o - mean), axis=-1, keepdims=True)
    o = ((o - mean) * lax.rsqrt(var + RW_LNX_EPS)).reshape(B, T, D) * lnx_g.astype(f32) + lnx_b.astype(f32)
    bonus = jnp.sum(r_h * k_h * r_k.astype(f32), axis=-1, keepdims=True) * v_h
    o = (o + bonus.reshape(B, T, D)).astype(h.dtype)
    return (o * g) @ w_o


def sliding_window_sink_attention(h, w_qkv, w_o, q_norm_g, k_norm_g, sinks):
    B, T, _ = h.shape
    nb = T // BLOCK
    q, k, v = jnp.split(h @ w_qkv, [SW_Q_DIM, SW_Q_DIM + SW_KV_DIM], axis=-1)
    q = rms_norm(q.reshape(B, nb, BLOCK, SW_KV_HEADS, SW_GROUP, HEAD_DIM), q_norm_g)
    k = rms_norm(k.reshape(B, nb, BLOCK, SW_KV_HEADS, HEAD_DIM), k_norm_g)
    v = v.reshape(B, nb, BLOCK, SW_KV_HEADS, HEAD_DIM)

    def band(t):
        prev = jnp.pad(t[:, :-1], ((0, 0), (1, 0), (0, 0), (0, 0), (0, 0)))
        return jnp.concatenate([prev, t], axis=2)

    kb, vb = band(k), band(v)
    s = jnp.einsum('bnqhgd,bnkhd->bnhgqk', q, kb).astype(jnp.float32) * (HEAD_DIM ** -0.5)
    qi = jnp.arange(BLOCK)[:, None]
    ki = jnp.arange(2 * BLOCK)[None, :]
    rel = qi + BLOCK - ki
    blk = jnp.arange(nb)[:, None, None]
    valid = (rel >= 0) & (rel < SW_WINDOW) & (blk * BLOCK + ki - BLOCK >= 0)
    s = jnp.where(valid[None, :, None, None], s, -jnp.inf)
    sink = sinks.astype(jnp.float32).reshape(SW_KV_HEADS, SW_GROUP)[None, None, :, :, None, None]
    m = jnp.maximum(jnp.max(s, axis=-1, keepdims=True), sink)
    p = jnp.exp(s - m)
    p = (p / (jnp.sum(p, axis=-1, keepdims=True) + jnp.exp(sink - m))).astype(v.dtype)
    o = jnp.einsum('bnhgqk,bnkhd->bnqhgd', p, vb).reshape(B, T, SW_Q_DIM)
    return o @ w_o


def _normal(key, shape, scale):
    return jax.random.normal(key, shape, jnp.float32) * scale


def _gain(key, shape):
    return 1.0 + 0.02 * jax.random.normal(key, shape, jnp.float32)


def setup_inputs(seed: int = 0) -> dict:
    key = jax.random.key(seed)
    ks = iter(jax.random.split(key, 40))
    D, L = D_MODEL, DEPTH
    NA, NB, NC = N_A_LAYERS, N_B_LAYERS, N_C_LAYERS
    return {
        "x": _normal(next(ks), (BATCH, SEQ, D), 1.0),
        "c": _normal(next(ks), (BATCH, D), 1.0),
        "norm_g": _gain(next(ks), (L, 3, D)),
        "ada_w": _normal(next(ks), (L, D, N_MOD * D), 0.5 * D ** -0.5),
        "ada_b": _normal(next(ks), (L, N_MOD * D), 0.02),
        "ffn_w_in": _normal(next(ks), (L, 2, D, 2 * D_FF), D ** -0.5),
        "ffn_w_out": _normal(next(ks), (L, 2, D_FF, D), D_FF ** -0.5),
        "da_w_qkv": _normal(next(ks), (NA, D, 3 * D), D ** -0.5),
        "da_w_o": _normal(next(ks), (NA, D, D), D ** -0.5),
        "da_q_norm_g": _gain(next(ks), (NA, HEAD_DIM)),
        "da_k_norm_g": _gain(next(ks), (NA, HEAD_DIM)),
        "da_lambda": _normal(next(ks), (NA, 4, HEAD_DIM), 0.1),
        "da_subln_g": _gain(next(ks), (NA, DA_V_DIM)),
        "rw_mu": jax.random.uniform(next(ks), (NB, 6, D), jnp.float32),
        "rw_w_rkv": _normal(next(ks), (NB, 3, D, D), D ** -0.5),
        "rw_w_o": _normal(next(ks), (NB, D, D), D ** -0.5),
        "rw_decay_w0": jax.random.uniform(next(ks), (NB, D), jnp.float32, -6.5, -1.5),
        "rw_decay_w1": _normal(next(ks), (NB, D, RW_DECAY_LORA), D ** -0.5),
        "rw_decay_w2": _normal(next(ks), (NB, RW_DECAY_LORA, D), 0.5 * RW_DECAY_LORA ** -0.5),
        "rw_iclr_a0": _normal(next(ks), (NB, D), 0.1),
        "rw_iclr_a1": _normal(next(ks), (NB, D, RW_ICLR_LORA), D ** -0.5),
        "rw_iclr_a2": _normal(next(ks), (NB, RW_ICLR_LORA, D), 0.5 * RW_ICLR_LORA ** -0.5),
        "rw_gate_g1": _normal(next(ks), (NB, D, RW_GATE_LORA), D ** -0.5),
        "rw_gate_g2": _normal(next(ks), (NB, RW_GATE_LORA, D), RW_GATE_LORA ** -0.5),
        "rw_k_k": 0.85 + 0.02 * jax.random.normal(next(ks), (NB, D), jnp.float32),
        "rw_k_a": _gain(next(ks), (NB, D)),
        "rw_r_k": _normal(next(ks), (NB, RW_HEADS, HEAD_DIM), 0.1),
        "rw_lnx_g": _gain(next(ks), (NB, D)),
        "rw_lnx_b": _normal(next(ks), (NB, D), 0.02),
        "sw_w_qkv": _normal(next(ks), (NC, D, SW_Q_DIM + 2 * SW_KV_DIM), D ** -0.5),
        "sw_w_o": _normal(next(ks), (NC, SW_Q_DIM, D), SW_Q_DIM ** -0.5),
        "sw_q_norm_g": _gain(next(ks), (NC, HEAD_DIM)),
        "sw_k_norm_g": _gain(next(ks), (NC, HEAD_DIM)),
        "sw_sinks": _normal(next(ks), (NC, SW_Q_HEADS), 1.0),
    }


def reference(x, c, norm_g, ada_w, ada_b, ffn_w_in, ffn_w_out,
              da_w_qkv, da_w_o, da_q_norm_g, da_k_norm_g, da_lambda, da_subln_g,
              rw_mu, rw_w_rkv, rw_w_o, rw_decay_w0, rw_decay_w1, rw_decay_w2,
              rw_iclr_a0, rw_iclr_a1, rw_iclr_a2, rw_gate_g1, rw_gate_g2,
              rw_k_k, rw_k_a, rw_r_k, rw_lnx_g, rw_lnx_b,
              sw_w_qkv, sw_w_o, sw_q_norm_g, sw_k_norm_g, sw_sinks):
    cond = jax.nn.silu(c)
    for i in range(DEPTH):
        mod = (cond @ ada_w[i] + ada_b[i])[:, None, :]
        sh1, sc1, g1, sh2, sc2, g2, sh3, sc3, g3 = jnp.split(mod, N_MOD, axis=-1)
        h = modulate(rms_norm(x, norm_g[i, 0]), sh1, sc1)
        x = x + 0.5 * g1 * swiglu(h, ffn_w_in[i, 0], ffn_w_out[i, 0])
        h = modulate(rms_norm(x, norm_g[i, 1]), sh2, sc2)
        kind, j = i % N_MIXERS, i // N_MIXERS
        if kind == 0:
            y = differential_attention(h, da_w_qkv[j], da_w_o[j], da_q_norm_g[j], da_k_norm_g[j],
                                       da_lambda[j], da_subln_g[j], diff_lambda_init(i))
        elif kind == 1:
            y = rwkv7_time_mix(h, rw_mu[j], rw_w_rkv[j], rw_w_o[j], rw_decay_w0[j], rw_decay_w1[j],
                               rw_decay_w2[j], rw_iclr_a0[j], rw_iclr_a1[j], rw_iclr_a2[j],
                               rw_gate_g1[j], rw_gate_g2[j], rw_k_k[j], rw_k_a[j], rw_r_k[j],
                               rw_lnx_g[j], rw_lnx_b[j])
        else:
            y = sliding_window_sink_attention(h, sw_w_qkv[j], sw_w_o[j], sw_q_norm_g[j],
                                              sw_k_norm_g[j], sw_sinks[j])
        x = x + g2 * y
        h = modulate(rms_norm(x, norm_g[i, 2]), sh3, sc3)
        x = x + 0.5 * g3 * swiglu(h, ffn_w_in[i, 1], ffn_w_out[i, 1])
    return x
```

```python
import functools
import math

import jax
import jax.numpy as jnp
from jax import lax
from jax.experimental import pallas as pl
from jax.experimental.pallas import tpu as pltpu

F32 = jnp.float32
BF16 = jnp.bfloat16
HIGHEST = lax.Precision.HIGHEST

HEAD_DIM = 64
LANES = 128
NORM_EPS = 1e-6
SUBLN_EPS = 1e-5
RW_LNX_EPS = 64e-5
SW_WINDOW = 128
N_MOD = 9
RW_CHUNK = 64
VMEM_LIMIT = 56 * 1024 * 1024


def _cparams(*sem):
    return pltpu.CompilerParams(dimension_semantics=sem, vmem_limit_bytes=VMEM_LIMIT)


def _sigmoid(x):
    return 1.0 / (1.0 + jnp.exp(-x))


def _dot(a, b):
    return jnp.dot(a, b, preferred_element_type=F32)


def _dot_nt(a, b):
    return lax.dot_general(a, b, (((1,), (1,)), ((), ())), preferred_element_type=F32)


def _norm_mod(x, g, shift, scale):
    ms = jnp.mean(x * x, axis=-1, keepdims=True)
    return (x * lax.rsqrt(ms + NORM_EPS) * g) * (1.0 + scale) + shift


def _const_spec(shape):
    nd = len(shape)
    return pl.BlockSpec(shape, lambda *_: (0,) * nd, pipeline_mode=pl.Buffered(1))


def _group_mats(d):
    grp = jnp.arange(d) // HEAD_DIM
    gsum = (grp[:, None] == jnp.arange(LANES)[None, :]).astype(BF16)
    return gsum, gsum.T


def _group_bcast(v, gexp):
    hi = v.astype(BF16)
    lo = (v - hi.astype(F32)).astype(BF16)
    return _dot(hi, gexp) + _dot(lo, gexp)


def _ada_kernel(c_ref, w_ref, b_ref, o_ref):
    c = c_ref[...]
    cond = c * _sigmoid(c)
    o_ref[...] = jnp.dot(cond, w_ref[...], precision=HIGHEST, preferred_element_type=F32) + b_ref[...]


def _ada(c, ada_w, ada_b):
    L, D, ND = ada_w.shape
    B = c.shape[0]
    return pl.pallas_call(
        _ada_kernel,
        grid=(L, ND // D),
        in_specs=[pl.BlockSpec((B, D), lambda l, j: (0, 0)),
                  pl.BlockSpec((None, D, D), lambda l, j: (l, 0, j)),
                  pl.BlockSpec((None, 1, D), lambda l, j: (l, 0, j))],
        out_specs=pl.BlockSpec((None, B, D), lambda l, j: (l, 0, j)),
        out_shape=jax.ShapeDtypeStruct((L, B, ND), F32),
        compiler_params=_cparams("parallel", "parallel"),
        name="ada_mod",
    )(c, ada_w, ada_b.reshape(L, 1, ND))


def _ffn_kernel(x_ref, mod_ref, g_ref, win_ref, wout_ref, o_ref, *, d_ff, ck):
    x = x_ref[...]
    h = _norm_mod(x, g_ref[...], mod_ref[0:1, :], mod_ref[1:2, :]).astype(BF16)
    acc = None
    for c0 in range(0, d_ff, ck):
        gate = _dot(h, win_ref[:, c0:c0 + ck])
        up = _dot(h, win_ref[:, d_ff + c0:d_ff + c0 + ck])
        act = (gate * _sigmoid(gate) * up).astype(BF16)
        y = _dot(act, wout_ref[c0:c0 + ck, :])
        acc = y if acc is None else acc + y
    o_ref[...] = x + (0.5 * mod_ref[2:3, :]) * acc


def _ffn_chunk(d_ff):
    for n in (2, 1, 4, 11, 22):
        if d_ff % n == 0 and (d_ff // n) % LANES == 0:
            return d_ff // n
    return d_ff


def _ffn(x2d, mod3, g, w_in, w_out, *, tm, tiles_per_batch):
    N, D = x2d.shape
    d_ff = w_out.shape[0]
    kern = functools.partial(_ffn_kernel, d_ff=d_ff, ck=_ffn_chunk(d_ff))
    return pl.pallas_call(
        kern,
        grid=(N // tm,),
        in_specs=[pl.BlockSpec((tm, D), lambda i: (i, 0)),
                  pl.BlockSpec((None, 3, D), lambda i: (i // tiles_per_batch, 0, 0)),
                  _const_spec((1, D)),
                  _const_spec(w_in.shape),
                  _const_spec(w_out.shape)],
        out_specs=pl.BlockSpec((tm, D), lambda i: (i, 0)),
        out_shape=jax.ShapeDtypeStruct((N, D), F32),
        compiler_params=_cparams("parallel"),
        name="ffn_half",
    )(x2d, mod3, g.reshape(1, D), w_in, w_out)


def _out_proj_kernel(*refs, gated):
    if gated:
        y_ref, gate_ref, x_ref, mod_ref, w_ref, o_ref = refs
        y = (y_ref[...].astype(F32) * gate_ref[...].astype(F32)).astype(BF16)
    else:
        y_ref, x_ref, mod_ref, w_ref, o_ref = refs
        y = y_ref[...]
    o_ref[...] = x_ref[...] + mod_ref[2:3, :] * _dot(y, w_ref[...])


def _out_proj(y, gate, x2d, mod3, w_o, *, tm, tiles_per_batch):
    N, D = x2d.shape
    tile = pl.BlockSpec((tm, D), lambda i: (i, 0))
    gated = gate is not None
    ins = [y] + ([gate] if gated else []) + [x2d, mod3, w_o]
    specs = [tile] + ([tile] if gated else []) + [
        tile, pl.BlockSpec((None, 3, D), lambda i: (i // tiles_per_batch, 0, 0)), _const_spec(w_o.shape)]
    return pl.pallas_call(
        functools.partial(_out_proj_kernel, gated=gated),
        grid=(N // tm,),
        in_specs=specs,
        out_specs=tile,
        out_shape=jax.ShapeDtypeStruct((N, D), F32),
        compiler_params=_cparams("parallel"),
        name="out_proj",
    )(*ins)


def _group_rms_scale(y, gsum, gexp):
    ss = _dot((y * y).astype(BF16), gsum)
    return _group_bcast(lax.rsqrt(ss * (1.0 / HEAD_DIM) + NORM_EPS), gexp)


def _da_qkv_kernel(x_ref, mod_ref, g_ref, w_ref, gsum_ref, gexp_ref, qg_ref, kg_ref, q_ref, k_ref, vt_ref):
    D = x_ref.shape[1]
    h = _norm_mod(x_ref[...], g_ref[...], mod_ref[0:1, :], mod_ref[1:2, :]).astype(BF16)
    q = _dot(h, w_ref[:, 0:D])
    q_ref[...] = (q * _group_rms_scale(q, gsum_ref[...], gexp_ref[...]) * qg_ref[...]).astype(BF16)
    k = _dot(h, w_ref[:, D:2 * D])
    k_ref[...] = (k * _group_rms_scale(k, gsum_ref[...], gexp_ref[...]) * kg_ref[...]).astype(BF16)
    v = _dot(h, w_ref[:, 2 * D:3 * D])
    vt_ref[...] = v.T.astype(BF16)


def _da_qkv(x2d, mod3, g, w_qkv, q_gain, k_gain, *, B, T, tm):
    N, D = x2d.shape
    tpb = T // tm
    gsum, gexp = _group_mats(D)
    tile = pl.BlockSpec((tm, D), lambda i: (i, 0))
    return pl.pallas_call(
        _da_qkv_kernel,
        grid=(N // tm,),
        in_specs=[tile,
                  pl.BlockSpec((None, 3, D), lambda i: (i // tpb, 0, 0)),
                  _const_spec((1, D)), _const_spec(w_qkv.shape), _const_spec(gsum.shape),
                  _const_spec(gexp.shape), _const_spec((1, D)), _const_spec((1, D))],
        out_specs=[tile, tile, pl.BlockSpec((None, D, tm), lambda i: (i // tpb, 0, i % tpb))],
        out_shape=[jax.ShapeDtypeStruct((N, D), BF16), jax.ShapeDtypeStruct((N, D), BF16),
                   jax.ShapeDtypeStruct((B, D, T), BF16)],
        compiler_params=_cparams("parallel"),
        name="da_qkv",
    )(x2d, mod3, g.reshape(1, D), w_qkv, gsum, gexp, q_gain, k_gain)


def _da_attn_kernel(lam_ref, sg_ref, q_ref, k_ref, vt_ref, o_ref, acc_ref, *, tq, lambda_init):
    qi = pl.program_id(2)
    q = q_ref[...]
    lane = lax.broadcasted_iota(jnp.int32, (1, LANES), 1)
    zero = jnp.zeros_like(q)
    qm = (jnp.where(lane < HEAD_DIM, q, zero), jnp.where(lane >= HEAD_DIM, q, zero))
    acc_ref[...] = jnp.zeros_like(acc_ref)

    def block(j, carry, masked):
        start = pl.multiple_of(j * tq, tq)
        k = k_ref[pl.ds(start, tq), :]
        vt = vt_ref[:, pl.ds(start, tq)]
        out = []
        for c in range(2):
            m_prev, l_prev = carry[2 * c], carry[2 * c + 1]
            s = _dot_nt(k, qm[c])
            if masked:
                kpos = lax.broadcasted_iota(jnp.int32, (tq, tq), 0)
                qpos = lax.broadcasted_iota(jnp.int32, (tq, tq), 1)
                s = jnp.where(kpos <= qpos, s, -jnp.inf)
            m_new = jnp.maximum(m_prev, jnp.max(s, axis=0, keepdims=True))
            alpha = jnp.exp(m_prev - m_new)
            p = jnp.exp(s - m_new)
            l_new = alpha * l_prev + jnp.sum(p, axis=0, keepdims=True)
            acc_ref[c] = alpha * acc_ref[c] + _dot(vt, p.astype(BF16))
            out += [m_new, l_new]
        return tuple(out)

    neg = jnp.full((1, tq), -jnp.inf, F32)
    zer = jnp.zeros((1, tq), F32)
    carry = lax.fori_loop(0, qi, lambda j, c: block(j, c, False), (neg, zer, neg, zer))
    _, l0, _, l1 = block(qi, carry, True)

    lam = lam_ref[...]
    lam_full = (jnp.exp(jnp.sum(lam[0:1] * lam[1:2], axis=1, keepdims=True))
                - jnp.exp(jnp.sum(lam[2:3] * lam[3:4], axis=1, keepdims=True)) + lambda_init)
    ot = acc_ref[0] / l0 - lam_full * (acc_ref[1] / l1)
    ms = jnp.mean(ot * ot, axis=0, keepdims=True)
    ot = ot * lax.rsqrt(ms + SUBLN_EPS)
    o_ref[...] = (ot.T * sg_ref[...] * (1.0 - lambda_init)).astype(BF16)


def _da_attn(q, k, vt, lambdas, subln_g, *, B, T, D, tq, lambda_init):
    H = D // LANES
    q3, k3 = q.reshape(B, T, D), k.reshape(B, T, D)
    out = pl.pallas_call(
        functools.partial(_da_attn_kernel, tq=tq, lambda_init=lambda_init),
        grid=(B, H, T // tq),
        in_specs=[pl.BlockSpec((4, HEAD_DIM), lambda b, h, i: (0, 0)),
                  pl.BlockSpec((1, LANES), lambda b, h, i: (0, 0)),
                  pl.BlockSpec((None, tq, LANES), lambda b, h, i: (b, i, h)),
                  pl.BlockSpec((None, T, LANES), lambda b, h, i: (b, 0, h)),
                  pl.BlockSpec((None, LANES, T), lambda b, h, i: (b, h, 0))],
        out_specs=pl.BlockSpec((None, tq, LANES), lambda b, h, i: (b, i, h)),
        out_shape=jax.ShapeDtypeStruct((B, T, D), BF16),
        scratch_shapes=[pltpu.VMEM((2, LANES, tq), F32)],
        compiler_params=_cparams("parallel", "parallel", "arbitrary"),
        name="da_attn",
    )(lambdas, subln_g.reshape(1, LANES), q3, k3, vt)
    return out.reshape(B * T, D)


def _sw_qkv_kernel(x_ref, mod_ref, g_ref, w_ref, gsum_ref, gexp_ref, ksum_ref, kexp_ref, qg_ref, kg_ref,
                   q_ref, k_ref, v_ref):
    D = x_ref.shape[1]
    h = _norm_mod(x_ref[...], g_ref[...], mod_ref[0:1, :], mod_ref[1:2, :]).astype(BF16)
    q = _dot(h, w_ref[:, 0:D])
    q_ref[...] = (q * _group_rms_scale(q, gsum_ref[...], gexp_ref[...]) * qg_ref[...]).astype(BF16)
    kv = k_ref.shape[1]
    k = _dot(h, w_ref[:, D:D + kv])
    k_ref[...] = (k * _group_rms_scale(k, ksum_ref[...], kexp_ref[...]) * kg_ref[...]).astype(BF16)
    v_ref[...] = _dot(h, w_ref[:, D + kv:D + 2 * kv]).astype(BF16)


def _sw_qkv(x2d, mod3, g, w_qkv, q_gain, k_gain, *, tm, tiles_per_batch):
    N, D = x2d.shape
    kv = (w_qkv.shape[1] - D) // 2
    gsum, gexp = _group_mats(D)
    ksum, kexp = _group_mats(kv)
    tile = pl.BlockSpec((tm, D), lambda i: (i, 0))
    kvtile = pl.BlockSpec((tm, kv), lambda i: (i, 0))
    return pl.pallas_call(
        _sw_qkv_kernel,
        grid=(N // tm,),
        in_specs=[tile, pl.BlockSpec((None, 3, D), lambda i: (i // tiles_per_batch, 0, 0)),
                  _const_spec((1, D)), _const_spec(w_qkv.shape), _const_spec(gsum.shape),
                  _const_spec(gexp.shape), _const_spec(ksum.shape), _const_spec(kexp.shape),
                  _const_spec((1, D)), _const_spec((1, kv))],
        out_specs=[tile, kvtile, kvtile],
        out_shape=[jax.ShapeDtypeStruct((N, D), BF16), jax.ShapeDtypeStruct((N, kv), BF16),
                   jax.ShapeDtypeStruct((N, kv), BF16)],
        compiler_params=_cparams("parallel"),
        name="sw_qkv",
    )(x2d, mod3, g.reshape(1, D), w_qkv, gsum, gexp, ksum, kexp, q_gain, k_gain)


def _sw_attn_kernel(sink_ref, q_ref, k_ref, v_ref, x_ref, mod_ref, wo_ref, o_ref, att_ref, *, tq, group):
    i = pl.program_id(1)
    D = q_ref.shape[1]
    nk = tq + SW_WINDOW
    k0 = pl.multiple_of(jnp.maximum(i * tq - SW_WINDOW, 0), SW_WINDOW)
    qpos = i * tq + lax.broadcasted_iota(jnp.int32, (tq, nk), 0)
    kpos = k0 + lax.broadcasted_iota(jnp.int32, (tq, nk), 1)
    valid = jnp.logical_and(kpos <= qpos, kpos > qpos - SW_WINDOW)
    lane = lax.broadcasted_iota(jnp.int32, (1, LANES), 1)
    for blk in range(D // LANES):
        kvh = (2 * blk) // group
        k = k_ref[pl.ds(k0, nk), kvh * LANES:(kvh + 1) * LANES]
        v = v_ref[pl.ds(k0, nk), kvh * LANES:(kvh + 1) * LANES]
        q = q_ref[:, blk * LANES:(blk + 1) * LANES]
        outs = []
        for c in range(2):
            sel = (lane < HEAD_DIM) if c == 0 else (lane >= HEAD_DIM)
            s = _dot_nt(jnp.where(sel, q, jnp.zeros_like(q)), k)
            s = jnp.where(valid, s, -jnp.inf)
            snk = sink_ref[2 * blk + c]
            m = jnp.maximum(jnp.max(s, axis=1, keepdims=True), snk)
            p = jnp.exp(s - m)
            denom = jnp.sum(p, axis=1, keepdims=True) + jnp.exp(snk - m)
            outs.append(_dot((p / denom).astype(BF16), v))
        att_ref[:, blk * LANES:(blk + 1) * LANES] = jnp.where(lane < HEAD_DIM, outs[0], outs[1]).astype(BF16)
    o_ref[...] = x_ref[...] + mod_ref[2:3, :] * _dot(att_ref[...], wo_ref[...])


def _sw_attn(q, kd, vd, sinks, x2d, mod3, w_o, *, B, T, tq, group):
    N, D = x2d.shape
    nq = T // tq
    kvd = kd.shape[1]
    tile = pl.BlockSpec((tq, D), lambda b, i: (b * nq + i, 0))
    seq = pl.BlockSpec((None, T, kvd), lambda b, i: (b, 0, 0))
    return pl.pallas_call(
        functools.partial(_sw_attn_kernel, tq=tq, group=group),
        grid=(B, nq),
        in_specs=[pl.BlockSpec(memory_space=pltpu.SMEM), tile, seq, seq, tile,
                  pl.BlockSpec((None, 3, D), lambda b, i: (b, 0, 0)), _const_spec(w_o.shape)],
        out_specs=tile,
        out_shape=jax.ShapeDtypeStruct((N, D), F32),
        scratch_shapes=[pltpu.VMEM((tq, D), BF16)],
        compiler_params=_cparams("parallel", "arbitrary"),
        name="sw_attn",
    )(sinks, q, kd.reshape(B, T, kvd), vd.reshape(B, T, kvd), x2d, mod3, w_o)


def _rw_prep_kernel(x_ref, xp_ref, mod_ref, g_ref, mu_ref, wrkv_ref, w0_ref, w1_ref, w2_ref, a0_ref, a1_ref,
                    a2_ref, g1_ref, g2_ref, kk_ref, ka_ref, gsum_ref, gexp_ref,
                    r_out, lw_out, k_out, v_out, kk_out, b_out, g_out, *, tiles_per_batch):
    D = x_ref.shape[1]
    tm = x_ref.shape[0]
    shift, scale, gn = mod_ref[0:1, :], mod_ref[1:2, :], g_ref[...]
    h = _norm_mod(x_ref[...], gn, shift, scale)
    hp = _norm_mod(xp_ref[...], gn, shift, scale)[7:8, :]
    hp = jnp.where(pl.program_id(0) % tiles_per_batch == 0, jnp.zeros_like(hp), hp)
    row = lax.broadcasted_iota(jnp.int32, (tm, 1), 0)
    xx = jnp.where(row == 0, hp, pltpu.roll(h, 1, axis=0)) - h

    def mixed(n):
        return (h + xx * mu_ref[n:n + 1, :]).astype(BF16)

    r = _dot(mixed(0), wrkv_ref[0])
    k = _dot(mixed(1), wrkv_ref[1])
    v = _dot(mixed(2), wrkv_ref[2])
    d = _dot(jnp.tanh(_dot(mixed(3), w1_ref[...])).astype(BF16), w2_ref[...]) + w0_ref[...]
    z = -d
    softplus = jnp.maximum(z, 0.0) + jnp.log(1.0 + jnp.exp(-jnp.abs(z)))
    lw_out[...] = -jnp.exp(-softplus - 0.5)
    a = _sigmoid(a0_ref[...] + _dot(_dot(mixed(4), a1_ref[...]).astype(BF16), a2_ref[...]))
    g_out[...] = _dot(_sigmoid(_dot(mixed(5), g1_ref[...])).astype(BF16), g2_ref[...]).astype(BF16)
    kk = k * kk_ref[...]
    ss = _dot((kk * kk).astype(BF16), gsum_ref[...])
    kk = kk * _group_bcast(1.0 / jnp.maximum(jnp.sqrt(ss), 1e-12), gexp_ref[...])
    r_out[...] = r.astype(BF16)
    k_out[...] = (k * (1.0 + (a - 1.0) * ka_ref[...])).astype(BF16)
    v_out[...] = v.astype(BF16)
    kk_out[...] = kk.astype(BF16)
    b_out[...] = (kk * a).astype(BF16)


def _pad_to(w, axis, n):
    pad = [(0, 0)] * w.ndim
    pad[axis] = (0, n - w.shape[axis])
    return jnp.pad(w, pad)


def _round_up(n, m):
    return -(-n // m) * m


def _rw_prep(x2d, mod3, g, p, *, tm, tiles_per_batch):
    N, D = x2d.shape
    gsum, gexp = _group_mats(D)
    row = lambda t: t.reshape(1, D)
    lora = lambda w1, w2: (_pad_to(w1, 1, _round_up(w1.shape[1], LANES)).astype(BF16),
                           _pad_to(w2, 0, _round_up(w2.shape[0], LANES)).astype(BF16))
    w1, w2 = lora(p["decay_w1"], p["decay_w2"])
    a1, a2 = lora(p["iclr_a1"], p["iclr_a2"])
    g1, g2 = lora(p["gate_g1"], p["gate_g2"])
    consts = [row(g), p["mu"], p["w_rkv"].astype(BF16), row(p["decay_w0"]), w1, w2, row(p["iclr_a0"]), a1, a2,
              g1, g2, row(p["k_k"]), row(p["k_a"]), gsum, gexp]
    tile = pl.BlockSpec((tm, D), lambda i: (i, 0))
    outs = [jax.ShapeDtypeStruct((N, D), BF16)] * 7
    outs[1] = jax.ShapeDtypeStruct((N, D), F32)
    return pl.pallas_call(
        functools.partial(_rw_prep_kernel, tiles_per_batch=tiles_per_batch),
        grid=(N // tm,),
        in_specs=[tile,
                  pl.BlockSpec((8, D), lambda i: (jnp.maximum(i * (tm // 8) - 1, 0), 0)),
                  pl.BlockSpec((None, 3, D), lambda i: (i // tiles_per_batch, 0, 0))]
                 + [_const_spec(c.shape) for c in consts],
        out_specs=[tile] * 7,
        out_shape=outs,
        compiler_params=_cparams("parallel"),
        name="rw_prep",
    )(x2d, x2d, mod3, *consts)


def _stack_heads(x, m0):
    z = jnp.zeros_like(x)
    return jnp.concatenate([jnp.where(m0, x, z), jnp.where(m0, z, x)], axis=0)


def _rw_scan_kernel(r_ref, lw_ref, k_ref, v_ref, kk_ref, b_ref, rk_ref, lng_ref, lnb_ref, o_ref, st_ref, os_ref,
                    *, tt):
    C = RW_CHUNK
    C2 = 2 * C

    @pl.when(pl.program_id(2) == 0)
    def _():
        st_ref[...] = jnp.zeros_like(st_ref)

    lane = lax.broadcasted_iota(jnp.int32, (1, LANES), 1)
    m0 = lane < HEAD_DIM
    ti = lax.broadcasted_iota(jnp.int32, (C, C), 0)
    tj = lax.broadcasted_iota(jnp.int32, (C, C), 1)
    tri = (ti >= tj).astype(F32)
    ri = lax.broadcasted_iota(jnp.int32, (2 * C2, 2 * C2), 0)
    ci = lax.broadcasted_iota(jnp.int32, (2 * C2, 2 * C2), 1)
    rt, ct = ri % C, ci % C
    causal = rt - ct + (ri >= C2).astype(jnp.int32) > 0
    eye = (lax.broadcasted_iota(jnp.int32, (C2, C2), 0) == lax.broadcasted_iota(jnp.int32, (C2, C2), 1))
    eye_f = eye.astype(F32)
    si = lax.broadcasted_iota(jnp.int32, (C2, C2), 0)
    sj = lax.broadcasted_iota(jnp.int32, (C2, C2), 1)
    merge_masks = []
    s = 1
    while s < C:
        merge_masks.append(jnp.logical_and(si // (2 * s) == sj // (2 * s),
                                           jnp.logical_and(si % (2 * s) >= s, sj % (2 * s) < s)))
        s *= 2
    zeros_sq = jnp.zeros((C2, C2), BF16)

    st = st_ref[...]
    for c in range(tt // C):
        rows = slice(c * C, (c + 1) * C)
        lw = lw_ref[rows, :]
        r = r_ref[rows, :].astype(F32)
        k = k_ref[rows, :].astype(F32)
        v = v_ref[rows, :]
        kk = kk_ref[rows, :].astype(F32)
        b = b_ref[rows, :].astype(F32)
        cum = jnp.dot(tri, lw, precision=HIGHEST, preferred_element_type=F32)
        tot = cum[C - 1:C, :]
        e_neg = jnp.exp(-cum)
        r_t = _stack_heads(r * jnp.exp(cum), m0).astype(BF16)
        a_t = _stack_heads(-kk * jnp.exp(cum - lw), m0).astype(BF16)
        b_t = _stack_heads(b * e_neg, m0).astype(BF16)
        k_t = _stack_heads(k * e_neg, m0).astype(BF16)
        e_end = jnp.exp(tot - cum)
        b_h = _stack_heads(b * e_end, m0)
        k_h = _stack_heads(k * e_end, m0)
        v_s = _stack_heads(v, m0)

        aa = _dot_nt(jnp.concatenate([a_t, r_t], axis=0), jnp.concatenate([b_t, k_t], axis=0))
        aa = jnp.where(causal, aa, 0.0)
        a_ab, a_ak = aa[:C2, :C2], aa[:C2, C2:]
        a_r = aa[C2:, :].astype(BF16)

        t_inv = eye_f + jnp.where(merge_masks[0], a_ab, 0.0)
        for mask in merge_masks[1:]:
            tb = t_inv.astype(BF16)
            lba = jnp.where(mask, a_ab, 0.0).astype(BF16)
            t_inv = t_inv + _dot(_dot(tb, lba).astype(BF16), tb)
        akv = _dot(a_ak.astype(BF16), v_s)
        au = _dot(t_inv.astype(BF16), jnp.concatenate([a_t, akv.astype(BF16)], axis=1))
        rhs = jnp.concatenate([au.astype(BF16), jnp.concatenate([zeros_sq, v_s], axis=1)], axis=0)
        lhs = jnp.concatenate([jnp.concatenate([b_h.T, k_h.T], axis=1).astype(BF16), a_r], axis=0)
        big = _dot(lhs, rhs)
        g_m, n_m = big[:C2, :C2], big[:C2, C2:]
        q_m = big[C2:, :C2] + r_t.astype(F32)
        o0 = big[C2:, C2:]

        stb = st.astype(BF16)
        res = _dot(jnp.concatenate([q_m, g_m], axis=0).astype(BF16), stb)
        o_st = res[:C2] + o0
        os_ref[rows, :] = o_st[:C] + o_st[C:]
        p_col = jnp.sum(jnp.where(eye, jnp.exp(tot), 0.0), axis=1, keepdims=True)
        st = p_col * st + res[C2:] + n_m
    st_ref[...] = st

    o = os_ref[...]
    blk = ((lax.broadcasted_iota(jnp.int32, (LANES, LANES), 0) < HEAD_DIM)
           == (lax.broadcasted_iota(jnp.int32, (LANES, LANES), 1) < HEAD_DIM)).astype(BF16)

    def head_sum(x):
        hi = x.astype(BF16)
        lo = (x - hi.astype(F32)).astype(BF16)
        return _dot(hi, blk) + _dot(lo, blk)

    mean = head_sum(o) * (1.0 / HEAD_DIM)
    cen = o - mean
    var = head_sum(cen * cen) * (1.0 / HEAD_DIM)
    y = cen * lax.rsqrt(var + RW_LNX_EPS) * lng_ref[...] + lnb_ref[...]
    r = r_ref[...].astype(F32)
    k = k_ref[...].astype(F32)
    bonus = head_sum(r * k * rk_ref[...]) * v_ref[...].astype(F32)
    o_ref[...] = (y + bonus).astype(BF16)


def _rw_scan(r, lw, k, v, kk, b, r_k, lnx_g, lnx_b, *, B, T, D, tt):
    hp = D // LANES
    blk = pl.BlockSpec((None, tt, LANES), lambda bi, h, t: (bi, t, h))
    vec = pl.BlockSpec((1, LANES), lambda bi, h, t: (0, h))
    r3 = lambda t: t.reshape(B, T, D)
    out = pl.pallas_call(
        functools.partial(_rw_scan_kernel, tt=tt),
        grid=(B, hp, T // tt),
        in_specs=[blk] * 6 + [vec] * 3,
        out_specs=blk,
        out_shape=jax.ShapeDtypeStruct((B, T, D), BF16),
        scratch_shapes=[pltpu.VMEM((LANES, LANES), F32), pltpu.VMEM((tt, LANES), F32)],
        compiler_params=_cparams("parallel", "parallel", "arbitrary"),
        name="rw_scan",
    )(r3(r), r3(lw), r3(k), r3(v), r3(kk), r3(b), r_k.reshape(1, D), lnx_g.reshape(1, D), lnx_b.reshape(1, D))
    return out.reshape(B * T, D)


def _diff_lambda_init(layer):
    return 0.8 - 0.6 * math.exp(-0.3 * layer)


def _pick_tile(n, pref):
    t = min(n, pref)
    while n % t:
        t //= 2
    return t


def kernel(x, c, norm_g, ada_w, ada_b, ffn_w_in, ffn_w_out, da_w_qkv, da_w_o, da_q_norm_g, da_k_norm_g, da_lambda, da_subln_g, rw_mu, rw_w_rkv, rw_w_o, rw_decay_w0, rw_decay_w1, rw_decay_w2, rw_iclr_a0, rw_iclr_a1, rw_iclr_a2, rw_gate_g1, rw_gate_g2, rw_k_k, rw_k_a, rw_r_k, rw_lnx_g, rw_lnx_b, sw_w_qkv, sw_w_o, sw_q_norm_g, sw_k_norm_g, sw_sinks):
    B, T, D = x.shape
    L = norm_g.shape[0]
    tm = _pick_tile(T, 512)
    tpb = T // tm
    kw = dict(tm=tm, tiles_per_batch=tpb)
    scale = HEAD_DIM ** -0.5

    mod = _ada(c, ada_w, ada_b).reshape(L, B, 3, 3, D)
    x2d = x.reshape(B * T, D)
    for i in range(L):
        kind, j = i % 3, i // 3
        x2d = _ffn(x2d, mod[i, :, 0], norm_g[i, 0], ffn_w_in[i, 0].astype(BF16), ffn_w_out[i, 0].astype(BF16), **kw)
        m2 = mod[i, :, 1]
        if kind == 0:
            qg = jnp.tile(da_q_norm_g[j] * scale, D // HEAD_DIM).reshape(1, D)
            kg = jnp.tile(da_k_norm_g[j], D // HEAD_DIM).reshape(1, D)
            q, k, vt = _da_qkv(x2d, m2, norm_g[i, 1], da_w_qkv[j].astype(BF16), qg, kg, B=B, T=T, tm=tm)
            o = _da_attn(q, k, vt, da_lambda[j], da_subln_g[j], B=B, T=T, D=D, tq=_pick_tile(T, 512),
                         lambda_init=_diff_lambda_init(i))
            x2d = _out_proj(o, None, x2d, m2, da_w_o[j].astype(BF16), **kw)
        elif kind == 1:
            p = dict(mu=rw_mu[j], w_rkv=rw_w_rkv[j], decay_w0=rw_decay_w0[j], decay_w1=rw_decay_w1[j],
                     decay_w2=rw_decay_w2[j], iclr_a0=rw_iclr_a0[j], iclr_a1=rw_iclr_a1[j], iclr_a2=rw_iclr_a2[j],
                     gate_g1=rw_gate_g1[j], gate_g2=rw_gate_g2[j], k_k=rw_k_k[j], k_a=rw_k_a[j])
            r, lw, k, v, kk, b, g = _rw_prep(x2d, m2, norm_g[i, 1], p, **kw)
            y = _rw_scan(r, lw, k, v, kk, b, rw_r_k[j], rw_lnx_g[j], rw_lnx_b[j], B=B, T=T, D=D,
                         tt=_pick_tile(T, 512))
            x2d = _out_proj(y, g, x2d, m2, rw_w_o[j].astype(BF16), **kw)
        else:
            kv = (sw_w_qkv.shape[-1] - D) // 2
            n_kv = kv // HEAD_DIM
            qg = jnp.tile(sw_q_norm_g[j] * scale, D // HEAD_DIM).reshape(1, D)
            kg = jnp.tile(sw_k_norm_g[j], n_kv).reshape(1, kv)
            q, k, v = _sw_qkv(x2d, m2, norm_g[i, 1], sw_w_qkv[j].astype(BF16), qg, kg, **kw)
            dup = lambda t: jnp.repeat(t.reshape(B * T, n_kv, 1, HEAD_DIM), 2, axis=2).reshape(B * T, 2 * kv)
            x2d = _sw_attn(q, dup(k), dup(v), sw_sinks[j], x2d, m2, sw_w_o[j].astype(BF16), B=B, T=T,
                           tq=_pick_tile(T, 256), group=(D // HEAD_DIM) // n_kv)
        x2d = _ffn(x2d, mod[i, :, 2], norm_g[i, 2], ffn_w_in[i, 1].astype(BF16), ffn_w_out[i, 1].astype(BF16), **kw)
    return x2d.reshape(B, T, D)
```

```python
import functools
import math

import jax
import jax.numpy as jnp
from jax import lax
from jax.experimental import pallas as pl
from jax.experimental.pallas import tpu as pltpu

F32 = jnp.float32
BF16 = jnp.bfloat16
HIGHEST = lax.Precision.HIGHEST

HEAD_DIM = 64
LANES = 128
NORM_EPS = 1e-6
SUBLN_EPS = 1e-5
RW_LNX_EPS = 64e-5
SW_WINDOW = 128
N_MOD = 9
RW_GROUP = 8
RW_CHUNK = 64
VMEM_LIMIT = 56 * 1024 * 1024


def _cparams(*sem):
    return pltpu.CompilerParams(dimension_semantics=sem, vmem_limit_bytes=VMEM_LIMIT)


def _sigmoid(x):
    return 1.0 / (1.0 + jnp.exp(-x))


def _dot(a, b):
    return jnp.dot(a, b, preferred_element_type=F32)


def _dot_nt(a, b):
    return lax.dot_general(a, b, (((1,), (1,)), ((), ())), preferred_element_type=F32)


def _norm_mod(x, g, shift, scale):
    ms = jnp.mean(x * x, axis=-1, keepdims=True)
    return (x * lax.rsqrt(ms + NORM_EPS) * g) * (1.0 + scale) + shift


def _const_spec(shape):
    nd = len(shape)
    return pl.BlockSpec(shape, lambda *_: (0,) * nd, pipeline_mode=pl.Buffered(1))


def _group_mats(d):
    grp = jnp.arange(d) // HEAD_DIM
    gsum = (grp[:, None] == jnp.arange(LANES)[None, :]).astype(BF16)
    return gsum, gsum.T


def _group_bcast(v, gexp):
    hi = v.astype(BF16)
    lo = (v - hi.astype(F32)).astype(BF16)
    return _dot(hi, gexp) + _dot(lo, gexp)


def _ada_kernel(c_ref, w_ref, b_ref, o_ref):
    c = c_ref[...]
    cond = c * _sigmoid(c)
    o_ref[...] = jnp.dot(cond, w_ref[...], precision=HIGHEST, preferred_element_type=F32) + b_ref[...]


def _ada(c, ada_w, ada_b):
    L, D, ND = ada_w.shape
    B = c.shape[0]
    return pl.pallas_call(
        _ada_kernel,
        grid=(L, ND // D),
        in_specs=[pl.BlockSpec((B, D), lambda l, j: (0, 0)),
                  pl.BlockSpec((None, D, D), lambda l, j: (l, 0, j)),
                  pl.BlockSpec((None, 1, D), lambda l, j: (l, 0, j))],
        out_specs=pl.BlockSpec((None, B, D), lambda l, j: (l, 0, j)),
        out_shape=jax.ShapeDtypeStruct((L, B, ND), F32),
        compiler_params=_cparams("parallel", "parallel"),
        name="ada_mod",
    )(c, ada_w, ada_b.reshape(L, 1, ND))


def _ffn_kernel(x_ref, mod_ref, g_ref, win_ref, wout_ref, o_ref, *, d_ff, ck):
    x = x_ref[...]
    h = _norm_mod(x, g_ref[...], mod_ref[0:1, :], mod_ref[1:2, :]).astype(BF16)
    acc = None
    for c0 in range(0, d_ff, ck):
        gate = _dot(h, win_ref[:, c0:c0 + ck])
        up = _dot(h, win_ref[:, d_ff + c0:d_ff + c0 + ck])
        act = (gate * _sigmoid(gate) * up).astype(BF16)
        y = _dot(act, wout_ref[c0:c0 + ck, :])
        acc = y if acc is None else acc + y
    o_ref[...] = x + (0.5 * mod_ref[2:3, :]) * acc


def _ffn_chunk(d_ff):
    for n in (2, 1, 4, 11, 22):
        if d_ff % n == 0 and (d_ff // n) % LANES == 0:
            return d_ff // n
    return d_ff


def _ffn(x2d, mod3, g, w_in, w_out, *, tm, tiles_per_batch):
    N, D = x2d.shape
    d_ff = w_out.shape[0]
    kern = functools.partial(_ffn_kernel, d_ff=d_ff, ck=_ffn_chunk(d_ff))
    return pl.pallas_call(
        kern,
        grid=(N // tm,),
        in_specs=[pl.BlockSpec((tm, D), lambda i: (i, 0)),
                  pl.BlockSpec((None, 3, D), lambda i: (i // tiles_per_batch, 0, 0)),
                  _const_spec((1, D)),
                  _const_spec(w_in.shape),
                  _const_spec(w_out.shape)],
        out_specs=pl.BlockSpec((tm, D), lambda i: (i, 0)),
        out_shape=jax.ShapeDtypeStruct((N, D), F32),
        compiler_params=_cparams("parallel"),
        name="ffn_half",
    )(x2d, mod3, g.reshape(1, D), w_in, w_out)


def _out_proj_kernel(*refs, gated):
    if gated:
        y_ref, gate_ref, x_ref, mod_ref, w_ref, o_ref = refs
        y = (y_ref[...].astype(F32) * gate_ref[...].astype(F32)).astype(BF16)
    else:
        y_ref, x_ref, mod_ref, w_ref, o_ref = refs
        y = y_ref[...]
    o_ref[...] = x_ref[...] + mod_ref[2:3, :] * _dot(y, w_ref[...])


def _out_proj(y, gate, x2d, mod3, w_o, *, tm, tiles_per_batch):
    N, D = x2d.shape
    tile = pl.BlockSpec((tm, D), lambda i: (i, 0))
    gated = gate is not None
    ins = [y] + ([gate] if gated else []) + [x2d, mod3, w_o]
    specs = [tile] + ([tile] if gated else []) + [
        tile, pl.BlockSpec((None, 3, D), lambda i: (i // tiles_per_batch, 0, 0)), _const_spec(w_o.shape)]
    return pl.pallas_call(
        functools.partial(_out_proj_kernel, gated=gated),
        grid=(N // tm,),
        in_specs=specs,
        out_specs=tile,
        out_shape=jax.ShapeDtypeStruct((N, D), F32),
        compiler_params=_cparams("parallel"),
        name="out_proj",
    )(*ins)


def _group_rms_scale(y, gsum, gexp):
    ss = _dot((y * y).astype(BF16), gsum)
    return _group_bcast(lax.rsqrt(ss * (1.0 / HEAD_DIM) + NORM_EPS), gexp)


def _da_qkv_kernel(x_ref, mod_ref, g_ref, w_ref, gsum_ref, gexp_ref, qg_ref, kg_ref, q_ref, k_ref, vt_ref):
    D = x_ref.shape[1]
    h = _norm_mod(x_ref[...], g_ref[...], mod_ref[0:1, :], mod_ref[1:2, :]).astype(BF16)
    q = _dot(h, w_ref[:, 0:D])
    q_ref[...] = (q * _group_rms_scale(q, gsum_ref[...], gexp_ref[...]) * qg_ref[...]).astype(BF16)
    k = _dot(h, w_ref[:, D:2 * D])
    k_ref[...] = (k * _group_rms_scale(k, gsum_ref[...], gexp_ref[...]) * kg_ref[...]).astype(BF16)
    v = _dot(h, w_ref[:, 2 * D:3 * D])
    vt_ref[...] = v.T.astype(BF16)


def _da_qkv(x2d, mod3, g, w_qkv, q_gain, k_gain, *, B, T, tm):
    N, D = x2d.shape
    tpb = T // tm
    gsum, gexp = _group_mats(D)
    tile = pl.BlockSpec((tm, D), lambda i: (i, 0))
    return pl.pallas_call(
        _da_qkv_kernel,
        grid=(N // tm,),
        in_specs=[tile,
                  pl.BlockSpec((None, 3, D), lambda i: (i // tpb, 0, 0)),
                  _const_spec((1, D)), _const_spec(w_qkv.shape), _const_spec(gsum.shape),
                  _const_spec(gexp.shape), _const_spec((1, D)), _const_spec((1, D))],
        out_specs=[tile, tile, pl.BlockSpec((None, D, tm), lambda i: (i // tpb, 0, i % tpb))],
        out_shape=[jax.ShapeDtypeStruct((N, D), BF16), jax.ShapeDtypeStruct((N, D), BF16),
                   jax.ShapeDtypeStruct((B, D, T), BF16)],
        compiler_params=_cparams("parallel"),
        name="da_qkv",
    )(x2d, mod3, g.reshape(1, D), w_qkv, gsum, gexp, q_gain, k_gain)


def _da_attn_kernel(lam_ref, sg_ref, q_ref, k_ref, vt_ref, o_ref, acc_ref, l_ref, sa_ref, sb_ref,
                    *, tq, lambda_init):
    qi = pl.program_id(2)
    q = q_ref[...]
    lane = lax.broadcasted_iota(jnp.int32, (1, LANES), 1)
    zero = jnp.zeros_like(q)
    qm = (jnp.where(lane < HEAD_DIM, q, zero), jnp.where(lane >= HEAD_DIM, q, zero))
    acc_ref[...] = jnp.zeros_like(acc_ref)

    def scores_into(s_ref, j):
        k = k_ref[pl.ds(pl.multiple_of(j * tq, tq), tq), :]
        for c in range(2):
            s_ref[c] = _dot_nt(k, qm[c])

    def consume(s_ref, j, carry, masked):
        vt = vt_ref[:, pl.ds(pl.multiple_of(j * tq, tq), tq)]
        out = []
        for c in range(2):
            m_prev, l_prev = carry[2 * c], carry[2 * c + 1]
            s = s_ref[c]
            if masked:
                kpos = lax.broadcasted_iota(jnp.int32, (tq, tq), 0)
                qpos = lax.broadcasted_iota(jnp.int32, (tq, tq), 1)
                s = jnp.where(kpos <= qpos, s, -jnp.inf)
            m_new = jnp.maximum(m_prev, jnp.max(s, axis=0, keepdims=True))
            alpha = jnp.exp2(m_prev - m_new)
            p = jnp.exp2(s - m_new)
            l_new = alpha * l_prev + jnp.sum(p, axis=0, keepdims=True)
            acc_ref[c] = alpha * acc_ref[c] + _dot(vt, p.astype(BF16))
            out += [m_new, l_new]
        return tuple(out)

    def pair(i, carry):
        scores_into(sb_ref, 2 * i + 1)
        carry = consume(sa_ref, 2 * i, carry, False)
        scores_into(sa_ref, 2 * i + 2)
        return consume(sb_ref, 2 * i + 1, carry, False)

    neg = jnp.full((1, tq), -jnp.inf, F32)
    zer = jnp.zeros((1, tq), F32)
    scores_into(sa_ref, 0)
    carry = lax.fori_loop(0, qi // 2, pair, (neg, zer, neg, zer))

    def finish(carry):
        l_ref[0:1, :] = carry[1]
        l_ref[1:2, :] = carry[3]

    @pl.when(qi % 2 == 0)
    def _():
        finish(consume(sa_ref, qi, carry, True))

    @pl.when(qi % 2 == 1)
    def _():
        scores_into(sb_ref, qi)
        finish(consume(sb_ref, qi, consume(sa_ref, qi - 1, carry, False), True))

    l0, l1 = l_ref[0:1, :], l_ref[1:2, :]
    lam = lam_ref[...]
    lam_full = (jnp.exp(jnp.sum(lam[0:1] * lam[1:2], axis=1, keepdims=True))
                - jnp.exp(jnp.sum(lam[2:3] * lam[3:4], axis=1, keepdims=True)) + lambda_init)
    ot = acc_ref[0] / l0 - lam_full * (acc_ref[1] / l1)
    ms = jnp.mean(ot * ot, axis=0, keepdims=True)
    ot = ot * lax.rsqrt(ms + SUBLN_EPS)
    o_ref[...] = (ot.T * sg_ref[...] * (1.0 - lambda_init)).astype(BF16)


def _da_attn(q, k, vt, lambdas, subln_g, *, B, T, D, tq, lambda_init):
    H = D // LANES
    q3, k3 = q.reshape(B, T, D), k.reshape(B, T, D)
    out = pl.pallas_call(
        functools.partial(_da_attn_kernel, tq=tq, lambda_init=lambda_init),
        grid=(B, H, T // tq),
        in_specs=[pl.BlockSpec((4, HEAD_DIM), lambda b, h, i: (0, 0)),
                  pl.BlockSpec((1, LANES), lambda b, h, i: (0, 0)),
                  pl.BlockSpec((None, tq, LANES), lambda b, h, i: (b, i, h)),
                  pl.BlockSpec((None, T, LANES), lambda b, h, i: (b, 0, h)),
                  pl.BlockSpec((None, LANES, T), lambda b, h, i: (b, h, 0))],
        out_specs=pl.BlockSpec((None, tq, LANES), lambda b, h, i: (b, i, h)),
        out_shape=jax.ShapeDtypeStruct((B, T, D), BF16),
        scratch_shapes=[pltpu.VMEM((2, LANES, tq), F32), pltpu.VMEM((2, tq), F32),
                        pltpu.VMEM((2, tq, tq), F32), pltpu.VMEM((2, tq, tq), F32)],
        compiler_params=_cparams("parallel", "parallel", "arbitrary"),
        name="da_attn",
    )(lambdas, subln_g.reshape(1, LANES), q3, k3, vt)
    return out.reshape(B * T, D)


def _sw_qkv_kernel(x_ref, mod_ref, g_ref, w_ref, gsum_ref, gexp_ref, ksum_ref, kexp_ref, qg_ref, kg_ref,
                   q_ref, k_ref, v_ref):
    D = x_ref.shape[1]
    h = _norm_mod(x_ref[...], g_ref[...], mod_ref[0:1, :], mod_ref[1:2, :]).astype(BF16)
    q = _dot(h, w_ref[:, 0:D])
    q_ref[...] = (q * _group_rms_scale(q, gsum_ref[...], gexp_ref[...]) * qg_ref[...]).astype(BF16)
    kv = k_ref.shape[1]
    k = _dot(h, w_ref[:, D:D + kv])
    k_ref[...] = (k * _group_rms_scale(k, ksum_ref[...], kexp_ref[...]) * kg_ref[...]).astype(BF16)
    v_ref[...] = _dot(h, w_ref[:, D + kv:D + 2 * kv]).astype(BF16)


def _sw_qkv(x2d, mod3, g, w_qkv, q_gain, k_gain, *, tm, tiles_per_batch):
    N, D = x2d.shape
    kv = (w_qkv.shape[1] - D) // 2
    gsum, gexp = _group_mats(D)
    ksum, kexp = _group_mats(kv)
    tile = pl.BlockSpec((tm, D), lambda i: (i, 0))
    kvtile = pl.BlockSpec((tm, kv), lambda i: (i, 0))
    return pl.pallas_call(
        _sw_qkv_kernel,
        grid=(N // tm,),
        in_specs=[tile, pl.BlockSpec((None, 3, D), lambda i: (i // tiles_per_batch, 0, 0)),
                  _const_spec((1, D)), _const_spec(w_qkv.shape), _const_spec(gsum.shape),
                  _const_spec(gexp.shape), _const_spec(ksum.shape), _const_spec(kexp.shape),
                  _const_spec((1, D)), _const_spec((1, kv))],
        out_specs=[tile, kvtile, kvtile],
        out_shape=[jax.ShapeDtypeStruct((N, D), BF16), jax.ShapeDtypeStruct((N, kv), BF16),
                   jax.ShapeDtypeStruct((N, kv), BF16)],
        compiler_params=_cparams("parallel"),
        name="sw_qkv",
    )(x2d, mod3, g.reshape(1, D), w_qkv, gsum, gexp, ksum, kexp, q_gain, k_gain)


def _sw_attn_kernel(sink_ref, q_ref, k_ref, v_ref, x_ref, mod_ref, wo_ref, o_ref, att_ref, *, tq, group):
    i = pl.program_id(1)
    D = q_ref.shape[1]
    nk = tq + SW_WINDOW
    k0 = pl.multiple_of(jnp.maximum(i * tq - SW_WINDOW, 0), SW_WINDOW)
    qpos = i * tq + lax.broadcasted_iota(jnp.int32, (tq, nk), 0)
    kpos = k0 + lax.broadcasted_iota(jnp.int32, (tq, nk), 1)
    valid = jnp.logical_and(kpos <= qpos, kpos > qpos - SW_WINDOW)
    lane = lax.broadcasted_iota(jnp.int32, (1, LANES), 1)
    for blk in range(D // LANES):
        kvh = (2 * blk) // group
        k = k_ref[pl.ds(k0, nk), kvh * LANES:(kvh + 1) * LANES]
        v = v_ref[pl.ds(k0, nk), kvh * LANES:(kvh + 1) * LANES]
        q = q_ref[:, blk * LANES:(blk + 1) * LANES]
        outs = []
        for c in range(2):
            sel = (lane < HEAD_DIM) if c == 0 else (lane >= HEAD_DIM)
            s = _dot_nt(jnp.where(sel, q, jnp.zeros_like(q)), k)
            s = jnp.where(valid, s, -jnp.inf)
            snk = sink_ref[2 * blk + c]
            m = jnp.maximum(jnp.max(s, axis=1, keepdims=True), snk)
            p = jnp.exp(s - m)
            denom = jnp.sum(p, axis=1, keepdims=True) + jnp.exp(snk - m)
            outs.append(_dot((p / denom).astype(BF16), v))
        att_ref[:, blk * LANES:(blk + 1) * LANES] = jnp.where(lane < HEAD_DIM, outs[0], outs[1]).astype(BF16)
    o_ref[...] = x_ref[...] + mod_ref[2:3, :] * _dot(att_ref[...], wo_ref[...])


def _sw_attn(q, kd, vd, sinks, x2d, mod3, w_o, *, B, T, tq, group):
    N, D = x2d.shape
    nq = T // tq
    kvd = kd.shape[1]
    tile = pl.BlockSpec((tq, D), lambda b, i: (b * nq + i, 0))
    seq = pl.BlockSpec((None, T, kvd), lambda b, i: (b, 0, 0))
    return pl.pallas_call(
        functools.partial(_sw_attn_kernel, tq=tq, group=group),
        grid=(B, nq),
        in_specs=[pl.BlockSpec(memory_space=pltpu.SMEM), tile, seq, seq, tile,
                  pl.BlockSpec((None, 3, D), lambda b, i: (b, 0, 0)), _const_spec(w_o.shape)],
        out_specs=tile,
        out_shape=jax.ShapeDtypeStruct((N, D), F32),
        scratch_shapes=[pltpu.VMEM((tq, D), BF16)],
        compiler_params=_cparams("parallel", "arbitrary"),
        name="sw_attn",
    )(sinks, q, kd.reshape(B, T, kvd), vd.reshape(B, T, kvd), x2d, mod3, w_o)


def _rw_prep_kernel(x_ref, xp_ref, mod_ref, g_ref, mu_ref, wrkv_ref, w0_ref, w1_ref, w2_ref, a0_ref, a1_ref,
                    a2_ref, g1_ref, g2_ref, kk_ref, ka_ref, gsum_ref, gexp_ref,
                    r_out, lw_out, k_out, v_out, kk_out, b_out, g_out, *, tiles_per_batch):
    D = x_ref.shape[1]
    tm = x_ref.shape[0]
    shift, scale, gn = mod_ref[0:1, :], mod_ref[1:2, :], g_ref[...]
    h = _norm_mod(x_ref[...], gn, shift, scale)
    hp = _norm_mod(xp_ref[...], gn, shift, scale)[7:8, :]
    hp = jnp.where(pl.program_id(0) % tiles_per_batch == 0, jnp.zeros_like(hp), hp)
    row = lax.broadcasted_iota(jnp.int32, (tm, 1), 0)
    xx = jnp.where(row == 0, hp, pltpu.roll(h, 1, axis=0)) - h

    def mixed(n):
        return (h + xx * mu_ref[n:n + 1, :]).astype(BF16)

    r = _dot(mixed(0), wrkv_ref[0])
    k = _dot(mixed(1), wrkv_ref[1])
    v = _dot(mixed(2), wrkv_ref[2])
    d = _dot(jnp.tanh(_dot(mixed(3), w1_ref[...])).astype(BF16), w2_ref[...]) + w0_ref[...]
    z = -d
    softplus = jnp.maximum(z, 0.0) + jnp.log(1.0 + jnp.exp(-jnp.abs(z)))
    lw_out[...] = -jnp.exp(-softplus - 0.5)
    a = _sigmoid(a0_ref[...] + _dot(_dot(mixed(4), a1_ref[...]).astype(BF16), a2_ref[...]))
    g_out[...] = _dot(_sigmoid(_dot(mixed(5), g1_ref[...])).astype(BF16), g2_ref[...]).astype(BF16)
    kk = k * kk_ref[...]
    ss = _dot((kk * kk).astype(BF16), gsum_ref[...])
    kk = kk * _group_bcast(1.0 / jnp.maximum(jnp.sqrt(ss), 1e-12), gexp_ref[...])
    r_out[...] = r.astype(BF16)
    k_out[...] = (k * (1.0 + (a - 1.0) * ka_ref[...])).astype(BF16)
    v_out[...] = v.astype(BF16)
    kk_out[...] = kk.astype(BF16)
    b_out[...] = (kk * a).astype(BF16)


def _pad_to(w, axis, n):
    pad = [(0, 0)] * w.ndim
    pad[axis] = (0, n - w.shape[axis])
    return jnp.pad(w, pad)


def _round_up(n, m):
    return -(-n // m) * m


def _rw_prep(x2d, mod3, g, p, *, tm, tiles_per_batch):
    N, D = x2d.shape
    gsum, gexp = _group_mats(D)
    row = lambda t: t.reshape(1, D)
    lora = lambda w1, w2: (_pad_to(w1, 1, _round_up(w1.shape[1], LANES)).astype(BF16),
                           _pad_to(w2, 0, _round_up(w2.shape[0], LANES)).astype(BF16))
    w1, w2 = lora(p["decay_w1"], p["decay_w2"])
    a1, a2 = lora(p["iclr_a1"], p["iclr_a2"])
    g1, g2 = lora(p["gate_g1"], p["gate_g2"])
    consts = [row(g), p["mu"], p["w_rkv"].astype(BF16), row(p["decay_w0"]), w1, w2, row(p["iclr_a0"]), a1, a2,
              g1, g2, row(p["k_k"]), row(p["k_a"]), gsum, gexp]
    tile = pl.BlockSpec((tm, D), lambda i: (i, 0))
    outs = [jax.ShapeDtypeStruct((N, D), BF16)] * 7
    outs[1] = jax.ShapeDtypeStruct((N, D), F32)
    return pl.pallas_call(
        functools.partial(_rw_prep_kernel, tiles_per_batch=tiles_per_batch),
        grid=(N // tm,),
        in_specs=[tile,
                  pl.BlockSpec((8, D), lambda i: (jnp.maximum(i * (tm // 8) - 1, 0), 0)),
                  pl.BlockSpec((None, 3, D), lambda i: (i // tiles_per_batch, 0, 0))]
                 + [_const_spec(c.shape) for c in consts],
        out_specs=[tile] * 7,
        out_shape=outs,
        compiler_params=_cparams("parallel"),
        name="rw_prep",
    )(x2d, x2d, mod3, *consts)


def _stack_heads(x, m0):
    z = jnp.zeros_like(x)
    return jnp.concatenate([jnp.where(m0, x, z), jnp.where(m0, z, x)], axis=0)


def _rw_scan_kernel(r_ref, lw_ref, k_ref, v_ref, kk_ref, b_ref, rk_ref, lng_ref, lnb_ref, o_ref, st_ref, os_ref,
                    *, tt, group):
    C = RW_CHUNK
    C2 = 2 * C

    @pl.when(pl.program_id(2) == 0)
    def _():
        st_ref[...] = jnp.zeros_like(st_ref)

    lane = lax.broadcasted_iota(jnp.int32, (1, LANES), 1)
    m0 = lane < HEAD_DIM
    ti = lax.broadcasted_iota(jnp.int32, (C, C), 0)
    tj = lax.broadcasted_iota(jnp.int32, (C, C), 1)
    tri = (ti >= tj).astype(F32)
    ri = lax.broadcasted_iota(jnp.int32, (2 * C2, 2 * C2), 0)
    ci = lax.broadcasted_iota(jnp.int32, (2 * C2, 2 * C2), 1)
    rt, ct = ri % C, ci % C
    causal = rt - ct + (ri >= C2).astype(jnp.int32) > 0
    eye = (lax.broadcasted_iota(jnp.int32, (C2, C2), 0) == lax.broadcasted_iota(jnp.int32, (C2, C2), 1))
    eye_f = eye.astype(F32)
    si = lax.broadcasted_iota(jnp.int32, (C2, C2), 0)
    sj = lax.broadcasted_iota(jnp.int32, (C2, C2), 1)
    merge_masks = []
    s = 1
    while s < C:
        merge_masks.append(jnp.logical_and(si // (2 * s) == sj // (2 * s),
                                           jnp.logical_and(si % (2 * s) >= s, sj % (2 * s) < s)))
        s *= 2
    zeros_sq = jnp.zeros((C2, C2), BF16)

    def chunk_terms(chunks):
        n = range(len(chunks))
        a_t, r_t, v_s, aa, lhs_top, p_col = [], [], [], [], [], []
        for i, c in enumerate(chunks):
            rows = slice(c * C, (c + 1) * C)
            lw = lw_ref[rows, :]
            r = r_ref[rows, :].astype(F32)
            k = k_ref[rows, :].astype(F32)
            kk = kk_ref[rows, :].astype(F32)
            b = b_ref[rows, :].astype(F32)
            cum = jnp.dot(tri, lw, precision=HIGHEST, preferred_element_type=F32)
            tot = cum[C - 1:C, :]
            e_neg = jnp.exp(-cum)
            r_t.append(_stack_heads(r * jnp.exp(cum), m0).astype(BF16))
            a_t.append(_stack_heads(-kk * jnp.exp(cum - lw), m0).astype(BF16))
            b_t = _stack_heads(b * e_neg, m0).astype(BF16)
            k_t = _stack_heads(k * e_neg, m0).astype(BF16)
            e_end = jnp.exp(tot - cum)
            b_h = _stack_heads(b * e_end, m0)
            k_h = _stack_heads(k * e_end, m0)
            lhs_top.append(jnp.concatenate([b_h.T, k_h.T], axis=1).astype(BF16))
            v_s.append(_stack_heads(v_ref[rows, :], m0))
            p_col.append(jnp.sum(jnp.where(eye, jnp.exp(tot), 0.0), axis=1, keepdims=True))
            full = _dot_nt(jnp.concatenate([a_t[i], r_t[i]], axis=0), jnp.concatenate([b_t, k_t], axis=0))
            aa.append(jnp.where(causal, full, 0.0))
        yield

        a_ab = [aa[i][:C2, :C2] for i in n]
        t_inv = [eye_f + jnp.where(merge_masks[0], a_ab[i], 0.0) for i in n]
        for mask in merge_masks[1:]:
            tb = [t_inv[i].astype(BF16) for i in n]
            half = [_dot(tb[i], jnp.where(mask, a_ab[i], 0.0).astype(BF16)).astype(BF16) for i in n]
            yield
            t_inv = [t_inv[i] + _dot(half[i], tb[i]) for i in n]
            yield
        akv = [_dot(aa[i][:C2, C2:].astype(BF16), v_s[i]).astype(BF16) for i in n]
        yield
        au = [_dot(t_inv[i].astype(BF16), jnp.concatenate([a_t[i], akv[i]], axis=1)).astype(BF16)
              for i in n]
        yield
        big = []
        for i in n:
            rhs = jnp.concatenate([au[i], jnp.concatenate([zeros_sq, v_s[i]], axis=1)], axis=0)
            lhs = jnp.concatenate([lhs_top[i], aa[i][C2:, :].astype(BF16)], axis=0)
            big.append(_dot(lhs, rhs))
        qg = [jnp.concatenate([big[i][C2:, :C2] + r_t[i].astype(F32), big[i][:C2, :C2]], axis=0).astype(BF16)
              for i in n]
        return qg, big, p_col

    def state_steps(st, chunks, terms):
        qg, big, p_col = terms
        for i, c in enumerate(chunks):
            res = _dot(qg[i], st[0].astype(BF16))
            o_st = res[:C2] + big[i][C2:, C2:]
            os_ref[c * C:(c + 1) * C, :] = o_st[:C] + o_st[C:]
            st[0] = p_col[i] * st[0] + res[C2:] + big[i][:C2, C2:]
            yield

    st = [st_ref[...]]
    n_chunks = tt // C
    pending = iter(())
    for g0 in range(0, n_chunks, group):
        chunks = list(range(g0, min(g0 + group, n_chunks)))
        gen = chunk_terms(chunks)
        while True:
            try:
                next(gen)
            except StopIteration as done:
                terms = done.value
                break
            next(pending, None)
        for _ in pending:
            pass
        pending = state_steps(st, chunks, terms)
    for _ in pending:
        pass
    st_ref[...] = st[0]

    o = os_ref[...]
    blk = ((lax.broadcasted_iota(jnp.int32, (LANES, LANES), 0) < HEAD_DIM)
           == (lax.broadcasted_iota(jnp.int32, (LANES, LANES), 1) < HEAD_DIM)).astype(BF16)

    def head_sum(x):
        hi = x.astype(BF16)
        lo = (x - hi.astype(F32)).astype(BF16)
        return _dot(hi, blk) + _dot(lo, blk)

    mean = head_sum(o) * (1.0 / HEAD_DIM)
    cen = o - mean
    var = head_sum(cen * cen) * (1.0 / HEAD_DIM)
    y = cen * lax.rsqrt(var + RW_LNX_EPS) * lng_ref[...] + lnb_ref[...]
    r = r_ref[...].astype(F32)
    k = k_ref[...].astype(F32)
    bonus = head_sum(r * k * rk_ref[...]) * v_ref[...].astype(F32)
    o_ref[...] = (y + bonus).astype(BF16)


def _rw_scan(r, lw, k, v, kk, b, r_k, lnx_g, lnx_b, *, B, T, D, tt):
    hp = D // LANES
    blk = pl.BlockSpec((None, tt, LANES), lambda bi, h, t: (bi, t, h))
    vec = pl.BlockSpec((1, LANES), lambda bi, h, t: (0, h))
    r3 = lambda t: t.reshape(B, T, D)
    out = pl.pallas_call(
        functools.partial(_rw_scan_kernel, tt=tt, group=RW_GROUP),
        grid=(B, hp, T // tt),
        in_specs=[blk] * 6 + [vec] * 3,
        out_specs=blk,
        out_shape=jax.ShapeDtypeStruct((B, T, D), BF16),
        scratch_shapes=[pltpu.VMEM((LANES, LANES), F32), pltpu.VMEM((tt, LANES), F32)],
        compiler_params=_cparams("parallel", "parallel", "arbitrary"),
        name="rw_scan",
    )(r3(r), r3(lw), r3(k), r3(v), r3(kk), r3(b), r_k.reshape(1, D), lnx_g.reshape(1, D), lnx_b.reshape(1, D))
    return out.reshape(B * T, D)


def _diff_lambda_init(layer):
    return 0.8 - 0.6 * math.exp(-0.3 * layer)


def _pick_tile(n, pref):
    t = min(n, pref)
    while n % t:
        t //= 2
    return t


def kernel(x, c, norm_g, ada_w, ada_b, ffn_w_in, ffn_w_out, da_w_qkv, da_w_o, da_q_norm_g, da_k_norm_g, da_lambda, da_subln_g, rw_mu, rw_w_rkv, rw_w_o, rw_decay_w0, rw_decay_w1, rw_decay_w2, rw_iclr_a0, rw_iclr_a1, rw_iclr_a2, rw_gate_g1, rw_gate_g2, rw_k_k, rw_k_a, rw_r_k, rw_lnx_g, rw_lnx_b, sw_w_qkv, sw_w_o, sw_q_norm_g, sw_k_norm_g, sw_sinks):
    B, T, D = x.shape
    L = norm_g.shape[0]
    tm = _pick_tile(T, 512)
    tpb = T // tm
    kw = dict(tm=tm, tiles_per_batch=tpb)
    scale = HEAD_DIM ** -0.5

    mod = _ada(c, ada_w, ada_b).reshape(L, B, 3, 3, D)
    x2d = x.reshape(B * T, D)
    for i in range(L):
        kind, j = i % 3, i // 3
        x2d = _ffn(x2d, mod[i, :, 0], norm_g[i, 0], ffn_w_in[i, 0].astype(BF16), ffn_w_out[i, 0].astype(BF16), **kw)
        m2 = mod[i, :, 1]
        if kind == 0:
            qg = jnp.tile(da_q_norm_g[j] * (scale * math.log2(math.e)), D // HEAD_DIM).reshape(1, D)
            kg = jnp.tile(da_k_norm_g[j], D // HEAD_DIM).reshape(1, D)
            q, k, vt = _da_qkv(x2d, m2, norm_g[i, 1], da_w_qkv[j].astype(BF16), qg, kg, B=B, T=T, tm=tm)
            o = _da_attn(q, k, vt, da_lambda[j], da_subln_g[j], B=B, T=T, D=D, tq=_pick_tile(T, 512),
                         lambda_init=_diff_lambda_init(i))
            x2d = _out_proj(o, None, x2d, m2, da_w_o[j].astype(BF16), **kw)
        elif kind == 1:
            p = dict(mu=rw_mu[j], w_rkv=rw_w_rkv[j], decay_w0=rw_decay_w0[j], decay_w1=rw_decay_w1[j],
                     decay_w2=rw_decay_w2[j], iclr_a0=rw_iclr_a0[j], iclr_a1=rw_iclr_a1[j], iclr_a2=rw_iclr_a2[j],
                     gate_g1=rw_gate_g1[j], gate_g2=rw_gate_g2[j], k_k=rw_k_k[j], k_a=rw_k_a[j])
            r, lw, k, v, kk, b, g = _rw_prep(x2d, m2, norm_g[i, 1], p, **kw)
            y = _rw_scan(r, lw, k, v, kk, b, rw_r_k[j], rw_lnx_g[j], rw_lnx_b[j], B=B, T=T, D=D,
                         tt=_pick_tile(T, 2048))
            x2d = _out_proj(y, g, x2d, m2, rw_w_o[j].astype(BF16), **kw)
        else:
            kv = (sw_w_qkv.shape[-1] - D) // 2
            n_kv = kv // HEAD_DIM
            qg = jnp.tile(sw_q_norm_g[j] * scale, D // HEAD_DIM).reshape(1, D)
            kg = jnp.tile(sw_k_norm_g[j], n_kv).reshape(1, kv)
            q, k, v = _sw_qkv(x2d, m2, norm_g[i, 1], sw_w_qkv[j].astype(BF16), qg, kg, **kw)
            dup = lambda t: jnp.repeat(t.reshape(B * T, n_kv, 1, HEAD_DIM), 2, axis=2).reshape(B * T, 2 * kv)
            x2d = _sw_attn(q, dup(k), dup(v), sw_sinks[j], x2d, m2, sw_w_o[j].astype(BF16), B=B, T=T,
                           tq=_pick_tile(T, 256), group=(D // HEAD_DIM) // n_kv)
        x2d = _ffn(x2d, mod[i, :, 2], norm_g[i, 2], ffn_w_in[i, 1].astype(BF16), ffn_w_out[i, 1].astype(BF16), **kw)
    return x2d.reshape(B, T, D)
```

```python
import functools
import math

import jax
import jax.numpy as jnp
from jax import lax
from jax.experimental import pallas as pl
from jax.experimental.pallas import tpu as pltpu

F32 = jnp.float32
BF16 = jnp.bfloat16
HIGHEST = lax.Precision.HIGHEST

HEAD_DIM = 64
LANES = 128
NORM_EPS = 1e-6
SUBLN_EPS = 1e-5
RW_LNX_EPS = 64e-5
SW_WINDOW = 128
N_MOD = 9
RW_GROUP = 8
RW_CHUNK = 64
VMEM_LIMIT = 56 * 1024 * 1024


def _cparams(*sem):
    return pltpu.CompilerParams(dimension_semantics=sem, vmem_limit_bytes=VMEM_LIMIT)


def _sigmoid(x):
    return 1.0 / (1.0 + jnp.exp(-x))


def _dot(a, b):
    return jnp.dot(a, b, preferred_element_type=F32)


def _dot_nt(a, b):
    return lax.dot_general(a, b, (((1,), (1,)), ((), ())), preferred_element_type=F32)


def _norm_mod(x, g, shift, scale):
    ms = jnp.mean(x * x, axis=-1, keepdims=True)
    return (x * lax.rsqrt(ms + NORM_EPS) * g) * (1.0 + scale) + shift


def _const_spec(shape):
    nd = len(shape)
    return pl.BlockSpec(shape, lambda *_: (0,) * nd, pipeline_mode=pl.Buffered(1))


def _group_mats(d):
    grp = jnp.arange(d) // HEAD_DIM
    gsum = (grp[:, None] == jnp.arange(LANES)[None, :]).astype(BF16)
    return gsum, gsum.T


def _group_bcast(v, gexp):
    hi = v.astype(BF16)
    lo = (v - hi.astype(F32)).astype(BF16)
    return _dot(hi, gexp) + _dot(lo, gexp)


def _ada_kernel(c_ref, w_ref, b_ref, o_ref):
    c = c_ref[...]
    cond = c * _sigmoid(c)
    o_ref[...] = jnp.dot(cond, w_ref[...], precision=HIGHEST, preferred_element_type=F32) + b_ref[...]


def _ada(c, ada_w, ada_b):
    L, D, ND = ada_w.shape
    B = c.shape[0]
    return pl.pallas_call(
        _ada_kernel,
        grid=(L, ND // D),
        in_specs=[pl.BlockSpec((B, D), lambda l, j: (0, 0)),
                  pl.BlockSpec((None, D, D), lambda l, j: (l, 0, j)),
                  pl.BlockSpec((None, 1, D), lambda l, j: (l, 0, j))],
        out_specs=pl.BlockSpec((None, B, D), lambda l, j: (l, 0, j)),
        out_shape=jax.ShapeDtypeStruct((L, B, ND), F32),
        compiler_params=_cparams("parallel", "parallel"),
        name="ada_mod",
    )(c, ada_w, ada_b.reshape(L, 1, ND))


def _ffn_kernel(x_ref, mod_ref, g_ref, win_ref, wout_ref, o_ref, *, d_ff, ck):
    x = x_ref[...]
    h = _norm_mod(x, g_ref[...], mod_ref[0:1, :], mod_ref[1:2, :]).astype(BF16)
    acc = None
    for c0 in range(0, d_ff, ck):
        gate = _dot(h, win_ref[:, c0:c0 + ck])
        up = _dot(h, win_ref[:, d_ff + c0:d_ff + c0 + ck])
        act = (gate * _sigmoid(gate) * up).astype(BF16)
        y = _dot(act, wout_ref[c0:c0 + ck, :])
        acc = y if acc is None else acc + y
    o_ref[...] = x + (0.5 * mod_ref[2:3, :]) * acc


def _ffn_chunk(d_ff):
    for n in (2, 1, 4, 11, 22):
        if d_ff % n == 0 and (d_ff // n) % LANES == 0:
            return d_ff // n
    return d_ff


def _ffn(x2d, mod3, g, w_in, w_out, *, tm, tiles_per_batch):
    N, D = x2d.shape
    d_ff = w_out.shape[0]
    kern = functools.partial(_ffn_kernel, d_ff=d_ff, ck=_ffn_chunk(d_ff))
    return pl.pallas_call(
        kern,
        grid=(N // tm,),
        in_specs=[pl.BlockSpec((tm, D), lambda i: (i, 0)),
                  pl.BlockSpec((None, 3, D), lambda i: (i // tiles_per_batch, 0, 0)),
                  _const_spec((1, D)),
                  _const_spec(w_in.shape),
                  _const_spec(w_out.shape)],
        out_specs=pl.BlockSpec((tm, D), lambda i: (i, 0)),
        out_shape=jax.ShapeDtypeStruct((N, D), F32),
        compiler_params=_cparams("parallel"),
        name="ffn_half",
    )(x2d, mod3, g.reshape(1, D), w_in, w_out)


def _out_proj_kernel(*refs, gated):
    if gated:
        y_ref, gate_ref, x_ref, mod_ref, w_ref, o_ref = refs
        y = (y_ref[...].astype(F32) * gate_ref[...].astype(F32)).astype(BF16)
    else:
        y_ref, x_ref, mod_ref, w_ref, o_ref = refs
        y = y_ref[...]
    o_ref[...] = x_ref[...] + mod_ref[2:3, :] * _dot(y, w_ref[...])


def _out_proj(y, gate, x2d, mod3, w_o, *, tm, tiles_per_batch):
    N, D = x2d.shape
    tile = pl.BlockSpec((tm, D), lambda i: (i, 0))
    gated = gate is not None
    ins = [y] + ([gate] if gated else []) + [x2d, mod3, w_o]
    specs = [tile] + ([tile] if gated else []) + [
        tile, pl.BlockSpec((None, 3, D), lambda i: (i // tiles_per_batch, 0, 0)), _const_spec(w_o.shape)]
    return pl.pallas_call(
        functools.partial(_out_proj_kernel, gated=gated),
        grid=(N // tm,),
        in_specs=specs,
        out_specs=tile,
        out_shape=jax.ShapeDtypeStruct((N, D), F32),
        compiler_params=_cparams("parallel"),
        name="out_proj",
    )(*ins)


def _group_rms_scale(y, gsum, gexp):
    ss = _dot((y * y).astype(BF16), gsum)
    return _group_bcast(lax.rsqrt(ss * (1.0 / HEAD_DIM) + NORM_EPS), gexp)


def _da_qkv_kernel(x_ref, mod_ref, g_ref, w_ref, gsum_ref, gexp_ref, qg_ref, kg_ref, q_ref, k_ref, vt_ref):
    D = x_ref.shape[1]
    h = _norm_mod(x_ref[...], g_ref[...], mod_ref[0:1, :], mod_ref[1:2, :]).astype(BF16)
    q = _dot(h, w_ref[:, 0:D])
    q_ref[...] = (q * _group_rms_scale(q, gsum_ref[...], gexp_ref[...]) * qg_ref[...]).astype(BF16)
    k = _dot(h, w_ref[:, D:2 * D])
    k_ref[...] = (k * _group_rms_scale(k, gsum_ref[...], gexp_ref[...]) * kg_ref[...]).astype(BF16)
    v = _dot(h, w_ref[:, 2 * D:3 * D])
    vt_ref[...] = v.T.astype(BF16)


def _da_qkv(x2d, mod3, g, w_qkv, q_gain, k_gain, *, B, T, tm):
    N, D = x2d.shape
    tpb = T // tm
    gsum, gexp = _group_mats(D)
    tile = pl.BlockSpec((tm, D), lambda i: (i, 0))
    return pl.pallas_call(
        _da_qkv_kernel,
        grid=(N // tm,),
        in_specs=[tile,
                  pl.BlockSpec((None, 3, D), lambda i: (i // tpb, 0, 0)),
                  _const_spec((1, D)), _const_spec(w_qkv.shape), _const_spec(gsum.shape),
                  _const_spec(gexp.shape), _const_spec((1, D)), _const_spec((1, D))],
        out_specs=[tile, tile, pl.BlockSpec((None, D, tm), lambda i: (i // tpb, 0, i % tpb))],
        out_shape=[jax.ShapeDtypeStruct((N, D), BF16), jax.ShapeDtypeStruct((N, D), BF16),
                   jax.ShapeDtypeStruct((B, D, T), BF16)],
        compiler_params=_cparams("parallel"),
        name="da_qkv",
    )(x2d, mod3, g.reshape(1, D), w_qkv, gsum, gexp, q_gain, k_gain)


def _da_attn_kernel(lam_ref, sg_ref, q_ref, k_ref, vt_ref, o_ref, acc_ref, sa_ref, sb_ref, *, tq):
    nq = q_ref.shape[0] // tq
    lane = lax.broadcasted_iota(jnp.int32, (1, LANES), 1)
    lam = lam_ref[...]
    lambda_init = lam[4:5, 0:1]
    lam_full = (jnp.exp(jnp.sum(lam[0:1] * lam[1:2], axis=1, keepdims=True))
                - jnp.exp(jnp.sum(lam[2:3] * lam[3:4], axis=1, keepdims=True)) + lambda_init)
    kpos = lax.broadcasted_iota(jnp.int32, (tq, tq), 0)
    qpos = lax.broadcasted_iota(jnp.int32, (tq, tq), 1)
    causal = kpos <= qpos
    bufs = (sa_ref, sb_ref)
    q_maps = {}

    def q_masked(qi):
        if qi not in q_maps:
            q = q_ref[qi * tq:(qi + 1) * tq, :]
            zero = jnp.zeros_like(q)
            q_maps[qi] = (jnp.where(lane < HEAD_DIM, q, zero), jnp.where(lane >= HEAD_DIM, q, zero))
        return q_maps[qi]

    def scores_into(s_ref, qi, j):
        k = k_ref[j * tq:(j + 1) * tq, :]
        for c in range(2):
            s_ref[c] = _dot_nt(k, q_masked(qi)[c])

    def consume(s_ref, acc, qi, j, carry):
        vt = vt_ref[:, j * tq:(j + 1) * tq]
        out = []
        for c in range(2):
            s = s_ref[c]
            if j == qi:
                s = jnp.where(causal, s, -jnp.inf)
            m_cur = jnp.max(s, axis=0, keepdims=True)
            if j == 0:
                m_new = m_cur
                p = jnp.exp2(s - m_new)
                l_new = jnp.sum(p, axis=0, keepdims=True)
                acc[c] = _dot(vt, p.astype(BF16))
            else:
                m_prev, l_prev = carry[2 * c], carry[2 * c + 1]
                m_new = jnp.maximum(m_prev, m_cur)
                alpha = jnp.exp2(m_prev - m_new)
                p = jnp.exp2(s - m_new)
                l_new = alpha * l_prev + jnp.sum(p, axis=0, keepdims=True)
                acc[c] = alpha * acc[c] + _dot(vt, p.astype(BF16))
            out += [m_new, l_new]
        return tuple(out)

    def finish(acc, qi, carry):
        ot = acc[0] / carry[1] - lam_full * (acc[1] / carry[3])
        ms = jnp.mean(ot * ot, axis=0, keepdims=True)
        ot = ot * lax.rsqrt(ms + SUBLN_EPS)
        o_ref[qi * tq:(qi + 1) * tq, :] = (ot.T * sg_ref[...] * (1.0 - lambda_init)).astype(BF16)

    tasks = [(qi, j) for qi in range(nq) for j in range(qi + 1)]
    scores_into(bufs[0], *tasks[0])
    carry = None
    for t, (qi, j) in enumerate(tasks):
        if t + 1 < len(tasks):
            scores_into(bufs[(t + 1) % 2], *tasks[t + 1])
        acc = acc_ref.at[qi % 2]
        carry = consume(bufs[t % 2], acc, qi, j, carry)
        if j == qi:
            finish(acc, qi, carry)


def _da_attn(q, k, vt, lambdas, subln_g, *, B, T, D, tq, lambda_init):
    H = D // LANES
    q3, k3 = q.reshape(B, T, D), k.reshape(B, T, D)
    seq = pl.BlockSpec((None, T, LANES), lambda b, h: (b, 0, h))
    out = pl.pallas_call(
        functools.partial(_da_attn_kernel, tq=tq),
        grid=(B, H),
        in_specs=[pl.BlockSpec((5, HEAD_DIM), lambda b, h: (0, 0)),
                  pl.BlockSpec((1, LANES), lambda b, h: (0, 0)),
                  seq, seq,
                  pl.BlockSpec((None, LANES, T), lambda b, h: (b, h, 0))],
        out_specs=seq,
        out_shape=jax.ShapeDtypeStruct((B, T, D), BF16),
        scratch_shapes=[pltpu.VMEM((2, 2, LANES, tq), F32),
                        pltpu.VMEM((2, tq, tq), F32), pltpu.VMEM((2, tq, tq), F32)],
        compiler_params=_cparams("parallel", "parallel"),
        name="da_attn",
    )(jnp.concatenate([lambdas, jnp.full((1, HEAD_DIM), lambda_init, F32)], axis=0),
      subln_g.reshape(1, LANES), q3, k3, vt)
    return out.reshape(B * T, D)


def _sw_qkv_kernel(x_ref, mod_ref, g_ref, w_ref, gsum_ref, gexp_ref, ksum_ref, kexp_ref, qg_ref, kg_ref,
                   q_ref, k_ref, v_ref):
    D = x_ref.shape[1]
    h = _norm_mod(x_ref[...], g_ref[...], mod_ref[0:1, :], mod_ref[1:2, :]).astype(BF16)
    q = _dot(h, w_ref[:, 0:D])
    q_ref[...] = (q * _group_rms_scale(q, gsum_ref[...], gexp_ref[...]) * qg_ref[...]).astype(BF16)
    kv = k_ref.shape[1]
    k = _dot(h, w_ref[:, D:D + kv])
    k_ref[...] = (k * _group_rms_scale(k, ksum_ref[...], kexp_ref[...]) * kg_ref[...]).astype(BF16)
    v_ref[...] = _dot(h, w_ref[:, D + kv:D + 2 * kv]).astype(BF16)


def _sw_qkv(x2d, mod3, g, w_qkv, q_gain, k_gain, *, tm, tiles_per_batch):
    N, D = x2d.shape
    kv = (w_qkv.shape[1] - D) // 2
    gsum, gexp = _group_mats(D)
    ksum, kexp = _group_mats(kv)
    tile = pl.BlockSpec((tm, D), lambda i: (i, 0))
    kvtile = pl.BlockSpec((tm, kv), lambda i: (i, 0))
    return pl.pallas_call(
        _sw_qkv_kernel,
        grid=(N // tm,),
        in_specs=[tile, pl.BlockSpec((None, 3, D), lambda i: (i // tiles_per_batch, 0, 0)),
                  _const_spec((1, D)), _const_spec(w_qkv.shape), _const_spec(gsum.shape),
                  _const_spec(gexp.shape), _const_spec(ksum.shape), _const_spec(kexp.shape),
                  _const_spec((1, D)), _const_spec((1, kv))],
        out_specs=[tile, kvtile, kvtile],
        out_shape=[jax.ShapeDtypeStruct((N, D), BF16), jax.ShapeDtypeStruct((N, kv), BF16),
                   jax.ShapeDtypeStruct((N, kv), BF16)],
        compiler_params=_cparams("parallel"),
        name="sw_qkv",
    )(x2d, mod3, g.reshape(1, D), w_qkv, gsum, gexp, ksum, kexp, q_gain, k_gain)


def _sw_attn_kernel(sink_ref, q_ref, k_ref, v_ref, x_ref, mod_ref, wo_ref, o_ref, att_ref, *, tq, group):
    i = pl.program_id(1)
    D = q_ref.shape[1]
    W = SW_WINDOW
    n_kv = (D // HEAD_DIM) // group
    tiles_per_kv = group // 2
    lane = lax.broadcasted_iota(jnp.int32, (1, LANES), 1)
    first = lane < HEAD_DIM
    jobs = [(sub, kvh) for sub in range(tq // W) for kvh in range(n_kv)]

    def band_start(sub):
        return pl.multiple_of(jnp.maximum(i * tq + (sub - 1) * W, 0), W)

    def scores(sub, kvh):
        k = k_ref[pl.ds(band_start(sub), 2 * W), kvh * LANES:(kvh + 1) * LANES]
        parts = []
        for b in range(tiles_per_kv):
            blk = kvh * tiles_per_kv + b
            q = q_ref[sub * W:(sub + 1) * W, blk * LANES:(blk + 1) * LANES]
            zero = jnp.zeros_like(q)
            parts += [jnp.where(first, q, zero), jnp.where(first, zero, q)]
        return _dot_nt(jnp.concatenate(parts, axis=0), k)

    pending = scores(*jobs[0])
    for n, (sub, kvh) in enumerate(jobs):
        s_all = pending
        if n + 1 < len(jobs):
            pending = scores(*jobs[n + 1])
        qpos = i * tq + sub * W + lax.broadcasted_iota(jnp.int32, (W, 2 * W), 0)
        kpos = band_start(sub) + lax.broadcasted_iota(jnp.int32, (W, 2 * W), 1)
        valid = jnp.logical_and(kpos <= qpos, kpos > qpos - W)
        probs, inv = [], []
        for g in range(group):
            s = jnp.where(valid, s_all[g * W:(g + 1) * W], -jnp.inf)
            snk = sink_ref[kvh * group + g] * math.log2(math.e)
            m = jnp.maximum(jnp.max(s, axis=1, keepdims=True), snk)
            p = jnp.exp2(s - m)
            inv.append(1.0 / (jnp.sum(p, axis=1, keepdims=True) + jnp.exp2(snk - m)))
            probs.append(p.astype(BF16))
        v = v_ref[pl.ds(band_start(sub), 2 * W), kvh * LANES:(kvh + 1) * LANES]
        o_all = _dot(jnp.concatenate(probs, axis=0), v)
        for b in range(tiles_per_kv):
            blk = kvh * tiles_per_kv + b
            o0 = o_all[2 * b * W:(2 * b + 1) * W] * inv[2 * b]
            o1 = o_all[(2 * b + 1) * W:(2 * b + 2) * W] * inv[2 * b + 1]
            att_ref[sub * W:(sub + 1) * W, blk * LANES:(blk + 1) * LANES] = jnp.where(first, o0, o1).astype(BF16)
    o_ref[...] = x_ref[...] + mod_ref[2:3, :] * _dot(att_ref[...], wo_ref[...])


def _sw_attn(q, kd, vd, sinks, x2d, mod3, w_o, *, B, T, tq, group):
    N, D = x2d.shape
    nq = T // tq
    kvd = kd.shape[1]
    tile = pl.BlockSpec((tq, D), lambda b, i: (b * nq + i, 0))
    seq = pl.BlockSpec((None, T, kvd), lambda b, i: (b, 0, 0))
    return pl.pallas_call(
        functools.partial(_sw_attn_kernel, tq=tq, group=group),
        grid=(B, nq),
        in_specs=[pl.BlockSpec(memory_space=pltpu.SMEM), tile, seq, seq, tile,
                  pl.BlockSpec((None, 3, D), lambda b, i: (b, 0, 0)), _const_spec(w_o.shape)],
        out_specs=tile,
        out_shape=jax.ShapeDtypeStruct((N, D), F32),
        scratch_shapes=[pltpu.VMEM((tq, D), BF16)],
        compiler_params=_cparams("parallel", "arbitrary"),
        name="sw_attn",
    )(sinks, q, kd.reshape(B, T, kvd), vd.reshape(B, T, kvd), x2d, mod3, w_o)


def _rw_prep_kernel(x_ref, xp_ref, mod_ref, g_ref, mu_ref, wrkv_ref, w0_ref, w1_ref, w2_ref, a0_ref, a1_ref,
                    a2_ref, g1_ref, g2_ref, kk_ref, ka_ref, gsum_ref, gexp_ref,
                    r_out, lw_out, k_out, v_out, kk_out, b_out, g_out, *, tiles_per_batch):
    D = x_ref.shape[1]
    tm = x_ref.shape[0]
    shift, scale, gn = mod_ref[0:1, :], mod_ref[1:2, :], g_ref[...]
    h = _norm_mod(x_ref[...], gn, shift, scale)
    hp = _norm_mod(xp_ref[...], gn, shift, scale)[7:8, :]
    hp = jnp.where(pl.program_id(0) % tiles_per_batch == 0, jnp.zeros_like(hp), hp)
    row = lax.broadcasted_iota(jnp.int32, (tm, 1), 0)
    xx = jnp.where(row == 0, hp, pltpu.roll(h, 1, axis=0)) - h

    def mixed(n):
        return (h + xx * mu_ref[n:n + 1, :]).astype(BF16)

    r = _dot(mixed(0), wrkv_ref[0])
    k = _dot(mixed(1), wrkv_ref[1])
    v = _dot(mixed(2), wrkv_ref[2])
    d = _dot(jnp.tanh(_dot(mixed(3), w1_ref[...])).astype(BF16), w2_ref[...]) + w0_ref[...]
    z = -d
    softplus = jnp.maximum(z, 0.0) + jnp.log(1.0 + jnp.exp(-jnp.abs(z)))
    lw_out[...] = -jnp.exp(-softplus - 0.5)
    a = _sigmoid(a0_ref[...] + _dot(_dot(mixed(4), a1_ref[...]).astype(BF16), a2_ref[...]))
    g_out[...] = _dot(_sigmoid(_dot(mixed(5), g1_ref[...])).astype(BF16), g2_ref[...]).astype(BF16)
    kk = k * kk_ref[...]
    ss = _dot((kk * kk).astype(BF16), gsum_ref[...])
    kk = kk * _group_bcast(1.0 / jnp.maximum(jnp.sqrt(ss), 1e-12), gexp_ref[...])
    r_out[...] = r.astype(BF16)
    k_out[...] = (k * (1.0 + (a - 1.0) * ka_ref[...])).astype(BF16)
    v_out[...] = v.astype(BF16)
    kk_out[...] = kk.astype(BF16)
    b_out[...] = (kk * a).astype(BF16)


def _pad_to(w, axis, n):
    pad = [(0, 0)] * w.ndim
    pad[axis] = (0, n - w.shape[axis])
    return jnp.pad(w, pad)


def _round_up(n, m):
    return -(-n // m) * m


def _rw_prep(x2d, mod3, g, p, *, tm, tiles_per_batch):
    N, D = x2d.shape
    gsum, gexp = _group_mats(D)
    row = lambda t: t.reshape(1, D)
    lora = lambda w1, w2: (_pad_to(w1, 1, _round_up(w1.shape[1], LANES)).astype(BF16),
                           _pad_to(w2, 0, _round_up(w2.shape[0], LANES)).astype(BF16))
    w1, w2 = lora(p["decay_w1"], p["decay_w2"])
    a1, a2 = lora(p["iclr_a1"], p["iclr_a2"])
    g1, g2 = lora(p["gate_g1"], p["gate_g2"])
    consts = [row(g), p["mu"], p["w_rkv"].astype(BF16), row(p["decay_w0"]), w1, w2, row(p["iclr_a0"]), a1, a2,
              g1, g2, row(p["k_k"]), row(p["k_a"]), gsum, gexp]
    tile = pl.BlockSpec((tm, D), lambda i: (i, 0))
    outs = [jax.ShapeDtypeStruct((N, D), BF16)] * 7
    outs[1] = jax.ShapeDtypeStruct((N, D), F32)
    return pl.pallas_call(
        functools.partial(_rw_prep_kernel, tiles_per_batch=tiles_per_batch),
        grid=(N // tm,),
        in_specs=[tile,
                  pl.BlockSpec((8, D), lambda i: (jnp.maximum(i * (tm // 8) - 1, 0), 0)),
                  pl.BlockSpec((None, 3, D), lambda i: (i // tiles_per_batch, 0, 0))]
                 + [_const_spec(c.shape) for c in consts],
        out_specs=[tile] * 7,
        out_shape=outs,
        compiler_params=_cparams("parallel"),
        name="rw_prep",
    )(x2d, x2d, mod3, *consts)


def _stack_heads(x, m0):
    z = jnp.zeros_like(x)
    return jnp.concatenate([jnp.where(m0, x, z), jnp.where(m0, z, x)], axis=0)


def _rw_scan_kernel(r_ref, lw_ref, k_ref, v_ref, kk_ref, b_ref, rk_ref, lng_ref, lnb_ref, o_ref, st_ref, os_ref,
                    *, tt, group):
    C = RW_CHUNK
    C2 = 2 * C

    @pl.when(pl.program_id(2) == 0)
    def _():
        st_ref[...] = jnp.zeros_like(st_ref)

    lane = lax.broadcasted_iota(jnp.int32, (1, LANES), 1)
    m0 = lane < HEAD_DIM
    ti = lax.broadcasted_iota(jnp.int32, (C, C), 0)
    tj = lax.broadcasted_iota(jnp.int32, (C, C), 1)
    tri = (ti >= tj).astype(F32)
    ri = lax.broadcasted_iota(jnp.int32, (2 * C2, 2 * C2), 0)
    ci = lax.broadcasted_iota(jnp.int32, (2 * C2, 2 * C2), 1)
    rt, ct = ri % C, ci % C
    causal = rt - ct + (ri >= C2).astype(jnp.int32) > 0
    eye = (lax.broadcasted_iota(jnp.int32, (C2, C2), 0) == lax.broadcasted_iota(jnp.int32, (C2, C2), 1))
    eye_f = eye.astype(F32)
    si = lax.broadcasted_iota(jnp.int32, (C2, C2), 0)
    sj = lax.broadcasted_iota(jnp.int32, (C2, C2), 1)
    merge_masks = []
    s = 1
    while s < C:
        merge_masks.append(jnp.logical_and(si // (2 * s) == sj // (2 * s),
                                           jnp.logical_and(si % (2 * s) >= s, sj % (2 * s) < s)))
        s *= 2
    zeros_sq = jnp.zeros((C2, C2), BF16)

    def chunk_terms(chunks):
        n = range(len(chunks))
        a_t, r_t, v_s, aa, lhs_top, p_col = [], [], [], [], [], []
        for i, c in enumerate(chunks):
            rows = slice(c * C, (c + 1) * C)
            lw = lw_ref[rows, :]
            r = r_ref[rows, :].astype(F32)
            k = k_ref[rows, :].astype(F32)
            kk = kk_ref[rows, :].astype(F32)
            b = b_ref[rows, :].astype(F32)
            cum = jnp.dot(tri, lw, precision=HIGHEST, preferred_element_type=F32)
            tot = cum[C - 1:C, :]
            e_neg = jnp.exp(-cum)
            r_t.append(_stack_heads(r * jnp.exp(cum), m0).astype(BF16))
            a_t.append(_stack_heads(-kk * jnp.exp(cum - lw), m0).astype(BF16))
            b_t = _stack_heads(b * e_neg, m0).astype(BF16)
            k_t = _stack_heads(k * e_neg, m0).astype(BF16)
            e_end = jnp.exp(tot - cum)
            b_h = _stack_heads(b * e_end, m0)
            k_h = _stack_heads(k * e_end, m0)
            lhs_top.append(jnp.concatenate([b_h.T, k_h.T], axis=1).astype(BF16))
            v_s.append(_stack_heads(v_ref[rows, :], m0))
            p_col.append(jnp.sum(jnp.where(eye, jnp.exp(tot), 0.0), axis=1, keepdims=True))
            full = _dot_nt(jnp.concatenate([a_t[i], r_t[i]], axis=0), jnp.concatenate([b_t, k_t], axis=0))
            aa.append(jnp.where(causal, full, 0.0))
        yield

        a_ab = [aa[i][:C2, :C2] for i in n]
        t_inv = [eye_f + jnp.where(merge_masks[0], a_ab[i], 0.0) for i in n]
        for mask in merge_masks[1:]:
            tb = [t_inv[i].astype(BF16) for i in n]
            half = [_dot(tb[i], jnp.where(mask, a_ab[i], 0.0).astype(BF16)).astype(BF16) for i in n]
            yield
            t_inv = [t_inv[i] + _dot(half[i], tb[i]) for i in n]
            yield
        akv = [_dot(aa[i][:C2, C2:].astype(BF16), v_s[i]).astype(BF16) for i in n]
        yield
        au = [_dot(t_inv[i].astype(BF16), jnp.concatenate([a_t[i], akv[i]], axis=1)).astype(BF16)
              for i in n]
        yield
        big = []
        for i in n:
            rhs = jnp.concatenate([au[i], jnp.concatenate([zeros_sq, v_s[i]], axis=1)], axis=0)
            lhs = jnp.concatenate([lhs_top[i], aa[i][C2:, :].astype(BF16)], axis=0)
            big.append(_dot(lhs, rhs))
        qg = [jnp.concatenate([big[i][C2:, :C2] + r_t[i].astype(F32), big[i][:C2, :C2]], axis=0).astype(BF16)
              for i in n]
        return qg, big, p_col

    def state_steps(st, chunks, terms):
        qg, big, p_col = terms
        for i, c in enumerate(chunks):
            res = _dot(qg[i], st[0].astype(BF16))
            o_st = res[:C2] + big[i][C2:, C2:]
            os_ref[c * C:(c + 1) * C, :] = o_st[:C] + o_st[C:]
            st[0] = p_col[i] * st[0] + res[C2:] + big[i][:C2, C2:]
            yield

    st = [st_ref[...]]
    n_chunks = tt // C
    pending = iter(())
    for g0 in range(0, n_chunks, group):
        chunks = list(range(g0, min(g0 + group, n_chunks)))
        gen = chunk_terms(chunks)
        while True:
            try:
                next(gen)
            except StopIteration as done:
                terms = done.value
                break
            next(pending, None)
        for _ in pending:
            pass
        pending = state_steps(st, chunks, terms)
    for _ in pending:
        pass
    st_ref[...] = st[0]

    o = os_ref[...]
    blk = ((lax.broadcasted_iota(jnp.int32, (LANES, LANES), 0) < HEAD_DIM)
           == (lax.broadcasted_iota(jnp.int32, (LANES, LANES), 1) < HEAD_DIM)).astype(BF16)

    def head_sum(x):
        hi = x.astype(BF16)
        lo = (x - hi.astype(F32)).astype(BF16)
        return _dot(hi, blk) + _dot(lo, blk)

    mean = head_sum(o) * (1.0 / HEAD_DIM)
    cen = o - mean
    var = head_sum(cen * cen) * (1.0 / HEAD_DIM)
    y = cen * lax.rsqrt(var + RW_LNX_EPS) * lng_ref[...] + lnb_ref[...]
    r = r_ref[...].astype(F32)
    k = k_ref[...].astype(F32)
    bonus = head_sum(r * k * rk_ref[...]) * v_ref[...].astype(F32)
    o_ref[...] = (y + bonus).astype(BF16)


def _rw_scan(r, lw, k, v, kk, b, r_k, lnx_g, lnx_b, *, B, T, D, tt):
    hp = D // LANES
    blk = pl.BlockSpec((None, tt, LANES), lambda bi, h, t: (bi, t, h))
    vec = pl.BlockSpec((1, LANES), lambda bi, h, t: (0, h))
    r3 = lambda t: t.reshape(B, T, D)
    out = pl.pallas_call(
        functools.partial(_rw_scan_kernel, tt=tt, group=RW_GROUP),
        grid=(B, hp, T // tt),
        in_specs=[blk] * 6 + [vec] * 3,
        out_specs=blk,
        out_shape=jax.ShapeDtypeStruct((B, T, D), BF16),
        scratch_shapes=[pltpu.VMEM((LANES, LANES), F32), pltpu.VMEM((tt, LANES), F32)],
        compiler_params=_cparams("parallel", "parallel", "arbitrary"),
        name="rw_scan",
    )(r3(r), r3(lw), r3(k), r3(v), r3(kk), r3(b), r_k.reshape(1, D), lnx_g.reshape(1, D), lnx_b.reshape(1, D))
    return out.reshape(B * T, D)


def _diff_lambda_init(layer):
    return 0.8 - 0.6 * math.exp(-0.3 * layer)


def _pick_tile(n, pref):
    t = min(n, pref)
    while n % t:
        t //= 2
    return t


def kernel(x, c, norm_g, ada_w, ada_b, ffn_w_in, ffn_w_out, da_w_qkv, da_w_o, da_q_norm_g, da_k_norm_g, da_lambda, da_subln_g, rw_mu, rw_w_rkv, rw_w_o, rw_decay_w0, rw_decay_w1, rw_decay_w2, rw_iclr_a0, rw_iclr_a1, rw_iclr_a2, rw_gate_g1, rw_gate_g2, rw_k_k, rw_k_a, rw_r_k, rw_lnx_g, rw_lnx_b, sw_w_qkv, sw_w_o, sw_q_norm_g, sw_k_norm_g, sw_sinks):
    B, T, D = x.shape
    L = norm_g.shape[0]
    tm = _pick_tile(T, 512)
    tpb = T // tm
    kw = dict(tm=tm, tiles_per_batch=tpb)
    scale = HEAD_DIM ** -0.5

    mod = _ada(c, ada_w, ada_b).reshape(L, B, 3, 3, D)
    x2d = x.reshape(B * T, D)
    for i in range(L):
        kind, j = i % 3, i // 3
        x2d = _ffn(x2d, mod[i, :, 0], norm_g[i, 0], ffn_w_in[i, 0].astype(BF16), ffn_w_out[i, 0].astype(BF16), **kw)
        m2 = mod[i, :, 1]
        if kind == 0:
            qg = jnp.tile(da_q_norm_g[j] * (scale * math.log2(math.e)), D // HEAD_DIM).reshape(1, D)
            kg = jnp.tile(da_k_norm_g[j], D // HEAD_DIM).reshape(1, D)
            q, k, vt = _da_qkv(x2d, m2, norm_g[i, 1], da_w_qkv[j].astype(BF16), qg, kg, B=B, T=T, tm=tm)
            o = _da_attn(q, k, vt, da_lambda[j], da_subln_g[j], B=B, T=T, D=D, tq=_pick_tile(T, 512),
                         lambda_init=_diff_lambda_init(i))
            x2d = _out_proj(o, None, x2d, m2, da_w_o[j].astype(BF16), **kw)
        elif kind == 1:
            p = dict(mu=rw_mu[j], w_rkv=rw_w_rkv[j], decay_w0=rw_decay_w0[j], decay_w1=rw_decay_w1[j],
                     decay_w2=rw_decay_w2[j], iclr_a0=rw_iclr_a0[j], iclr_a1=rw_iclr_a1[j], iclr_a2=rw_iclr_a2[j],
                     gate_g1=rw_gate_g1[j], gate_g2=rw_gate_g2[j], k_k=rw_k_k[j], k_a=rw_k_a[j])
            r, lw, k, v, kk, b, g = _rw_prep(x2d, m2, norm_g[i, 1], p, **kw)
            y = _rw_scan(r, lw, k, v, kk, b, rw_r_k[j], rw_lnx_g[j], rw_lnx_b[j], B=B, T=T, D=D,
                         tt=_pick_tile(T, 2048))
            x2d = _out_proj(y, g, x2d, m2, rw_w_o[j].astype(BF16), **kw)
        else:
            kv = (sw_w_qkv.shape[-1] - D) // 2
            n_kv = kv // HEAD_DIM
            qg = jnp.tile(sw_q_norm_g[j] * (scale * math.log2(math.e)), D // HEAD_DIM).reshape(1, D)
            kg = jnp.tile(sw_k_norm_g[j], n_kv).reshape(1, kv)
            q, k, v = _sw_qkv(x2d, m2, norm_g[i, 1], sw_w_qkv[j].astype(BF16), qg, kg, **kw)
            dup = lambda t: jnp.repeat(t.reshape(B * T, n_kv, 1, HEAD_DIM), 2, axis=2).reshape(B * T, 2 * kv)
            x2d = _sw_attn(q, dup(k), dup(v), sw_sinks[j], x2d, m2, sw_w_o[j].astype(BF16), B=B, T=T,
                           tq=_pick_tile(T, 512), group=(D // HEAD_DIM) // n_kv)
        x2d = _ffn(x2d, mod[i, :, 2], norm_g[i, 2], ffn_w_in[i, 1].astype(BF16), ffn_w_out[i, 1].astype(BF16), **kw)
    return x2d.reshape(B, T, D)
```

```python
import functools
import math

import jax
import jax.numpy as jnp
from jax import lax
from jax.experimental import pallas as pl
from jax.experimental.pallas import tpu as pltpu

F32 = jnp.float32
BF16 = jnp.bfloat16
HIGHEST = lax.Precision.HIGHEST

HEAD_DIM = 64
LANES = 128
NORM_EPS = 1e-6
SUBLN_EPS = 1e-5
RW_LNX_EPS = 64e-5
SW_WINDOW = 128
N_MOD = 9
FFN_ROWS = 512
RW_GROUP = 8
RW_CHUNK = 64
VMEM_LIMIT = 56 * 1024 * 1024


def _cparams(*sem):
    return pltpu.CompilerParams(dimension_semantics=sem, vmem_limit_bytes=VMEM_LIMIT)


def _sigmoid(x):
    return 1.0 / (1.0 + jnp.exp(-x))


def _dot(a, b):
    return jnp.dot(a, b, preferred_element_type=F32)


def _dot_nt(a, b):
    return lax.dot_general(a, b, (((1,), (1,)), ((), ())), preferred_element_type=F32)


def _norm_mod(x, g, shift, scale):
    ms = jnp.mean(x * x, axis=-1, keepdims=True)
    return (x * lax.rsqrt(ms + NORM_EPS) * g) * (1.0 + scale) + shift


def _const_spec(shape):
    nd = len(shape)
    return pl.BlockSpec(shape, lambda *_: (0,) * nd, pipeline_mode=pl.Buffered(1))


def _group_mats(d):
    grp = jnp.arange(d) // HEAD_DIM
    gsum = (grp[:, None] == jnp.arange(LANES)[None, :]).astype(BF16)
    return gsum, gsum.T


def _group_bcast(v, gexp):
    hi = v.astype(BF16)
    lo = (v - hi.astype(F32)).astype(BF16)
    return _dot(hi, gexp) + _dot(lo, gexp)


def _ada_kernel(c_ref, w_ref, b_ref, o_ref):
    c = c_ref[...]
    cond = c * _sigmoid(c)
    o_ref[...] = jnp.dot(cond, w_ref[...], precision=HIGHEST, preferred_element_type=F32) + b_ref[...]


def _ada(c, ada_w, ada_b):
    L, D, ND = ada_w.shape
    B = c.shape[0]
    return pl.pallas_call(
        _ada_kernel,
        grid=(L, ND // D),
        in_specs=[pl.BlockSpec((B, D), lambda l, j: (0, 0)),
                  pl.BlockSpec((None, D, D), lambda l, j: (l, 0, j)),
                  pl.BlockSpec((None, 1, D), lambda l, j: (l, 0, j))],
        out_specs=pl.BlockSpec((None, B, D), lambda l, j: (l, 0, j)),
        out_shape=jax.ShapeDtypeStruct((L, B, ND), F32),
        compiler_params=_cparams("parallel", "parallel"),
        name="ada_mod",
    )(c, ada_w, ada_b.reshape(L, 1, ND))


def _ffn_kernel(*refs, d_ff, ck, rows, mixer):
    if mixer is None:
        x_ref, mod_ref, g_ref, win_ref, wout_ref, o_ref = refs
    elif mixer == "plain":
        x_ref, mod_ref, g_ref, win_ref, wout_ref, y_ref, mixmod_ref, wo_ref, o_ref = refs
    else:
        x_ref, mod_ref, g_ref, win_ref, wout_ref, y_ref, ygate_ref, mixmod_ref, wo_ref, o_ref = refs
    tm = x_ref.shape[0]
    subs = list(range(0, tm, rows))
    chunks = list(range(0, d_ff, ck))

    def residual_in(r0):
        x = x_ref[r0:r0 + rows, :]
        if mixer is None:
            return x
        y = y_ref[r0:r0 + rows, :]
        if mixer == "gated":
            y = (y.astype(F32) * ygate_ref[r0:r0 + rows, :].astype(F32)).astype(BF16)
        return x + mixmod_ref[2:3, :] * _dot(y, wo_ref[...])

    xs = {r0: residual_in(r0) for r0 in subs}
    hidden = {r0: _norm_mod(xs[r0], g_ref[...], mod_ref[0:1, :], mod_ref[1:2, :]).astype(BF16) for r0 in subs}

    def gate_up(r0, c0):
        h = hidden[r0]
        return _dot(h, win_ref[:, c0:c0 + ck]), _dot(h, win_ref[:, d_ff + c0:d_ff + c0 + ck])

    jobs = [(r0, c0) for r0 in subs for c0 in chunks]
    pending = gate_up(*jobs[0])
    acc = None
    for n, (r0, c0) in enumerate(jobs):
        gate, up = pending
        if n + 1 < len(jobs):
            pending = gate_up(*jobs[n + 1])
        act = (gate * _sigmoid(gate) * up).astype(BF16)
        y = _dot(act, wout_ref[c0:c0 + ck, :])
        acc = y if c0 == 0 else acc + y
        if c0 == chunks[-1]:
            o_ref[r0:r0 + rows, :] = xs[r0] + (0.5 * mod_ref[2:3, :]) * acc


def _ffn_chunk(d_ff):
    for n in (2, 1, 4, 11, 22):
        if d_ff % n == 0 and (d_ff // n) % LANES == 0:
            return d_ff // n
    return d_ff


def _ffn(x2d, mod3, g, w_in, w_out, *, tm, tiles_per_batch, mix=None):
    N, D = x2d.shape
    d_ff = w_out.shape[0]
    tile = pl.BlockSpec((tm, D), lambda i: (i, 0))
    modspec = pl.BlockSpec((None, 3, D), lambda i: (i // tiles_per_batch, 0, 0))
    ins = [x2d, mod3, g.reshape(1, D), w_in, w_out]
    specs = [tile, modspec, _const_spec((1, D)), _const_spec(w_in.shape), _const_spec(w_out.shape)]
    mixer = None
    if mix is not None:
        y, y_gate, mixmod, w_o = mix
        mixer = "plain" if y_gate is None else "gated"
        ins += [y] + ([] if y_gate is None else [y_gate]) + [mixmod, w_o]
        specs += [tile] + ([] if y_gate is None else [tile]) + [modspec, _const_spec(w_o.shape)]
    kern = functools.partial(_ffn_kernel, d_ff=d_ff, ck=_ffn_chunk(d_ff), rows=min(tm, FFN_ROWS), mixer=mixer)
    return pl.pallas_call(
        kern,
        grid=(N // tm,),
        in_specs=specs,
        out_specs=tile,
        out_shape=jax.ShapeDtypeStruct((N, D), F32),
        compiler_params=_cparams("parallel"),
        name="ffn_half" if mixer is None else "ffn_half_" + mixer,
    )(*ins)


def _group_rms_scale(y, gsum, gexp):
    ss = _dot((y * y).astype(BF16), gsum)
    return _group_bcast(lax.rsqrt(ss * (1.0 / HEAD_DIM) + NORM_EPS), gexp)


def _da_qkv_kernel(x_ref, mod_ref, g_ref, w_ref, gsum_ref, gexp_ref, qg_ref, kg_ref, q_ref, k_ref, vt_ref):
    D = x_ref.shape[1]
    h = _norm_mod(x_ref[...], g_ref[...], mod_ref[0:1, :], mod_ref[1:2, :]).astype(BF16)
    q = _dot(h, w_ref[:, 0:D])
    k = _dot(h, w_ref[:, D:2 * D])
    v = _dot(h, w_ref[:, 2 * D:3 * D])
    vt_ref[...] = v.T.astype(BF16)
    q_ref[...] = (q * _group_rms_scale(q, gsum_ref[...], gexp_ref[...]) * qg_ref[...]).astype(BF16)
    k_ref[...] = (k * _group_rms_scale(k, gsum_ref[...], gexp_ref[...]) * kg_ref[...]).astype(BF16)


def _da_qkv(x2d, mod3, g, w_qkv, q_gain, k_gain, *, B, T, tm):
    N, D = x2d.shape
    tpb = T // tm
    gsum, gexp = _group_mats(D)
    tile = pl.BlockSpec((tm, D), lambda i: (i, 0))
    return pl.pallas_call(
        _da_qkv_kernel,
        grid=(N // tm,),
        in_specs=[tile,
                  pl.BlockSpec((None, 3, D), lambda i: (i // tpb, 0, 0)),
                  _const_spec((1, D)), _const_spec(w_qkv.shape), _const_spec(gsum.shape),
                  _const_spec(gexp.shape), _const_spec((1, D)), _const_spec((1, D))],
        out_specs=[tile, tile, pl.BlockSpec((None, D, tm), lambda i: (i // tpb, 0, i % tpb))],
        out_shape=[jax.ShapeDtypeStruct((N, D), BF16), jax.ShapeDtypeStruct((N, D), BF16),
                   jax.ShapeDtypeStruct((B, D, T), BF16)],
        compiler_params=_cparams("parallel"),
        name="da_qkv",
    )(x2d, mod3, g.reshape(1, D), w_qkv, gsum, gexp, q_gain, k_gain)


def _da_attn_kernel(lam_ref, sg_ref, q_ref, k_ref, vt_ref, o_ref, acc_ref, sa_ref, sb_ref, *, tq):
    nq = q_ref.shape[0] // tq
    lane = lax.broadcasted_iota(jnp.int32, (1, LANES), 1)
    lam = lam_ref[...]
    lambda_init = lam[4:5, 0:1]
    lam_full = (jnp.exp(jnp.sum(lam[0:1] * lam[1:2], axis=1, keepdims=True))
                - jnp.exp(jnp.sum(lam[2:3] * lam[3:4], axis=1, keepdims=True)) + lambda_init)
    kpos = lax.broadcasted_iota(jnp.int32, (tq, tq), 0)
    qpos = lax.broadcasted_iota(jnp.int32, (tq, tq), 1)
    causal = kpos <= qpos
    bufs = (sa_ref, sb_ref)
    q_maps = {}

    def q_masked(qi):
        if qi not in q_maps:
            q = q_ref[qi * tq:(qi + 1) * tq, :]
            zero = jnp.zeros_like(q)
            q_maps[qi] = (jnp.where(lane < HEAD_DIM, q, zero), jnp.where(lane >= HEAD_DIM, q, zero))
        return q_maps[qi]

    def scores_into(s_ref, qi, j):
        k = k_ref[j * tq:(j + 1) * tq, :]
        for c in range(2):
            s_ref[c] = _dot_nt(k, q_masked(qi)[c])

    def consume(s_ref, acc, qi, j, carry):
        vt = vt_ref[:, j * tq:(j + 1) * tq]
        out = []
        for c in range(2):
            s = s_ref[c]
            if j == qi:
                s = jnp.where(causal, s, -jnp.inf)
            m_cur = jnp.max(s, axis=0, keepdims=True)
            if j == 0:
                m_new = m_cur
                p = jnp.exp2(s - m_new)
                l_new = jnp.sum(p, axis=0, keepdims=True)
                acc[c] = _dot(vt, p.astype(BF16))
            else:
                m_prev, l_prev = carry[2 * c], carry[2 * c + 1]
                m_new = jnp.maximum(m_prev, m_cur)
                alpha = jnp.exp2(m_prev - m_new)
                p = jnp.exp2(s - m_new)
                l_new = alpha * l_prev + jnp.sum(p, axis=0, keepdims=True)
                acc[c] = alpha * acc[c] + _dot(vt, p.astype(BF16))
            out += [m_new, l_new]
        return tuple(out)

    def finish(acc, qi, carry):
        ot = acc[0] / carry[1] - lam_full * (acc[1] / carry[3])
        ms = jnp.mean(ot * ot, axis=0, keepdims=True)
        ot = ot * lax.rsqrt(ms + SUBLN_EPS)
        o_ref[qi * tq:(qi + 1) * tq, :] = (ot.T * sg_ref[...] * (1.0 - lambda_init)).astype(BF16)

    tasks = [(qi, j) for qi in range(nq) for j in range(qi + 1)]
    scores_into(bufs[0], *tasks[0])
    carry = None
    for t, (qi, j) in enumerate(tasks):
        if t + 1 < len(tasks):
            scores_into(bufs[(t + 1) % 2], *tasks[t + 1])
        acc = acc_ref.at[qi % 2]
        carry = consume(bufs[t % 2], acc, qi, j, carry)
        if j == qi:
            finish(acc, qi, carry)


def _da_attn(q, k, vt, lambdas, subln_g, *, B, T, D, tq, lambda_init):
    H = D // LANES
    q3, k3 = q.reshape(B, T, D), k.reshape(B, T, D)
    seq = pl.BlockSpec((None, T, LANES), lambda b, h: (b, 0, h))
    out = pl.pallas_call(
        functools.partial(_da_attn_kernel, tq=tq),
        grid=(B, H),
        in_specs=[pl.BlockSpec((5, HEAD_DIM), lambda b, h: (0, 0)),
                  pl.BlockSpec((1, LANES), lambda b, h: (0, 0)),
                  seq, seq,
                  pl.BlockSpec((None, LANES, T), lambda b, h: (b, h, 0))],
        out_specs=seq,
        out_shape=jax.ShapeDtypeStruct((B, T, D), BF16),
        scratch_shapes=[pltpu.VMEM((2, 2, LANES, tq), F32),
                        pltpu.VMEM((2, tq, tq), F32), pltpu.VMEM((2, tq, tq), F32)],
        compiler_params=_cparams("parallel", "parallel"),
        name="da_attn",
    )(jnp.concatenate([lambdas, jnp.full((1, HEAD_DIM), lambda_init, F32)], axis=0),
      subln_g.reshape(1, LANES), q3, k3, vt)
    return out.reshape(B * T, D)


def _sw_qkv_kernel(x_ref, mod_ref, g_ref, w_ref, gsum_ref, gexp_ref, ksum_ref, kexp_ref, qg_ref, kg_ref,
                   q_ref, k_ref, v_ref):
    D = x_ref.shape[1]
    h = _norm_mod(x_ref[...], g_ref[...], mod_ref[0:1, :], mod_ref[1:2, :]).astype(BF16)
    kv = k_ref.shape[1]
    q = _dot(h, w_ref[:, 0:D])
    k = _dot(h, w_ref[:, D:D + kv])
    v_ref[...] = _dot(h, w_ref[:, D + kv:D + 2 * kv]).astype(BF16)
    q_ref[...] = (q * _group_rms_scale(q, gsum_ref[...], gexp_ref[...]) * qg_ref[...]).astype(BF16)
    k_ref[...] = (k * _group_rms_scale(k, ksum_ref[...], kexp_ref[...]) * kg_ref[...]).astype(BF16)


def _sw_qkv(x2d, mod3, g, w_qkv, q_gain, k_gain, *, tm, tiles_per_batch):
    N, D = x2d.shape
    kv = (w_qkv.shape[1] - D) // 2
    gsum, gexp = _group_mats(D)
    ksum, kexp = _group_mats(kv)
    tile = pl.BlockSpec((tm, D), lambda i: (i, 0))
    kvtile = pl.BlockSpec((tm, kv), lambda i: (i, 0))
    return pl.pallas_call(
        _sw_qkv_kernel,
        grid=(N // tm,),
        in_specs=[tile, pl.BlockSpec((None, 3, D), lambda i: (i // tiles_per_batch, 0, 0)),
                  _const_spec((1, D)), _const_spec(w_qkv.shape), _const_spec(gsum.shape),
                  _const_spec(gexp.shape), _const_spec(ksum.shape), _const_spec(kexp.shape),
                  _const_spec((1, D)), _const_spec((1, kv))],
        out_specs=[tile, kvtile, kvtile],
        out_shape=[jax.ShapeDtypeStruct((N, D), BF16), jax.ShapeDtypeStruct((N, kv), BF16),
                   jax.ShapeDtypeStruct((N, kv), BF16)],
        compiler_params=_cparams("parallel"),
        name="sw_qkv",
    )(x2d, mod3, g.reshape(1, D), w_qkv, gsum, gexp, ksum, kexp, q_gain, k_gain)


def _sw_attn_kernel(sink_ref, q_ref, k_ref, v_ref, x_ref, mod_ref, wo_ref, o_ref, att_ref, *, tq, group):
    i = pl.program_id(1)
    D = q_ref.shape[1]
    W = SW_WINDOW
    n_kv = (D // HEAD_DIM) // group
    tiles_per_kv = group // 2
    lane = lax.broadcasted_iota(jnp.int32, (1, LANES), 1)
    first = lane < HEAD_DIM
    jobs = [(sub, kvh) for sub in range(tq // W) for kvh in range(n_kv)]

    def band_start(sub):
        return pl.multiple_of(jnp.maximum(i * tq + (sub - 1) * W, 0), W)

    def scores(sub, kvh):
        k = k_ref[pl.ds(band_start(sub), 2 * W), kvh * LANES:(kvh + 1) * LANES]
        parts = []
        for b in range(tiles_per_kv):
            blk = kvh * tiles_per_kv + b
            q = q_ref[sub * W:(sub + 1) * W, blk * LANES:(blk + 1) * LANES]
            zero = jnp.zeros_like(q)
            parts += [jnp.where(first, q, zero), jnp.where(first, zero, q)]
        return _dot_nt(jnp.concatenate(parts, axis=0), k)

    pending = scores(*jobs[0])
    for n, (sub, kvh) in enumerate(jobs):
        s_all = pending
        if n + 1 < len(jobs):
            pending = scores(*jobs[n + 1])
        qpos = i * tq + sub * W + lax.broadcasted_iota(jnp.int32, (W, 2 * W), 0)
        kpos = band_start(sub) + lax.broadcasted_iota(jnp.int32, (W, 2 * W), 1)
        valid = jnp.logical_and(kpos <= qpos, kpos > qpos - W)
        probs, inv = [], []
        for g in range(group):
            s = jnp.where(valid, s_all[g * W:(g + 1) * W], -jnp.inf)
            snk = sink_ref[kvh * group + g] * math.log2(math.e)
            m = jnp.maximum(jnp.max(s, axis=1, keepdims=True), snk)
            p = jnp.exp2(s - m)
            inv.append(1.0 / (jnp.sum(p, axis=1, keepdims=True) + jnp.exp2(snk - m)))
            probs.append(p.astype(BF16))
        v = v_ref[pl.ds(band_start(sub), 2 * W), kvh * LANES:(kvh + 1) * LANES]
        o_all = _dot(jnp.concatenate(probs, axis=0), v)
        for b in range(tiles_per_kv):
            blk = kvh * tiles_per_kv + b
            o0 = o_all[2 * b * W:(2 * b + 1) * W] * inv[2 * b]
            o1 = o_all[(2 * b + 1) * W:(2 * b + 2) * W] * inv[2 * b + 1]
            att_ref[sub * W:(sub + 1) * W, blk * LANES:(blk + 1) * LANES] = jnp.where(first, o0, o1).astype(BF16)
    o_ref[...] = x_ref[...] + mod_ref[2:3, :] * _dot(att_ref[...], wo_ref[...])


def _sw_attn(q, kd, vd, sinks, x2d, mod3, w_o, *, B, T, tq, group):
    N, D = x2d.shape
    nq = T // tq
    kvd = kd.shape[1]
    tile = pl.BlockSpec((tq, D), lambda b, i: (b * nq + i, 0))
    seq = pl.BlockSpec((None, T, kvd), lambda b, i: (b, 0, 0))
    return pl.pallas_call(
        functools.partial(_sw_attn_kernel, tq=tq, group=group),
        grid=(B, nq),
        in_specs=[pl.BlockSpec(memory_space=pltpu.SMEM), tile, seq, seq, tile,
                  pl.BlockSpec((None, 3, D), lambda b, i: (b, 0, 0)), _const_spec(w_o.shape)],
        out_specs=tile,
        out_shape=jax.ShapeDtypeStruct((N, D), F32),
        scratch_shapes=[pltpu.VMEM((tq, D), BF16)],
        compiler_params=_cparams("parallel", "arbitrary"),
        name="sw_attn",
    )(sinks, q, kd.reshape(B, T, kvd), vd.reshape(B, T, kvd), x2d, mod3, w_o)


def _rw_prep_kernel(x_ref, xp_ref, mod_ref, g_ref, mu_ref, wrkv_ref, w0_ref, w1_ref, w2_ref, a0_ref, a1_ref,
                    a2_ref, g1_ref, g2_ref, kk_ref, ka_ref, gsum_ref, gexp_ref,
                    r_out, lw_out, k_out, v_out, kk_out, b_out, g_out, *, tiles_per_batch):
    D = x_ref.shape[1]
    tm = x_ref.shape[0]
    shift, scale, gn = mod_ref[0:1, :], mod_ref[1:2, :], g_ref[...]
    h = _norm_mod(x_ref[...], gn, shift, scale)
    hp = _norm_mod(xp_ref[...], gn, shift, scale)[7:8, :]
    hp = jnp.where(pl.program_id(0) % tiles_per_batch == 0, jnp.zeros_like(hp), hp)
    row = lax.broadcasted_iota(jnp.int32, (tm, 1), 0)
    xx = jnp.where(row == 0, hp, pltpu.roll(h, 1, axis=0)) - h

    def mixed(n):
        return (h + xx * mu_ref[n:n + 1, :]).astype(BF16)

    d1 = _dot(mixed(3), w1_ref[...])
    a1 = _dot(mixed(4), a1_ref[...])
    g1 = _dot(mixed(5), g1_ref[...])
    r = _dot(mixed(0), wrkv_ref[0])
    k = _dot(mixed(1), wrkv_ref[1])
    v = _dot(mixed(2), wrkv_ref[2])
    d = _dot(jnp.tanh(d1).astype(BF16), w2_ref[...]) + w0_ref[...]
    a = _sigmoid(a0_ref[...] + _dot(a1.astype(BF16), a2_ref[...]))
    g_out[...] = _dot(_sigmoid(g1).astype(BF16), g2_ref[...]).astype(BF16)
    z = -d
    softplus = jnp.maximum(z, 0.0) + jnp.log(1.0 + jnp.exp(-jnp.abs(z)))
    lw_out[...] = -jnp.exp(-softplus - 0.5)
    kk = k * kk_ref[...]
    ss = _dot((kk * kk).astype(BF16), gsum_ref[...])
    kk = kk * _group_bcast(1.0 / jnp.maximum(jnp.sqrt(ss), 1e-12), gexp_ref[...])
    r_out[...] = r.astype(BF16)
    k_out[...] = (k * (1.0 + (a - 1.0) * ka_ref[...])).astype(BF16)
    v_out[...] = v.astype(BF16)
    kk_out[...] = kk.astype(BF16)
    b_out[...] = (kk * a).astype(BF16)


def _pad_to(w, axis, n):
    pad = [(0, 0)] * w.ndim
    pad[axis] = (0, n - w.shape[axis])
    return jnp.pad(w, pad)


def _round_up(n, m):
    return -(-n // m) * m


def _rw_prep(x2d, mod3, g, p, *, tm, tiles_per_batch):
    N, D = x2d.shape
    gsum, gexp = _group_mats(D)
    row = lambda t: t.reshape(1, D)
    lora = lambda w1, w2: (_pad_to(w1, 1, _round_up(w1.shape[1], LANES)).astype(BF16),
                           _pad_to(w2, 0, _round_up(w2.shape[0], LANES)).astype(BF16))
    w1, w2 = lora(p["decay_w1"], p["decay_w2"])
    a1, a2 = lora(p["iclr_a1"], p["iclr_a2"])
    g1, g2 = lora(p["gate_g1"], p["gate_g2"])
    consts = [row(g), p["mu"], p["w_rkv"].astype(BF16), row(p["decay_w0"]), w1, w2, row(p["iclr_a0"]), a1, a2,
              g1, g2, row(p["k_k"]), row(p["k_a"]), gsum, gexp]
    tile = pl.BlockSpec((tm, D), lambda i: (i, 0))
    outs = [jax.ShapeDtypeStruct((N, D), BF16)] * 7
    outs[1] = jax.ShapeDtypeStruct((N, D), F32)
    return pl.pallas_call(
        functools.partial(_rw_prep_kernel, tiles_per_batch=tiles_per_batch),
        grid=(N // tm,),
        in_specs=[tile,
                  pl.BlockSpec((8, D), lambda i: (jnp.maximum(i * (tm // 8) - 1, 0), 0)),
                  pl.BlockSpec((None, 3, D), lambda i: (i // tiles_per_batch, 0, 0))]
                 + [_const_spec(c.shape) for c in consts],
        out_specs=[tile] * 7,
        out_shape=outs,
        compiler_params=_cparams("parallel"),
        name="rw_prep",
    )(x2d, x2d, mod3, *consts)


def _stack_heads(x, m0):
    z = jnp.zeros_like(x)
    return jnp.concatenate([jnp.where(m0, x, z), jnp.where(m0, z, x)], axis=0)


def _rw_scan_kernel(r_ref, lw_ref, k_ref, v_ref, kk_ref, b_ref, rk_ref, lng_ref, lnb_ref, o_ref, st_ref, os_ref,
                    *, tt, group):
    C = RW_CHUNK
    C2 = 2 * C

    @pl.when(pl.program_id(2) == 0)
    def _():
        st_ref[...] = jnp.zeros_like(st_ref)

    lane = lax.broadcasted_iota(jnp.int32, (1, LANES), 1)
    m0 = lane < HEAD_DIM
    ti = lax.broadcasted_iota(jnp.int32, (C, C), 0)
    tj = lax.broadcasted_iota(jnp.int32, (C, C), 1)
    tri = (ti >= tj).astype(F32)
    ri = lax.broadcasted_iota(jnp.int32, (2 * C2, 2 * C2), 0)
    ci = lax.broadcasted_iota(jnp.int32, (2 * C2, 2 * C2), 1)
    rt, ct = ri % C, ci % C
    causal = rt - ct + (ri >= C2).astype(jnp.int32) > 0
    eye = (lax.broadcasted_iota(jnp.int32, (C2, C2), 0) == lax.broadcasted_iota(jnp.int32, (C2, C2), 1))
    eye_f = eye.astype(F32)
    si = lax.broadcasted_iota(jnp.int32, (C2, C2), 0)
    sj = lax.broadcasted_iota(jnp.int32, (C2, C2), 1)
    merge_masks = []
    s = 1
    while s < C:
        merge_masks.append(jnp.logical_and(si // (2 * s) == sj // (2 * s),
                                           jnp.logical_and(si % (2 * s) >= s, sj % (2 * s) < s)))
        s *= 2
    zeros_sq = jnp.zeros((C2, C2), BF16)

    def chunk_terms(chunks):
        n = range(len(chunks))
        a_t, r_t, v_s, aa, lhs_top, p_col = [], [], [], [], [], []
        for i, c in enumerate(chunks):
            rows = slice(c * C, (c + 1) * C)
            lw = lw_ref[rows, :]
            r = r_ref[rows, :].astype(F32)
            k = k_ref[rows, :].astype(F32)
            kk = kk_ref[rows, :].astype(F32)
            b = b_ref[rows, :].astype(F32)
            cum = jnp.dot(tri, lw, precision=HIGHEST, preferred_element_type=F32)
            tot = cum[C - 1:C, :]
            e_neg = jnp.exp(-cum)
            r_t.append(_stack_heads(r * jnp.exp(cum), m0).astype(BF16))
            a_t.append(_stack_heads(-kk * jnp.exp(cum - lw), m0).astype(BF16))
            b_t = _stack_heads(b * e_neg, m0).astype(BF16)
            k_t = _stack_heads(k * e_neg, m0).astype(BF16)
            e_end = jnp.exp(tot - cum)
            b_h = _stack_heads(b * e_end, m0)
            k_h = _stack_heads(k * e_end, m0)
            lhs_top.append(jnp.concatenate([b_h.T, k_h.T], axis=1).astype(BF16))
            v_s.append(_stack_heads(v_ref[rows, :], m0))
            p_col.append(jnp.sum(jnp.where(eye, jnp.exp(tot), 0.0), axis=1, keepdims=True))
            full = _dot_nt(jnp.concatenate([a_t[i], r_t[i]], axis=0), jnp.concatenate([b_t, k_t], axis=0))
            aa.append(jnp.where(causal, full, 0.0))
        yield

        a_ab = [aa[i][:C2, :C2] for i in n]
        t_inv = [eye_f + jnp.where(merge_masks[0], a_ab[i], 0.0) for i in n]
        for mask in merge_masks[1:]:
            tb = [t_inv[i].astype(BF16) for i in n]
            half = [_dot(tb[i], jnp.where(mask, a_ab[i], 0.0).astype(BF16)).astype(BF16) for i in n]
            yield
            t_inv = [t_inv[i] + _dot(half[i], tb[i]) for i in n]
            yield
        akv = [_dot(aa[i][:C2, C2:].astype(BF16), v_s[i]).astype(BF16) for i in n]
        yield
        au = [_dot(t_inv[i].astype(BF16), jnp.concatenate([a_t[i], akv[i]], axis=1)).astype(BF16)
              for i in n]
        yield
        big = []
        for i in n:
            rhs = jnp.concatenate([au[i], jnp.concatenate([zeros_sq, v_s[i]], axis=1)], axis=0)
            lhs = jnp.concatenate([lhs_top[i], aa[i][C2:, :].astype(BF16)], axis=0)
            big.append(_dot(lhs, rhs))
        qg = [jnp.concatenate([big[i][C2:, :C2] + r_t[i].astype(F32), big[i][:C2, :C2]], axis=0).astype(BF16)
              for i in n]
        return qg, big, p_col

    def state_steps(st, chunks, terms):
        qg, big, p_col = terms
        for i, c in enumerate(chunks):
            res = _dot(qg[i], st[0].astype(BF16))
            o_st = res[:C2] + big[i][C2:, C2:]
            os_ref[c * C:(c + 1) * C, :] = o_st[:C] + o_st[C:]
            st[0] = p_col[i] * st[0] + res[C2:] + big[i][:C2, C2:]
            yield

    st = [st_ref[...]]
    n_chunks = tt // C
    pending = iter(())
    for g0 in range(0, n_chunks, group):
        chunks = list(range(g0, min(g0 + group, n_chunks)))
        gen = chunk_terms(chunks)
        while True:
            try:
                next(gen)
            except StopIteration as done:
                terms = done.value
                break
            next(pending, None)
        for _ in pending:
            pass
        pending = state_steps(st, chunks, terms)
    for _ in pending:
        pass
    st_ref[...] = st[0]

    o = os_ref[...]
    blk = ((lax.broadcasted_iota(jnp.int32, (LANES, LANES), 0) < HEAD_DIM)
           == (lax.broadcasted_iota(jnp.int32, (LANES, LANES), 1) < HEAD_DIM)).astype(BF16)

    def head_sum(x):
        hi = x.astype(BF16)
        lo = (x - hi.astype(F32)).astype(BF16)
        return _dot(hi, blk) + _dot(lo, blk)

    mean = head_sum(o) * (1.0 / HEAD_DIM)
    cen = o - mean
    var = head_sum(cen * cen) * (1.0 / HEAD_DIM)
    y = cen * lax.rsqrt(var + RW_LNX_EPS) * lng_ref[...] + lnb_ref[...]
    r = r_ref[...].astype(F32)
    k = k_ref[...].astype(F32)
    bonus = head_sum(r * k * rk_ref[...]) * v_ref[...].astype(F32)
    o_ref[...] = (y + bonus).astype(BF16)


def _rw_scan(r, lw, k, v, kk, b, r_k, lnx_g, lnx_b, *, B, T, D, tt):
    hp = D // LANES
    blk = pl.BlockSpec((None, tt, LANES), lambda bi, h, t: (bi, t, h))
    vec = pl.BlockSpec((1, LANES), lambda bi, h, t: (0, h))
    r3 = lambda t: t.reshape(B, T, D)
    out = pl.pallas_call(
        functools.partial(_rw_scan_kernel, tt=tt, group=RW_GROUP),
        grid=(B, hp, T // tt),
        in_specs=[blk] * 6 + [vec] * 3,
        out_specs=blk,
        out_shape=jax.ShapeDtypeStruct((B, T, D), BF16),
        scratch_shapes=[pltpu.VMEM((LANES, LANES), F32), pltpu.VMEM((tt, LANES), F32)],
        compiler_params=_cparams("parallel", "parallel", "arbitrary"),
        name="rw_scan",
    )(r3(r), r3(lw), r3(k), r3(v), r3(kk), r3(b), r_k.reshape(1, D), lnx_g.reshape(1, D), lnx_b.reshape(1, D))
    return out.reshape(B * T, D)


def _diff_lambda_init(layer):
    return 0.8 - 0.6 * math.exp(-0.3 * layer)


def _pick_tile(n, pref):
    t = min(n, pref)
    while n % t:
        t //= 2
    return t


def kernel(x, c, norm_g, ada_w, ada_b, ffn_w_in, ffn_w_out, da_w_qkv, da_w_o, da_q_norm_g, da_k_norm_g, da_lambda, da_subln_g, rw_mu, rw_w_rkv, rw_w_o, rw_decay_w0, rw_decay_w1, rw_decay_w2, rw_iclr_a0, rw_iclr_a1, rw_iclr_a2, rw_gate_g1, rw_gate_g2, rw_k_k, rw_k_a, rw_r_k, rw_lnx_g, rw_lnx_b, sw_w_qkv, sw_w_o, sw_q_norm_g, sw_k_norm_g, sw_sinks):
    B, T, D = x.shape
    L = norm_g.shape[0]
    tm = _pick_tile(T, 512)
    tpb = T // tm
    kw = dict(tm=tm, tiles_per_batch=tpb)
    tf = _pick_tile(T, 2 * FFN_ROWS)
    fkw = dict(tm=tf, tiles_per_batch=T // tf)
    scale = HEAD_DIM ** -0.5

    mod = _ada(c, ada_w, ada_b).reshape(L, B, 3, 3, D)
    x2d = x.reshape(B * T, D)
    for i in range(L):
        kind, j = i % 3, i // 3
        x2d = _ffn(x2d, mod[i, :, 0], norm_g[i, 0], ffn_w_in[i, 0].astype(BF16), ffn_w_out[i, 0].astype(BF16), **fkw)
        m2 = mod[i, :, 1]
        if kind == 0:
            qg = jnp.tile(da_q_norm_g[j] * (scale * math.log2(math.e)), D // HEAD_DIM).reshape(1, D)
            kg = jnp.tile(da_k_norm_g[j], D // HEAD_DIM).reshape(1, D)
            q, k, vt = _da_qkv(x2d, m2, norm_g[i, 1], da_w_qkv[j].astype(BF16), qg, kg, B=B, T=T, tm=tm)
            o = _da_attn(q, k, vt, da_lambda[j], da_subln_g[j], B=B, T=T, D=D, tq=_pick_tile(T, 512),
                         lambda_init=_diff_lambda_init(i))
            mix = (o, None, m2, da_w_o[j].astype(BF16))
        elif kind == 1:
            p = dict(mu=rw_mu[j], w_rkv=rw_w_rkv[j], decay_w0=rw_decay_w0[j], decay_w1=rw_decay_w1[j],
                     decay_w2=rw_decay_w2[j], iclr_a0=rw_iclr_a0[j], iclr_a1=rw_iclr_a1[j], iclr_a2=rw_iclr_a2[j],
                     gate_g1=rw_gate_g1[j], gate_g2=rw_gate_g2[j], k_k=rw_k_k[j], k_a=rw_k_a[j])
            r, lw, k, v, kk, b, g = _rw_prep(x2d, m2, norm_g[i, 1], p, **kw)
            y = _rw_scan(r, lw, k, v, kk, b, rw_r_k[j], rw_lnx_g[j], rw_lnx_b[j], B=B, T=T, D=D,
                         tt=_pick_tile(T, 2048))
            mix = (y, g, m2, rw_w_o[j].astype(BF16))
        else:
            kv = (sw_w_qkv.shape[-1] - D) // 2
            n_kv = kv // HEAD_DIM
            qg = jnp.tile(sw_q_norm_g[j] * (scale * math.log2(math.e)), D // HEAD_DIM).reshape(1, D)
            dup = lambda w: jnp.repeat(w.reshape(D, n_kv, 1, HEAD_DIM), 2, axis=2).reshape(D, 2 * kv)
            w = sw_w_qkv[j]
            w = jnp.concatenate([w[:, :D], dup(w[:, D:D + kv]), dup(w[:, D + kv:])], axis=1).astype(BF16)
            kg = jnp.tile(sw_k_norm_g[j], 2 * n_kv).reshape(1, 2 * kv)
            q, k, v = _sw_qkv(x2d, m2, norm_g[i, 1], w, qg, kg, **kw)
            x2d = _sw_attn(q, k, v, sw_sinks[j], x2d, m2, sw_w_o[j].astype(BF16), B=B, T=T,
                           tq=_pick_tile(T, 512), group=(D // HEAD_DIM) // n_kv)
            mix = None
        x2d = _ffn(x2d, mod[i, :, 2], norm_g[i, 2], ffn_w_in[i, 1].astype(BF16), ffn_w_out[i, 1].astype(BF16),
                   mix=mix, **fkw)
    return x2d.reshape(B, T, D)
```

```python
import functools
import math

import jax
import jax.numpy as jnp
from jax import lax
from jax.experimental import pallas as pl
from jax.experimental.pallas import tpu as pltpu

F32 = jnp.float32
BF16 = jnp.bfloat16
HIGHEST = lax.Precision.HIGHEST

HEAD_DIM = 64
LANES = 128
NORM_EPS = 1e-6
SUBLN_EPS = 1e-5
RW_LNX_EPS = 64e-5
SW_WINDOW = 128
N_MOD = 9
FFN_ROWS = 512
RW_GROUP = 8
RW_CHUNK = 64
VMEM_LIMIT = 56 * 1024 * 1024


def _cparams(*sem):
    return pltpu.CompilerParams(dimension_semantics=sem, vmem_limit_bytes=VMEM_LIMIT)


def _sigmoid(x):
    return 1.0 / (1.0 + jnp.exp(-x))


def _dot(a, b):
    return jnp.dot(a, b, preferred_element_type=F32)


def _dot_nt(a, b):
    return lax.dot_general(a, b, (((1,), (1,)), ((), ())), preferred_element_type=F32)


def _norm_mod(x, g, shift, scale):
    ms = jnp.mean(x * x, axis=-1, keepdims=True)
    return (x * lax.rsqrt(ms + NORM_EPS) * g) * (1.0 + scale) + shift


def _const_spec(shape):
    nd = len(shape)
    return pl.BlockSpec(shape, lambda *_: (0,) * nd, pipeline_mode=pl.Buffered(1))


def _group_mats(d):
    grp = jnp.arange(d) // HEAD_DIM
    gsum = (grp[:, None] == jnp.arange(LANES)[None, :]).astype(BF16)
    return gsum, jnp.concatenate([gsum.T, gsum.T], axis=0)


def _group_bcast(v, gexp):
    hi = v.astype(BF16)
    lo = (v - hi.astype(F32)).astype(BF16)
    return _dot(jnp.concatenate([hi, lo], axis=1), gexp)


def _ada_kernel(c_ref, w_ref, b_ref, o_ref):
    c = c_ref[...]
    cond = c * _sigmoid(c)
    o_ref[...] = jnp.dot(cond, w_ref[...], precision=HIGHEST, preferred_element_type=F32) + b_ref[...]


def _ada(c, ada_w, ada_b):
    L, D, ND = ada_w.shape
    B = c.shape[0]
    return pl.pallas_call(
        _ada_kernel,
        grid=(L, ND // D),
        in_specs=[pl.BlockSpec((B, D), lambda l, j: (0, 0)),
                  pl.BlockSpec((None, D, D), lambda l, j: (l, 0, j)),
                  pl.BlockSpec((None, 1, D), lambda l, j: (l, 0, j))],
        out_specs=pl.BlockSpec((None, B, D), lambda l, j: (l, 0, j)),
        out_shape=jax.ShapeDtypeStruct((L, B, ND), F32),
        compiler_params=_cparams("parallel", "parallel"),
        name="ada_mod",
    )(c, ada_w, ada_b.reshape(L, 1, ND))


def _ffn_kernel(*refs, d_ff, ck, rows, mixer):
    if mixer is None:
        x_ref, mod_ref, g_ref, win_ref, wout_ref, o_ref = refs
    elif mixer == "plain":
        x_ref, mod_ref, g_ref, win_ref, wout_ref, y_ref, mixmod_ref, wo_ref, o_ref = refs
    else:
        x_ref, mod_ref, g_ref, win_ref, wout_ref, y_ref, ygate_ref, mixmod_ref, wo_ref, o_ref = refs
    tm = x_ref.shape[0]
    subs = list(range(0, tm, rows))
    chunks = list(range(0, d_ff, ck))

    def residual_in(r0):
        x = x_ref[r0:r0 + rows, :]
        if mixer is None:
            return x
        y = y_ref[r0:r0 + rows, :]
        if mixer == "gated":
            y = (y.astype(F32) * ygate_ref[r0:r0 + rows, :].astype(F32)).astype(BF16)
        return x + mixmod_ref[2:3, :] * _dot(y, wo_ref[...])

    xs = {r0: residual_in(r0) for r0 in subs}
    hidden = {r0: _norm_mod(xs[r0], g_ref[...], mod_ref[0:1, :], mod_ref[1:2, :]).astype(BF16) for r0 in subs}

    def gate_up(r0, c0):
        h = hidden[r0]
        return _dot(h, win_ref[:, c0:c0 + ck]), _dot(h, win_ref[:, d_ff + c0:d_ff + c0 + ck])

    jobs = [(r0, c0) for r0 in subs for c0 in chunks]
    pending = gate_up(*jobs[0])
    acc = None
    for n, (r0, c0) in enumerate(jobs):
        gate, up = pending
        if n + 1 < len(jobs):
            pending = gate_up(*jobs[n + 1])
        act = (gate * _sigmoid(gate) * up).astype(BF16)
        y = _dot(act, wout_ref[c0:c0 + ck, :])
        acc = y if c0 == 0 else acc + y
        if c0 == chunks[-1]:
            o_ref[r0:r0 + rows, :] = xs[r0] + (0.5 * mod_ref[2:3, :]) * acc


def _ffn_chunk(d_ff):
    for n in (2, 1, 4, 11, 22):
        if d_ff % n == 0 and (d_ff // n) % LANES == 0:
            return d_ff // n
    return d_ff


def _ffn(x2d, mod3, g, w_in, w_out, *, tm, tiles_per_batch, mix=None):
    N, D = x2d.shape
    d_ff = w_out.shape[0]
    tile = pl.BlockSpec((tm, D), lambda i: (i, 0))
    modspec = pl.BlockSpec((None, 3, D), lambda i: (i // tiles_per_batch, 0, 0))
    ins = [x2d, mod3, g.reshape(1, D), w_in, w_out]
    specs = [tile, modspec, _const_spec((1, D)), _const_spec(w_in.shape), _const_spec(w_out.shape)]
    mixer = None
    if mix is not None:
        y, y_gate, mixmod, w_o = mix
        mixer = "plain" if y_gate is None else "gated"
        ins += [y] + ([] if y_gate is None else [y_gate]) + [mixmod, w_o]
        specs += [tile] + ([] if y_gate is None else [tile]) + [modspec, _const_spec(w_o.shape)]
    kern = functools.partial(_ffn_kernel, d_ff=d_ff, ck=_ffn_chunk(d_ff), rows=min(tm, FFN_ROWS), mixer=mixer)
    return pl.pallas_call(
        kern,
        grid=(N // tm,),
        in_specs=specs,
        out_specs=tile,
        out_shape=jax.ShapeDtypeStruct((N, D), F32),
        compiler_params=_cparams("parallel"),
        name="ffn_half" if mixer is None else "ffn_half_" + mixer,
    )(*ins)


def _group_rms_scale(y, gsum, gexp):
    ss = _dot((y * y).astype(BF16), gsum)
    return _group_bcast(lax.rsqrt(ss * (1.0 / HEAD_DIM) + NORM_EPS), gexp)


def _da_qkv_kernel(x_ref, mod_ref, g_ref, w_ref, gsum_ref, gexp_ref, qg_ref, kg_ref, q_ref, k_ref, vt_ref):
    D = x_ref.shape[1]
    h = _norm_mod(x_ref[...], g_ref[...], mod_ref[0:1, :], mod_ref[1:2, :]).astype(BF16)
    q = _dot(h, w_ref[:, 0:D])
    k = _dot(h, w_ref[:, D:2 * D])
    v = _dot(h, w_ref[:, 2 * D:3 * D])
    vt_ref[...] = v.T.astype(BF16)
    q_ref[...] = (q * _group_rms_scale(q, gsum_ref[...], gexp_ref[...]) * qg_ref[...]).astype(BF16)
    k_ref[...] = (k * _group_rms_scale(k, gsum_ref[...], gexp_ref[...]) * kg_ref[...]).astype(BF16)


def _da_qkv(x2d, mod3, g, w_qkv, q_gain, k_gain, *, B, T, tm):
    N, D = x2d.shape
    tpb = T // tm
    gsum, gexp = _group_mats(D)
    tile = pl.BlockSpec((tm, D), lambda i: (i, 0))
    return pl.pallas_call(
        _da_qkv_kernel,
        grid=(N // tm,),
        in_specs=[tile,
                  pl.BlockSpec((None, 3, D), lambda i: (i // tpb, 0, 0)),
                  _const_spec((1, D)), _const_spec(w_qkv.shape), _const_spec(gsum.shape),
                  _const_spec(gexp.shape), _const_spec((1, D)), _const_spec((1, D))],
        out_specs=[tile, tile, pl.BlockSpec((None, D, tm), lambda i: (i // tpb, 0, i % tpb))],
        out_shape=[jax.ShapeDtypeStruct((N, D), BF16), jax.ShapeDtypeStruct((N, D), BF16),
                   jax.ShapeDtypeStruct((B, D, T), BF16)],
        compiler_params=_cparams("parallel"),
        name="da_qkv",
    )(x2d, mod3, g.reshape(1, D), w_qkv, gsum, gexp, q_gain, k_gain)


def _da_attn_kernel(lam_ref, sg_ref, q_ref, k_ref, vt_ref, o_ref, acc_ref, sa_ref, sb_ref, *, tq):
    nq = q_ref.shape[0] // tq
    lane = lax.broadcasted_iota(jnp.int32, (1, LANES), 1)
    lam = lam_ref[...]
    lambda_init = lam[4:5, 0:1]
    lam_full = (jnp.exp(jnp.sum(lam[0:1] * lam[1:2], axis=1, keepdims=True))
                - jnp.exp(jnp.sum(lam[2:3] * lam[3:4], axis=1, keepdims=True)) + lambda_init)
    kpos = lax.broadcasted_iota(jnp.int32, (tq, tq), 0)
    qpos = lax.broadcasted_iota(jnp.int32, (tq, tq), 1)
    causal = kpos <= qpos
    bufs = (sa_ref, sb_ref)
    q_maps = {}

    def q_masked(qi):
        if qi not in q_maps:
            q = q_ref[qi * tq:(qi + 1) * tq, :]
            zero = jnp.zeros_like(q)
            q_maps[qi] = (jnp.where(lane < HEAD_DIM, q, zero), jnp.where(lane >= HEAD_DIM, q, zero))
        return q_maps[qi]

    half = tq // 2

    def scores_into(s_ref, qi, j):
        k0 = j * tq
        for c in range(2):
            qm = q_masked(qi)[c]
            if j < qi:
                s_ref[c] = _dot_nt(k_ref[k0:k0 + tq, :], qm)
            else:
                s_ref[c, 0:half, :] = _dot_nt(k_ref[k0:k0 + half, :], qm)
                s_ref[c, half:tq, half:tq] = _dot_nt(k_ref[k0 + half:k0 + tq, :], qm[half:tq, :])

    def accumulate(s, vt, acc_c, lo, hi, prev):
        m_cur = jnp.max(s, axis=0, keepdims=True)
        if prev is None:
            m_new = m_cur
            p = jnp.exp2(s - m_new)
            l_new = jnp.sum(p, axis=0, keepdims=True)
            acc_c[:, lo:hi] = _dot(vt, p.astype(BF16))
        else:
            m_prev, l_prev = prev
            m_new = jnp.maximum(m_prev, m_cur)
            alpha = jnp.exp2(m_prev - m_new)
            p = jnp.exp2(s - m_new)
            l_new = alpha * l_prev + jnp.sum(p, axis=0, keepdims=True)
            acc_c[:, lo:hi] = alpha * acc_c[:, lo:hi] + _dot(vt, p.astype(BF16))
        return m_new, l_new

    def consume(s_ref, acc, qi, j, carry):
        k0 = j * tq
        out = []
        for c in range(2):
            prev = None if j == 0 else (carry[2 * c], carry[2 * c + 1])
            if j < qi:
                m, l = accumulate(s_ref[c], vt_ref[:, k0:k0 + tq], acc.at[c], 0, tq, prev)
            else:
                s_a = jnp.where(causal[0:half, :], s_ref[c, 0:half, :], -jnp.inf)
                m, l = accumulate(s_a, vt_ref[:, k0:k0 + half], acc.at[c], 0, tq, prev)
                s_b = jnp.where(causal[0:half, 0:half], s_ref[c, half:tq, half:tq], -jnp.inf)
                m_b, l_b = accumulate(s_b, vt_ref[:, k0 + half:k0 + tq], acc.at[c], half, tq,
                                      (m[:, half:], l[:, half:]))
                m = jnp.concatenate([m[:, :half], m_b], axis=1)
                l = jnp.concatenate([l[:, :half], l_b], axis=1)
            out += [m, l]
        return tuple(out)

    def finish(acc, qi, carry):
        ot = acc[0] / carry[1] - lam_full * (acc[1] / carry[3])
        ms = jnp.mean(ot * ot, axis=0, keepdims=True)
        ot = ot * lax.rsqrt(ms + SUBLN_EPS)
        o_ref[qi * tq:(qi + 1) * tq, :] = (ot.T * sg_ref[...] * (1.0 - lambda_init)).astype(BF16)

    tasks = [(qi, j) for qi in range(nq) for j in range(qi + 1)]
    scores_into(bufs[0], *tasks[0])
    carry = None
    for t, (qi, j) in enumerate(tasks):
        if t + 1 < len(tasks):
            scores_into(bufs[(t + 1) % 2], *tasks[t + 1])
        acc = acc_ref.at[qi % 2]
        carry = consume(bufs[t % 2], acc, qi, j, carry)
        if j == qi:
            finish(acc, qi, carry)


def _da_attn(q, k, vt, lambdas, subln_g, *, B, T, D, tq, lambda_init):
    H = D // LANES
    q3, k3 = q.reshape(B, T, D), k.reshape(B, T, D)
    seq = pl.BlockSpec((None, T, LANES), lambda b, h: (b, 0, h))
    out = pl.pallas_call(
        functools.partial(_da_attn_kernel, tq=tq),
        grid=(B, H),
        in_specs=[pl.BlockSpec((5, HEAD_DIM), lambda b, h: (0, 0)),
                  pl.BlockSpec((1, LANES), lambda b, h: (0, 0)),
                  seq, seq,
                  pl.BlockSpec((None, LANES, T), lambda b, h: (b, h, 0))],
        out_specs=seq,
        out_shape=jax.ShapeDtypeStruct((B, T, D), BF16),
        scratch_shapes=[pltpu.VMEM((2, 2, LANES, tq), F32),
                        pltpu.VMEM((2, tq, tq), F32), pltpu.VMEM((2, tq, tq), F32)],
        compiler_params=_cparams("parallel", "parallel"),
        name="da_attn",
    )(jnp.concatenate([lambdas, jnp.full((1, HEAD_DIM), lambda_init, F32)], axis=0),
      subln_g.reshape(1, LANES), q3, k3, vt)
    return out.reshape(B * T, D)


def _sw_qkv_kernel(x_ref, mod_ref, g_ref, w_ref, gsum_ref, gexp_ref, ksum_ref, kexp_ref, qg_ref, kg_ref,
                   q_ref, k_ref, v_ref):
    D = x_ref.shape[1]
    h = _norm_mod(x_ref[...], g_ref[...], mod_ref[0:1, :], mod_ref[1:2, :]).astype(BF16)
    kv = k_ref.shape[1]
    q = _dot(h, w_ref[:, 0:D])
    k = _dot(h, w_ref[:, D:D + kv])
    v_ref[...] = _dot(h, w_ref[:, D + kv:D + 2 * kv]).astype(BF16)
    q_ref[...] = (q * _group_rms_scale(q, gsum_ref[...], gexp_ref[...]) * qg_ref[...]).astype(BF16)
    k_ref[...] = (k * _group_rms_scale(k, ksum_ref[...], kexp_ref[...]) * kg_ref[...]).astype(BF16)


def _sw_qkv(x2d, mod3, g, w_qkv, q_gain, k_gain, *, tm, tiles_per_batch):
    N, D = x2d.shape
    kv = (w_qkv.shape[1] - D) // 2
    gsum, gexp = _group_mats(D)
    ksum, kexp = _group_mats(kv)
    tile = pl.BlockSpec((tm, D), lambda i: (i, 0))
    kvtile = pl.BlockSpec((tm, kv), lambda i: (i, 0))
    return pl.pallas_call(
        _sw_qkv_kernel,
        grid=(N // tm,),
        in_specs=[tile, pl.BlockSpec((None, 3, D), lambda i: (i // tiles_per_batch, 0, 0)),
                  _const_spec((1, D)), _const_spec(w_qkv.shape), _const_spec(gsum.shape),
                  _const_spec(gexp.shape), _const_spec(ksum.shape), _const_spec(kexp.shape),
                  _const_spec((1, D)), _const_spec((1, kv))],
        out_specs=[tile, kvtile, kvtile],
        out_shape=[jax.ShapeDtypeStruct((N, D), BF16), jax.ShapeDtypeStruct((N, kv), BF16),
                   jax.ShapeDtypeStruct((N, kv), BF16)],
        compiler_params=_cparams("parallel"),
        name="sw_qkv",
    )(x2d, mod3, g.reshape(1, D), w_qkv, gsum, gexp, ksum, kexp, q_gain, k_gain)


def _sw_attn_kernel(sink_ref, q_ref, k_ref, v_ref, x_ref, mod_ref, wo_ref, o_ref, att_ref, *, tq, group):
    i = pl.program_id(1)
    D = q_ref.shape[1]
    W = SW_WINDOW
    n_kv = (D // HEAD_DIM) // group
    tiles_per_kv = group // 2
    lane = lax.broadcasted_iota(jnp.int32, (1, LANES), 1)
    first = lane < HEAD_DIM
    jobs = [(sub, kvh) for sub in range(tq // W) for kvh in range(n_kv)]

    def band_start(sub):
        return pl.multiple_of(jnp.maximum(i * tq + (sub - 1) * W, 0), W)

    def scores(sub, kvh):
        k = k_ref[pl.ds(band_start(sub), 2 * W), kvh * LANES:(kvh + 1) * LANES]
        parts = []
        for b in range(tiles_per_kv):
            blk = kvh * tiles_per_kv + b
            q = q_ref[sub * W:(sub + 1) * W, blk * LANES:(blk + 1) * LANES]
            zero = jnp.zeros_like(q)
            parts += [jnp.where(first, q, zero), jnp.where(first, zero, q)]
        return _dot_nt(jnp.concatenate(parts, axis=0), k)

    pending = scores(*jobs[0])
    for n, (sub, kvh) in enumerate(jobs):
        s_all = pending
        if n + 1 < len(jobs):
            pending = scores(*jobs[n + 1])
        qpos = i * tq + sub * W + lax.broadcasted_iota(jnp.int32, (W, 2 * W), 0)
        kpos = band_start(sub) + lax.broadcasted_iota(jnp.int32, (W, 2 * W), 1)
        valid = jnp.logical_and(kpos <= qpos, kpos > qpos - W)
        probs, inv = [], []
        for g in range(group):
            s = jnp.where(valid, s_all[g * W:(g + 1) * W], -jnp.inf)
            snk = sink_ref[kvh * group + g] * math.log2(math.e)
            m = jnp.maximum(jnp.max(s, axis=1, keepdims=True), snk)
            p = jnp.exp2(s - m)
            inv.append(1.0 / (jnp.sum(p, axis=1, keepdims=True) + jnp.exp2(snk - m)))
            probs.append(p.astype(BF16))
        v = v_ref[pl.ds(band_start(sub), 2 * W), kvh * LANES:(kvh + 1) * LANES]
        o_all = _dot(jnp.concatenate(probs, axis=0), v)
        for b in range(tiles_per_kv):
            blk = kvh * tiles_per_kv + b
            o0 = o_all[2 * b * W:(2 * b + 1) * W] * inv[2 * b]
            o1 = o_all[(2 * b + 1) * W:(2 * b + 2) * W] * inv[2 * b + 1]
            att_ref[sub * W:(sub + 1) * W, blk * LANES:(blk + 1) * LANES] = jnp.where(first, o0, o1).astype(BF16)
    o_ref[...] = x_ref[...] + mod_ref[2:3, :] * _dot(att_ref[...], wo_ref[...])


def _sw_attn(q, kd, vd, sinks, x2d, mod3, w_o, *, B, T, tq, group):
    N, D = x2d.shape
    nq = T // tq
    kvd = kd.shape[1]
    tile = pl.BlockSpec((tq, D), lambda b, i: (b * nq + i, 0))
    seq = pl.BlockSpec((None, T, kvd), lambda b, i: (b, 0, 0))
    return pl.pallas_call(
        functools.partial(_sw_attn_kernel, tq=tq, group=group),
        grid=(B, nq),
        in_specs=[pl.BlockSpec(memory_space=pltpu.SMEM), tile, seq, seq, tile,
                  pl.BlockSpec((None, 3, D), lambda b, i: (b, 0, 0)), _const_spec(w_o.shape)],
        out_specs=tile,
        out_shape=jax.ShapeDtypeStruct((N, D), F32),
        scratch_shapes=[pltpu.VMEM((tq, D), BF16)],
        compiler_params=_cparams("parallel", "arbitrary"),
        name="sw_attn",
    )(sinks, q, kd.reshape(B, T, kvd), vd.reshape(B, T, kvd), x2d, mod3, w_o)


def _rw_prep_kernel(x_ref, xp_ref, mod_ref, g_ref, mu_ref, wrkv_ref, w0_ref, w1_ref, w2_ref, a0_ref, a1_ref,
                    a2_ref, g1_ref, g2_ref, kk_ref, ka_ref, gsum_ref, gexp_ref,
                    r_out, lw_out, k_out, v_out, kk_out, b_out, g_out, *, tiles_per_batch):
    D = x_ref.shape[1]
    tm = x_ref.shape[0]
    shift, scale, gn = mod_ref[0:1, :], mod_ref[1:2, :], g_ref[...]
    h = _norm_mod(x_ref[...], gn, shift, scale)
    hp = _norm_mod(xp_ref[...], gn, shift, scale)[7:8, :]
    hp = jnp.where(pl.program_id(0) % tiles_per_batch == 0, jnp.zeros_like(hp), hp)
    row = lax.broadcasted_iota(jnp.int32, (tm, 1), 0)
    xx = jnp.where(row == 0, hp, pltpu.roll(h, 1, axis=0)) - h

    def mixed(n):
        return (h + xx * mu_ref[n:n + 1, :]).astype(BF16)

    d1 = _dot(mixed(3), w1_ref[...])
    a1 = _dot(mixed(4), a1_ref[...])
    g1 = _dot(mixed(5), g1_ref[...])
    r = _dot(mixed(0), wrkv_ref[0])
    k = _dot(mixed(1), wrkv_ref[1])
    v = _dot(mixed(2), wrkv_ref[2])
    d = _dot(jnp.tanh(d1).astype(BF16), w2_ref[...]) + w0_ref[...]
    a = _sigmoid(a0_ref[...] + _dot(a1.astype(BF16), a2_ref[...]))
    g_out[...] = _dot(_sigmoid(g1).astype(BF16), g2_ref[...]).astype(BF16)
    z = -d
    softplus = jnp.maximum(z, 0.0) + jnp.log(1.0 + jnp.exp(-jnp.abs(z)))
    lw_out[...] = -jnp.exp(-softplus - 0.5)
    kk = k * kk_ref[...]
    ss = _dot((kk * kk).astype(BF16), gsum_ref[...])
    kk = kk * _group_bcast(1.0 / jnp.maximum(jnp.sqrt(ss), 1e-12), gexp_ref[...])
    r_out[...] = r.astype(BF16)
    k_out[...] = (k * (1.0 + (a - 1.0) * ka_ref[...])).astype(BF16)
    v_out[...] = v.astype(BF16)
    kk_out[...] = kk.astype(BF16)
    b_out[...] = (kk * a).astype(BF16)


def _pad_to(w, axis, n):
    pad = [(0, 0)] * w.ndim
    pad[axis] = (0, n - w.shape[axis])
    return jnp.pad(w, pad)


def _round_up(n, m):
    return -(-n // m) * m


def _rw_prep(x2d, mod3, g, p, *, tm, tiles_per_batch):
    N, D = x2d.shape
    gsum, gexp = _group_mats(D)
    row = lambda t: t.reshape(1, D)
    lora = lambda w1, w2: (_pad_to(w1, 1, _round_up(w1.shape[1], LANES)).astype(BF16),
                           _pad_to(w2, 0, _round_up(w2.shape[0], LANES)).astype(BF16))
    w1, w2 = lora(p["decay_w1"], p["decay_w2"])
    a1, a2 = lora(p["iclr_a1"], p["iclr_a2"])
    g1, g2 = lora(p["gate_g1"], p["gate_g2"])
    consts = [row(g), p["mu"], p["w_rkv"].astype(BF16), row(p["decay_w0"]), w1, w2, row(p["iclr_a0"]), a1, a2,
              g1, g2, row(p["k_k"]), row(p["k_a"]), gsum, gexp]
    tile = pl.BlockSpec((tm, D), lambda i: (i, 0))
    outs = [jax.ShapeDtypeStruct((N, D), BF16)] * 7
    outs[1] = jax.ShapeDtypeStruct((N, D), F32)
    return pl.pallas_call(
        functools.partial(_rw_prep_kernel, tiles_per_batch=tiles_per_batch),
        grid=(N // tm,),
        in_specs=[tile,
                  pl.BlockSpec((8, D), lambda i: (jnp.maximum(i * (tm // 8) - 1, 0), 0)),
                  pl.BlockSpec((None, 3, D), lambda i: (i // tiles_per_batch, 0, 0))]
                 + [_const_spec(c.shape) for c in consts],
        out_specs=[tile] * 7,
        out_shape=outs,
        compiler_params=_cparams("parallel"),
        name="rw_prep",
    )(x2d, x2d, mod3, *consts)


def _stack_heads(x, m0):
    z = jnp.zeros_like(x)
    return jnp.concatenate([jnp.where(m0, x, z), jnp.where(m0, z, x)], axis=0)


def _rw_scan_kernel(r_ref, lw_ref, k_ref, v_ref, kk_ref, b_ref, rk_ref, lng_ref, lnb_ref, o_ref, st_ref, os_ref,
                    *, tt, group):
    C = RW_CHUNK
    C2 = 2 * C

    @pl.when(pl.program_id(2) == 0)
    def _():
        st_ref[...] = jnp.zeros_like(st_ref)

    lane = lax.broadcasted_iota(jnp.int32, (1, LANES), 1)
    m0 = lane < HEAD_DIM
    ti = lax.broadcasted_iota(jnp.int32, (C, C), 0)
    tj = lax.broadcasted_iota(jnp.int32, (C, C), 1)
    tri = jnp.where(ti >= tj, 1.0, 0.0).astype(BF16)
    ri = lax.broadcasted_iota(jnp.int32, (2 * C2, 2 * C2), 0)
    ci = lax.broadcasted_iota(jnp.int32, (2 * C2, 2 * C2), 1)
    rt, ct = ri % C, ci % C
    causal = rt - ct + (ri >= C2).astype(jnp.int32) > 0
    eye = (lax.broadcasted_iota(jnp.int32, (C2, C2), 0) == lax.broadcasted_iota(jnp.int32, (C2, C2), 1))
    eye_f = eye.astype(F32)
    si = lax.broadcasted_iota(jnp.int32, (C2, C2), 0)
    sj = lax.broadcasted_iota(jnp.int32, (C2, C2), 1)
    merge_masks = []
    s = 1
    while s < C:
        merge_masks.append(jnp.logical_and(si // (2 * s) == sj // (2 * s),
                                           jnp.logical_and(si % (2 * s) >= s, sj % (2 * s) < s)))
        s *= 2
    zeros_sq = jnp.zeros((C2, C2), BF16)

    def chunk_terms(chunks):
        n = range(len(chunks))
        a_t, r_t, v_s, aa, lhs_top, p_col = [], [], [], [], [], []
        for i, c in enumerate(chunks):
            rows = slice(c * C, (c + 1) * C)
            lw = lw_ref[rows, :]
            r = r_ref[rows, :].astype(F32)
            k = k_ref[rows, :].astype(F32)
            kk = kk_ref[rows, :].astype(F32)
            b = b_ref[rows, :].astype(F32)
            lw_hi = lw.astype(BF16)
            lw_lo = (lw - lw_hi.astype(F32)).astype(BF16)
            cum = _dot(tri, lw_hi) + _dot(tri, lw_lo)
            tot = cum[C - 1:C, :]
            e_neg = jnp.exp(-cum)
            r_t.append(_stack_heads(r * jnp.exp(cum), m0).astype(BF16))
            a_t.append(_stack_heads(-kk * jnp.exp(cum - lw), m0).astype(BF16))
            b_t = _stack_heads(b * e_neg, m0).astype(BF16)
            k_t = _stack_heads(k * e_neg, m0).astype(BF16)
            e_end = jnp.exp(tot - cum)
            b_h = _stack_heads(b * e_end, m0)
            k_h = _stack_heads(k * e_end, m0)
            lhs_top.append(jnp.concatenate([b_h.T, k_h.T], axis=1).astype(BF16))
            v_s.append(_stack_heads(v_ref[rows, :], m0))
            p_col.append(jnp.sum(jnp.where(eye, jnp.exp(tot), 0.0), axis=1, keepdims=True))
            full = _dot_nt(jnp.concatenate([a_t[i], r_t[i]], axis=0), jnp.concatenate([b_t, k_t], axis=0))
            aa.append(jnp.where(causal, full, 0.0))
        yield

        a_ab = [aa[i][:C2, :C2] for i in n]
        t_inv = [eye_f + jnp.where(merge_masks[0], a_ab[i], 0.0) for i in n]
        for mask in merge_masks[1:]:
            tb = [t_inv[i].astype(BF16) for i in n]
            half = [_dot(tb[i], jnp.where(mask, a_ab[i], 0.0).astype(BF16)).astype(BF16) for i in n]
            yield
            t_inv = [t_inv[i] + _dot(half[i], tb[i]) for i in n]
            yield
        akv = [_dot(aa[i][:C2, C2:].astype(BF16), v_s[i]).astype(BF16) for i in n]
        yield
        au = [_dot(t_inv[i].astype(BF16), jnp.concatenate([a_t[i], akv[i]], axis=1)).astype(BF16)
              for i in n]
        yield
        big = []
        for i in n:
            rhs = jnp.concatenate([au[i], jnp.concatenate([zeros_sq, v_s[i]], axis=1)], axis=0)
            lhs = jnp.concatenate([lhs_top[i], aa[i][C2:, :].astype(BF16)], axis=0)
            big.append(_dot(lhs, rhs))
        qg = [jnp.concatenate([big[i][C2:, :C2] + r_t[i].astype(F32), big[i][:C2, :C2]], axis=0).astype(BF16)
              for i in n]
        return qg, big, p_col

    def state_steps(st, chunks, terms):
        qg, big, p_col = terms
        for i, c in enumerate(chunks):
            res = _dot(qg[i], st[0].astype(BF16))
            o_st = res[:C2] + big[i][C2:, C2:]
            os_ref[c * C:(c + 1) * C, :] = o_st[:C] + o_st[C:]
            st[0] = p_col[i] * st[0] + res[C2:] + big[i][:C2, C2:]
            yield

    st = [st_ref[...]]
    n_chunks = tt // C
    pending = iter(())
    for g0 in range(0, n_chunks, group):
        chunks = list(range(g0, min(g0 + group, n_chunks)))
        gen = chunk_terms(chunks)
        while True:
            try:
                next(gen)
            except StopIteration as done:
                terms = done.value
                break
            next(pending, None)
        for _ in pending:
            pass
        pending = state_steps(st, chunks, terms)
    for _ in pending:
        pass
    st_ref[...] = st[0]

    o = os_ref[...]
    blk = ((lax.broadcasted_iota(jnp.int32, (LANES, LANES), 0) < HEAD_DIM)
           == (lax.broadcasted_iota(jnp.int32, (LANES, LANES), 1) < HEAD_DIM)).astype(BF16)

    def head_sum(x):
        hi = x.astype(BF16)
        lo = (x - hi.astype(F32)).astype(BF16)
        return _dot(hi, blk) + _dot(lo, blk)

    mean = head_sum(o) * (1.0 / HEAD_DIM)
    cen = o - mean
    var = head_sum(cen * cen) * (1.0 / HEAD_DIM)
    y = cen * lax.rsqrt(var + RW_LNX_EPS) * lng_ref[...] + lnb_ref[...]
    r = r_ref[...].astype(F32)
    k = k_ref[...].astype(F32)
    bonus = head_sum(r * k * rk_ref[...]) * v_ref[...].astype(F32)
    o_ref[...] = (y + bonus).astype(BF16)


def _rw_scan(r, lw, k, v, kk, b, r_k, lnx_g, lnx_b, *, B, T, D, tt):
    hp = D // LANES
    blk = pl.BlockSpec((None, tt, LANES), lambda bi, h, t: (bi, t, h))
    vec = pl.BlockSpec((1, LANES), lambda bi, h, t: (0, h))
    r3 = lambda t: t.reshape(B, T, D)
    out = pl.pallas_call(
        functools.partial(_rw_scan_kernel, tt=tt, group=RW_GROUP),
        grid=(B, hp, T // tt),
        in_specs=[blk] * 6 + [vec] * 3,
        out_specs=blk,
        out_shape=jax.ShapeDtypeStruct((B, T, D), BF16),
        scratch_shapes=[pltpu.VMEM((LANES, LANES), F32), pltpu.VMEM((tt, LANES), F32)],
        compiler_params=_cparams("parallel", "parallel", "arbitrary"),
        name="rw_scan",
    )(r3(r), r3(lw), r3(k), r3(v), r3(kk), r3(b), r_k.reshape(1, D), lnx_g.reshape(1, D), lnx_b.reshape(1, D))
    return out.reshape(B * T, D)


def _diff_lambda_init(layer):
    return 0.8 - 0.6 * math.exp(-0.3 * layer)


def _pick_tile(n, pref):
    t = min(n, pref)
    while n % t:
        t //= 2
    return t


def kernel(x, c, norm_g, ada_w, ada_b, ffn_w_in, ffn_w_out, da_w_qkv, da_w_o, da_q_norm_g, da_k_norm_g, da_lambda, da_subln_g, rw_mu, rw_w_rkv, rw_w_o, rw_decay_w0, rw_decay_w1, rw_decay_w2, rw_iclr_a0, rw_iclr_a1, rw_iclr_a2, rw_gate_g1, rw_gate_g2, rw_k_k, rw_k_a, rw_r_k, rw_lnx_g, rw_lnx_b, sw_w_qkv, sw_w_o, sw_q_norm_g, sw_k_norm_g, sw_sinks):
    B, T, D = x.shape
    L = norm_g.shape[0]
    tm = _pick_tile(T, 512)
    tpb = T // tm
    kw = dict(tm=tm, tiles_per_batch=tpb)
    tf = _pick_tile(T, 2 * FFN_ROWS)
    fkw = dict(tm=tf, tiles_per_batch=T // tf)
    scale = HEAD_DIM ** -0.5

    mod = _ada(c, ada_w, ada_b).reshape(L, B, 3, 3, D)
    x2d = x.reshape(B * T, D)
    for i in range(L):
        kind, j = i % 3, i // 3
        x2d = _ffn(x2d, mod[i, :, 0], norm_g[i, 0], ffn_w_in[i, 0].astype(BF16), ffn_w_out[i, 0].astype(BF16), **fkw)
        m2 = mod[i, :, 1]
        if kind == 0:
            qg = jnp.tile(da_q_norm_g[j] * (scale * math.log2(math.e)), D // HEAD_DIM).reshape(1, D)
            kg = jnp.tile(da_k_norm_g[j], D // HEAD_DIM).reshape(1, D)
            q, k, vt = _da_qkv(x2d, m2, norm_g[i, 1], da_w_qkv[j].astype(BF16), qg, kg, B=B, T=T, tm=tm)
            o = _da_attn(q, k, vt, da_lambda[j], da_subln_g[j], B=B, T=T, D=D, tq=_pick_tile(T, 512),
                         lambda_init=_diff_lambda_init(i))
            mix = (o, None, m2, da_w_o[j].astype(BF16))
        elif kind == 1:
            p = dict(mu=rw_mu[j], w_rkv=rw_w_rkv[j], decay_w0=rw_decay_w0[j], decay_w1=rw_decay_w1[j],
                     decay_w2=rw_decay_w2[j], iclr_a0=rw_iclr_a0[j], iclr_a1=rw_iclr_a1[j], iclr_a2=rw_iclr_a2[j],
                     gate_g1=rw_gate_g1[j], gate_g2=rw_gate_g2[j], k_k=rw_k_k[j], k_a=rw_k_a[j])
            r, lw, k, v, kk, b, g = _rw_prep(x2d, m2, norm_g[i, 1], p, **kw)
            y = _rw_scan(r, lw, k, v, kk, b, rw_r_k[j], rw_lnx_g[j], rw_lnx_b[j], B=B, T=T, D=D,
                         tt=_pick_tile(T, 2048))
            mix = (y, g, m2, rw_w_o[j].astype(BF16))
        else:
            kv = (sw_w_qkv.shape[-1] - D) // 2
            n_kv = kv // HEAD_DIM
            qg = jnp.tile(sw_q_norm_g[j] * (scale * math.log2(math.e)), D // HEAD_DIM).reshape(1, D)
            dup = lambda w: jnp.repeat(w.reshape(D, n_kv, 1, HEAD_DIM), 2, axis=2).reshape(D, 2 * kv)
            w = sw_w_qkv[j]
            w = jnp.concatenate([w[:, :D], dup(w[:, D:D + kv]), dup(w[:, D + kv:])], axis=1).astype(BF16)
            kg = jnp.tile(sw_k_norm_g[j], 2 * n_kv).reshape(1, 2 * kv)
            q, k, v = _sw_qkv(x2d, m2, norm_g[i, 1], w, qg, kg, **kw)
            x2d = _sw_attn(q, k, v, sw_sinks[j], x2d, m2, sw_w_o[j].astype(BF16), B=B, T=T,
                           tq=_pick_tile(T, 512), group=(D // HEAD_DIM) // n_kv)
            mix = None
        x2d = _ffn(x2d, mod[i, :, 2], norm_g[i, 2], ffn_w_in[i, 1].astype(BF16), ffn_w_out[i, 1].astype(BF16),
                   mix=mix, **fkw)
    return x2d.reshape(B, T, D)
```

```python
import functools
import math

import jax
import jax.numpy as jnp
from jax import lax
from jax.experimental import pallas as pl
from jax.experimental.pallas import tpu as pltpu

F32 = jnp.float32
BF16 = jnp.bfloat16
HIGHEST = lax.Precision.HIGHEST

HEAD_DIM = 64
LANES = 128
NORM_EPS = 1e-6
SUBLN_EPS = 1e-5
RW_LNX_EPS = 64e-5
SW_WINDOW = 128
N_MOD = 9
FFN_ROWS = 512
RW_GROUP = 8
RW_CHUNK = 64
VMEM_LIMIT = 56 * 1024 * 1024


def _cparams(*sem):
    return pltpu.CompilerParams(dimension_semantics=sem, vmem_limit_bytes=VMEM_LIMIT)


def _sigmoid(x):
    return 1.0 / (1.0 + jnp.exp(-x))


def _dot(a, b):
    return jnp.dot(a, b, preferred_element_type=F32)


def _dot_nt(a, b):
    return lax.dot_general(a, b, (((1,), (1,)), ((), ())), preferred_element_type=F32)


def _norm_mod(x, g, shift, scale):
    ms = jnp.mean(x * x, axis=-1, keepdims=True)
    return (x * lax.rsqrt(ms + NORM_EPS) * g) * (1.0 + scale) + shift


def _const_spec(shape):
    nd = len(shape)
    return pl.BlockSpec(shape, lambda *_: (0,) * nd, pipeline_mode=pl.Buffered(1))


def _group_mats(d):
    grp = jnp.arange(d) // HEAD_DIM
    gsum = (grp[:, None] == jnp.arange(LANES)[None, :]).astype(BF16)
    return gsum, jnp.concatenate([gsum.T, gsum.T], axis=0)


def _group_bcast(v, gexp):
    hi = v.astype(BF16)
    lo = (v - hi.astype(F32)).astype(BF16)
    return _dot(jnp.concatenate([hi, lo], axis=1), gexp)


def _ada_kernel(c_ref, w_ref, b_ref, o_ref):
    c = c_ref[...]
    cond = c * _sigmoid(c)
    o_ref[...] = jnp.dot(cond, w_ref[...], precision=HIGHEST, preferred_element_type=F32) + b_ref[...]


def _ada(c, ada_w, ada_b):
    L, D, ND = ada_w.shape
    B = c.shape[0]
    return pl.pallas_call(
        _ada_kernel,
        grid=(L, ND // D),
        in_specs=[pl.BlockSpec((B, D), lambda l, j: (0, 0)),
                  pl.BlockSpec((None, D, D), lambda l, j: (l, 0, j)),
                  pl.BlockSpec((None, 1, D), lambda l, j: (l, 0, j))],
        out_specs=pl.BlockSpec((None, B, D), lambda l, j: (l, 0, j)),
        out_shape=jax.ShapeDtypeStruct((L, B, ND), F32),
        compiler_params=_cparams("parallel", "parallel"),
        name="ada_mod",
    )(c, ada_w, ada_b.reshape(L, 1, ND))


def _ffn_kernel(*refs, d_ff, ck, rows, mixer):
    if mixer is None:
        x_ref, mod_ref, g_ref, win_ref, wout_ref, o_ref = refs
    elif mixer == "plain":
        x_ref, mod_ref, g_ref, win_ref, wout_ref, y_ref, mixmod_ref, wo_ref, o_ref = refs
    else:
        x_ref, mod_ref, g_ref, win_ref, wout_ref, y_ref, ygate_ref, mixmod_ref, wo_ref, o_ref = refs
    tm = x_ref.shape[0]
    subs = list(range(0, tm, rows))
    chunks = list(range(0, d_ff, ck))

    def residual_in(r0):
        x = x_ref[r0:r0 + rows, :]
        if mixer is None:
            return x
        y = y_ref[r0:r0 + rows, :]
        if mixer == "gated":
            y = (y.astype(F32) * ygate_ref[r0:r0 + rows, :].astype(F32)).astype(BF16)
        return x + mixmod_ref[2:3, :] * _dot(y, wo_ref[...])

    xs = {r0: residual_in(r0) for r0 in subs}
    hidden = {r0: _norm_mod(xs[r0], g_ref[...], mod_ref[0:1, :], mod_ref[1:2, :]).astype(BF16) for r0 in subs}

    def gate_up(r0, c0):
        h = hidden[r0]
        return _dot(h, win_ref[:, c0:c0 + ck]), _dot(h, win_ref[:, d_ff + c0:d_ff + c0 + ck])

    jobs = [(r0, c0) for r0 in subs for c0 in chunks]
    pending = gate_up(*jobs[0])
    acc = None
    for n, (r0, c0) in enumerate(jobs):
        gate, up = pending
        if n + 1 < len(jobs):
            pending = gate_up(*jobs[n + 1])
        act = (gate * _sigmoid(gate) * up).astype(BF16)
        y = _dot(act, wout_ref[c0:c0 + ck, :])
        acc = y if c0 == 0 else acc + y
        if c0 == chunks[-1]:
            o_ref[r0:r0 + rows, :] = xs[r0] + (0.5 * mod_ref[2:3, :]) * acc


def _ffn_chunk(d_ff):
    for n in (2, 1, 4, 11, 22):
        if d_ff % n == 0 and (d_ff // n) % LANES == 0:
            return d_ff // n
    return d_ff


def _ffn(x2d, mod3, g, w_in, w_out, *, tm, tiles_per_batch, mix=None):
    N, D = x2d.shape
    d_ff = w_out.shape[0]
    tile = pl.BlockSpec((tm, D), lambda i: (i, 0))
    modspec = pl.BlockSpec((None, 3, D), lambda i: (i // tiles_per_batch, 0, 0))
    ins = [x2d, mod3, g.reshape(1, D), w_in, w_out]
    specs = [tile, modspec, _const_spec((1, D)), _const_spec(w_in.shape), _const_spec(w_out.shape)]
    mixer = None
    if mix is not None:
        y, y_gate, mixmod, w_o = mix
        mixer = "plain" if y_gate is None else "gated"
        ins += [y] + ([] if y_gate is None else [y_gate]) + [mixmod, w_o]
        specs += [tile] + ([] if y_gate is None else [tile]) + [modspec, _const_spec(w_o.shape)]
    kern = functools.partial(_ffn_kernel, d_ff=d_ff, ck=_ffn_chunk(d_ff), rows=min(tm, FFN_ROWS), mixer=mixer)
    return pl.pallas_call(
        kern,
        grid=(N // tm,),
        in_specs=specs,
        out_specs=tile,
        out_shape=jax.ShapeDtypeStruct((N, D), F32),
        compiler_params=_cparams("parallel"),
        name="ffn_half" if mixer is None else "ffn_half_" + mixer,
    )(*ins)


def _group_rms_scale(y, gsum, gexp):
    ss = _dot((y * y).astype(BF16), gsum)
    return _group_bcast(lax.rsqrt(ss * (1.0 / HEAD_DIM) + NORM_EPS), gexp)


def _da_qkv_kernel(x_ref, mod_ref, g_ref, w_ref, gsum_ref, gexp_ref, qg_ref, kg_ref, q_ref, k_ref, vt_ref):
    D = x_ref.shape[1]
    h = _norm_mod(x_ref[...], g_ref[...], mod_ref[0:1, :], mod_ref[1:2, :]).astype(BF16)
    q = _dot(h, w_ref[:, 0:D])
    k = _dot(h, w_ref[:, D:2 * D])
    v = _dot(h, w_ref[:, 2 * D:3 * D])
    vt_ref[...] = v.T.astype(BF16)
    q_ref[...] = (q * _group_rms_scale(q, gsum_ref[...], gexp_ref[...]) * qg_ref[...]).astype(BF16)
    k_ref[...] = (k * _group_rms_scale(k, gsum_ref[...], gexp_ref[...]) * kg_ref[...]).astype(BF16)


def _da_qkv(x2d, mod3, g, w_qkv, q_gain, k_gain, *, B, T, tm):
    N, D = x2d.shape
    tpb = T // tm
    gsum, gexp = _group_mats(D)
    tile = pl.BlockSpec((tm, D), lambda i: (i, 0))
    return pl.pallas_call(
        _da_qkv_kernel,
        grid=(N // tm,),
        in_specs=[tile,
                  pl.BlockSpec((None, 3, D), lambda i: (i // tpb, 0, 0)),
                  _const_spec((1, D)), _const_spec(w_qkv.shape), _const_spec(gsum.shape),
                  _const_spec(gexp.shape), _const_spec((1, D)), _const_spec((1, D))],
        out_specs=[tile, tile, pl.BlockSpec((None, D, tm), lambda i: (i // tpb, 0, i % tpb))],
        out_shape=[jax.ShapeDtypeStruct((N, D), BF16), jax.ShapeDtypeStruct((N, D), BF16),
                   jax.ShapeDtypeStruct((B, D, T), BF16)],
        compiler_params=_cparams("parallel"),
        name="da_qkv",
    )(x2d, mod3, g.reshape(1, D), w_qkv, gsum, gexp, q_gain, k_gain)


def _da_attn_kernel(lam_ref, sg_ref, q_ref, k_ref, vt_ref, o_ref, acc_ref, sa_ref, sb_ref, *, tq):
    nq = q_ref.shape[0] // tq
    lane = lax.broadcasted_iota(jnp.int32, (1, LANES), 1)
    lam = lam_ref[...]
    lambda_init = lam[4:5, 0:1]
    lam_full = (jnp.exp(jnp.sum(lam[0:1] * lam[1:2], axis=1, keepdims=True))
                - jnp.exp(jnp.sum(lam[2:3] * lam[3:4], axis=1, keepdims=True)) + lambda_init)
    kpos = lax.broadcasted_iota(jnp.int32, (tq, tq), 0)
    qpos = lax.broadcasted_iota(jnp.int32, (tq, tq), 1)
    causal = kpos <= qpos
    bufs = (sa_ref, sb_ref)

    def scores_into(s_ref, qi, j):
        k = k_ref[j * tq:(j + 1) * tq, :]
        zero = jnp.zeros_like(k)
        k_maps = jnp.concatenate([jnp.where(lane < HEAD_DIM, k, zero), jnp.where(lane >= HEAD_DIM, k, zero)], axis=0)
        s = _dot_nt(k_maps, q_ref[qi * tq:(qi + 1) * tq, :])
        s_ref[0] = s[:tq]
        s_ref[1] = s[tq:]

    def consume(s_ref, acc, qi, j, carry):
        vt = vt_ref[:, j * tq:(j + 1) * tq]
        out = []
        for c in range(2):
            s = s_ref[c]
            if j == qi:
                s = jnp.where(causal, s, -jnp.inf)
            m_cur = jnp.max(s, axis=0, keepdims=True)
            if j == 0:
                m_new = m_cur
                p = jnp.exp2(s - m_new)
                l_new = jnp.sum(p, axis=0, keepdims=True)
                acc[c] = _dot(vt, p.astype(BF16))
            else:
                m_prev, l_prev = carry[2 * c], carry[2 * c + 1]
                m_new = jnp.maximum(m_prev, m_cur)
                alpha = jnp.exp2(m_prev - m_new)
                p = jnp.exp2(s - m_new)
                l_new = alpha * l_prev + jnp.sum(p, axis=0, keepdims=True)
                acc[c] = alpha * acc[c] + _dot(vt, p.astype(BF16))
            out += [m_new, l_new]
        return tuple(out)

    def finish(acc, qi, carry):
        ot = acc[0] / carry[1] - lam_full * (acc[1] / carry[3])
        ms = jnp.mean(ot * ot, axis=0, keepdims=True)
        ot = ot * lax.rsqrt(ms + SUBLN_EPS)
        o_ref[qi * tq:(qi + 1) * tq, :] = (ot.T * sg_ref[...] * (1.0 - lambda_init)).astype(BF16)

    tasks = [(qi, j) for qi in range(nq) for j in range(qi + 1)]
    scores_into(bufs[0], *tasks[0])
    carry = None
    for t, (qi, j) in enumerate(tasks):
        if t + 1 < len(tasks):
            scores_into(bufs[(t + 1) % 2], *tasks[t + 1])
        acc = acc_ref.at[qi % 2]
        carry = consume(bufs[t % 2], acc, qi, j, carry)
        if j == qi:
            finish(acc, qi, carry)


def _da_attn(q, k, vt, lambdas, subln_g, *, B, T, D, tq, lambda_init):
    H = D // LANES
    q3, k3 = q.reshape(B, T, D), k.reshape(B, T, D)
    seq = pl.BlockSpec((None, T, LANES), lambda b, h: (b, 0, h))
    out = pl.pallas_call(
        functools.partial(_da_attn_kernel, tq=tq),
        grid=(B, H),
        in_specs=[pl.BlockSpec((5, HEAD_DIM), lambda b, h: (0, 0)),
                  pl.BlockSpec((1, LANES), lambda b, h: (0, 0)),
                  seq, seq,
                  pl.BlockSpec((None, LANES, T), lambda b, h: (b, h, 0))],
        out_specs=seq,
        out_shape=jax.ShapeDtypeStruct((B, T, D), BF16),
        scratch_shapes=[pltpu.VMEM((2, 2, LANES, tq), F32),
                        pltpu.VMEM((2, tq, tq), F32), pltpu.VMEM((2, tq, tq), F32)],
        compiler_params=_cparams("parallel", "parallel"),
        name="da_attn",
    )(jnp.concatenate([lambdas, jnp.full((1, HEAD_DIM), lambda_init, F32)], axis=0),
      subln_g.reshape(1, LANES), q3, k3, vt)
    return out.reshape(B * T, D)


def _sw_qkv_kernel(x_ref, mod_ref, g_ref, w_ref, gsum_ref, gexp_ref, ksum_ref, kexp_ref, qg_ref, kg_ref,
                   q_ref, k_ref, v_ref):
    D = x_ref.shape[1]
    h = _norm_mod(x_ref[...], g_ref[...], mod_ref[0:1, :], mod_ref[1:2, :]).astype(BF16)
    kv = k_ref.shape[1]
    q = _dot(h, w_ref[:, 0:D])
    k = _dot(h, w_ref[:, D:D + kv])
    v_ref[...] = _dot(h, w_ref[:, D + kv:D + 2 * kv]).astype(BF16)
    q_ref[...] = (q * _group_rms_scale(q, gsum_ref[...], gexp_ref[...]) * qg_ref[...]).astype(BF16)
    k_ref[...] = (k * _group_rms_scale(k, ksum_ref[...], kexp_ref[...]) * kg_ref[...]).astype(BF16)


def _sw_qkv(x2d, mod3, g, w_qkv, q_gain, k_gain, *, tm, tiles_per_batch):
    N, D = x2d.shape
    kv = (w_qkv.shape[1] - D) // 2
    gsum, gexp = _group_mats(D)
    ksum, kexp = _group_mats(kv)
    tile = pl.BlockSpec((tm, D), lambda i: (i, 0))
    kvtile = pl.BlockSpec((tm, kv), lambda i: (i, 0))
    return pl.pallas_call(
        _sw_qkv_kernel,
        grid=(N // tm,),
        in_specs=[tile, pl.BlockSpec((None, 3, D), lambda i: (i // tiles_per_batch, 0, 0)),
                  _const_spec((1, D)), _const_spec(w_qkv.shape), _const_spec(gsum.shape),
                  _const_spec(gexp.shape), _const_spec(ksum.shape), _const_spec(kexp.shape),
                  _const_spec((1, D)), _const_spec((1, kv))],
        out_specs=[tile, kvtile, kvtile],
        out_shape=[jax.ShapeDtypeStruct((N, D), BF16), jax.ShapeDtypeStruct((N, kv), BF16),
                   jax.ShapeDtypeStruct((N, kv), BF16)],
        compiler_params=_cparams("parallel"),
        name="sw_qkv",
    )(x2d, mod3, g.reshape(1, D), w_qkv, gsum, gexp, ksum, kexp, q_gain, k_gain)


def _sw_attn_kernel(sink_ref, q_ref, k_ref, v_ref, x_ref, mod_ref, wo_ref, o_ref, att_ref, *, tq, group):
    i = pl.program_id(1)
    D = q_ref.shape[1]
    W = SW_WINDOW
    n_kv = (D // HEAD_DIM) // group
    tiles_per_kv = group // 2
    lane = lax.broadcasted_iota(jnp.int32, (1, LANES), 1)
    first = lane < HEAD_DIM
    jobs = [(sub, kvh) for sub in range(tq // W) for kvh in range(n_kv)]

    def band_start(sub):
        return pl.multiple_of(jnp.maximum(i * tq + (sub - 1) * W, 0), W)

    def scores(sub, kvh):
        k = k_ref[pl.ds(band_start(sub), 2 * W), kvh * LANES:(kvh + 1) * LANES]
        parts = []
        for b in range(tiles_per_kv):
            blk = kvh * tiles_per_kv + b
            q = q_ref[sub * W:(sub + 1) * W, blk * LANES:(blk + 1) * LANES]
            zero = jnp.zeros_like(q)
            parts += [jnp.where(first, q, zero), jnp.where(first, zero, q)]
        return _dot_nt(jnp.concatenate(parts, axis=0), k)

    pending = scores(*jobs[0])
    for n, (sub, kvh) in enumerate(jobs):
        s_all = pending
        if n + 1 < len(jobs):
            pending = scores(*jobs[n + 1])
        qpos = i * tq + sub * W + lax.broadcasted_iota(jnp.int32, (W, 2 * W), 0)
        kpos = band_start(sub) + lax.broadcasted_iota(jnp.int32, (W, 2 * W), 1)
        valid = jnp.logical_and(kpos <= qpos, kpos > qpos - W)
        probs, inv = [], []
        for g in range(group):
            s = jnp.where(valid, s_all[g * W:(g + 1) * W], -jnp.inf)
            snk = sink_ref[kvh * group + g] * math.log2(math.e)
            m = jnp.maximum(jnp.max(s, axis=1, keepdims=True), snk)
            p = jnp.exp2(s - m)
            inv.append(1.0 / (jnp.sum(p, axis=1, keepdims=True) + jnp.exp2(snk - m)))
            probs.append(p.astype(BF16))
        v = v_ref[pl.ds(band_start(sub), 2 * W), kvh * LANES:(kvh + 1) * LANES]
        o_all = _dot(jnp.concatenate(probs, axis=0), v)
        for b in range(tiles_per_kv):
            blk = kvh * tiles_per_kv + b
            o0 = o_all[2 * b * W:(2 * b + 1) * W] * inv[2 * b]
            o1 = o_all[(2 * b + 1) * W:(2 * b + 2) * W] * inv[2 * b + 1]
            att_ref[sub * W:(sub + 1) * W, blk * LANES:(blk + 1) * LANES] = jnp.where(first, o0, o1).astype(BF16)
    o_ref[...] = x_ref[...] + mod_ref[2:3, :] * _dot(att_ref[...], wo_ref[...])


def _sw_attn(q, kd, vd, sinks, x2d, mod3, w_o, *, B, T, tq, group):
    N, D = x2d.shape
    nq = T // tq
    kvd = kd.shape[1]
    tile = pl.BlockSpec((tq, D), lambda b, i: (b * nq + i, 0))
    seq = pl.BlockSpec((None, T, kvd), lambda b, i: (b, 0, 0))
    return pl.pallas_call(
        functools.partial(_sw_attn_kernel, tq=tq, group=group),
        grid=(B, nq),
        in_specs=[pl.BlockSpec(memory_space=pltpu.SMEM), tile, seq, seq, tile,
                  pl.BlockSpec((None, 3, D), lambda b, i: (b, 0, 0)), _const_spec(w_o.shape)],
        out_specs=tile,
        out_shape=jax.ShapeDtypeStruct((N, D), F32),
        scratch_shapes=[pltpu.VMEM((tq, D), BF16)],
        compiler_params=_cparams("parallel", "arbitrary"),
        name="sw_attn",
    )(sinks, q, kd.reshape(B, T, kvd), vd.reshape(B, T, kvd), x2d, mod3, w_o)


def _rw_prep_kernel(x_ref, xp_ref, mod_ref, g_ref, mu_ref, wrkv_ref, w0_ref, w1_ref, w2_ref, a0_ref, a1_ref,
                    a2_ref, g1_ref, g2_ref, kk_ref, ka_ref, gsum_ref, gexp_ref,
                    r_out, lw_out, k_out, v_out, kk_out, b_out, g_out, *, tiles_per_batch):
    D = x_ref.shape[1]
    tm = x_ref.shape[0]
    shift, scale, gn = mod_ref[0:1, :], mod_ref[1:2, :], g_ref[...]
    h = _norm_mod(x_ref[...], gn, shift, scale)
    hp = _norm_mod(xp_ref[...], gn, shift, scale)[7:8, :]
    hp = jnp.where(pl.program_id(0) % tiles_per_batch == 0, jnp.zeros_like(hp), hp)
    row = lax.broadcasted_iota(jnp.int32, (tm, 1), 0)
    xx = jnp.where(row == 0, hp, pltpu.roll(h, 1, axis=0)) - h

    def mixed(n):
        return (h + xx * mu_ref[n:n + 1, :]).astype(BF16)

    d1 = _dot(mixed(3), w1_ref[...])
    a1 = _dot(mixed(4), a1_ref[...])
    g1 = _dot(mixed(5), g1_ref[...])
    r = _dot(mixed(0), wrkv_ref[0])
    k = _dot(mixed(1), wrkv_ref[1])
    v = _dot(mixed(2), wrkv_ref[2])
    d = _dot(jnp.tanh(d1).astype(BF16), w2_ref[...]) + w0_ref[...]
    a = _sigmoid(a0_ref[...] + _dot(a1.astype(BF16), a2_ref[...]))
    g_out[...] = _dot(_sigmoid(g1).astype(BF16), g2_ref[...]).astype(BF16)
    z = -d
    softplus = jnp.maximum(z, 0.0) + jnp.log(1.0 + jnp.exp(-jnp.abs(z)))
    lw_out[...] = -jnp.exp(-softplus - 0.5)
    kk = k * kk_ref[...]
    ss = _dot((kk * kk).astype(BF16), gsum_ref[...])
    kk = kk * _group_bcast(1.0 / jnp.maximum(jnp.sqrt(ss), 1e-12), gexp_ref[...])
    r_out[...] = r.astype(BF16)
    k_out[...] = (k * (1.0 + (a - 1.0) * ka_ref[...])).astype(BF16)
    v_out[...] = v.astype(BF16)
    kk_out[...] = kk.astype(BF16)
    b_out[...] = (kk * a).astype(BF16)


def _pad_to(w, axis, n):
    pad = [(0, 0)] * w.ndim
    pad[axis] = (0, n - w.shape[axis])
    return jnp.pad(w, pad)


def _round_up(n, m):
    return -(-n // m) * m


def _rw_prep(x2d, mod3, g, p, *, tm, tiles_per_batch):
    N, D = x2d.shape
    gsum, gexp = _group_mats(D)
    row = lambda t: t.reshape(1, D)
    lora = lambda w1, w2: (_pad_to(w1, 1, _round_up(w1.shape[1], LANES)).astype(BF16),
                           _pad_to(w2, 0, _round_up(w2.shape[0], LANES)).astype(BF16))
    w1, w2 = lora(p["decay_w1"], p["decay_w2"])
    a1, a2 = lora(p["iclr_a1"], p["iclr_a2"])
    g1, g2 = lora(p["gate_g1"], p["gate_g2"])
    consts = [row(g), p["mu"], p["w_rkv"].astype(BF16), row(p["decay_w0"]), w1, w2, row(p["iclr_a0"]), a1, a2,
              g1, g2, row(p["k_k"]), row(p["k_a"]), gsum, gexp]
    tile = pl.BlockSpec((tm, D), lambda i: (i, 0))
    outs = [jax.ShapeDtypeStruct((N, D), BF16)] * 7
    outs[1] = jax.ShapeDtypeStruct((N, D), F32)
    return pl.pallas_call(
        functools.partial(_rw_prep_kernel, tiles_per_batch=tiles_per_batch),
        grid=(N // tm,),
        in_specs=[tile,
                  pl.BlockSpec((8, D), lambda i: (jnp.maximum(i * (tm // 8) - 1, 0), 0)),
                  pl.BlockSpec((None, 3, D), lambda i: (i // tiles_per_batch, 0, 0))]
                 + [_const_spec(c.shape) for c in consts],
        out_specs=[tile] * 7,
        out_shape=outs,
        compiler_params=_cparams("parallel"),
        name="rw_prep",
    )(x2d, x2d, mod3, *consts)


def _stack_heads(x, m0):
    z = jnp.zeros_like(x)
    return jnp.concatenate([jnp.where(m0, x, z), jnp.where(m0, z, x)], axis=0)


def _rw_scan_kernel(r_ref, lw_ref, k_ref, v_ref, kk_ref, b_ref, rk_ref, lng_ref, lnb_ref, o_ref, st_ref, os_ref,
                    *, tt, group):
    C = RW_CHUNK
    C2 = 2 * C

    @pl.when(pl.program_id(2) == 0)
    def _():
        st_ref[...] = jnp.zeros_like(st_ref)

    lane = lax.broadcasted_iota(jnp.int32, (1, LANES), 1)
    m0 = lane < HEAD_DIM
    ti = lax.broadcasted_iota(jnp.int32, (C, C), 0)
    tj = lax.broadcasted_iota(jnp.int32, (C, C), 1)
    tri = (ti >= tj).astype(F32)
    ri = lax.broadcasted_iota(jnp.int32, (2 * C2, 2 * C2), 0)
    ci = lax.broadcasted_iota(jnp.int32, (2 * C2, 2 * C2), 1)
    rt, ct = ri % C, ci % C
    causal = rt - ct + (ri >= C2).astype(jnp.int32) > 0
    eye = (lax.broadcasted_iota(jnp.int32, (C2, C2), 0) == lax.broadcasted_iota(jnp.int32, (C2, C2), 1))
    eye_f = eye.astype(F32)
    si = lax.broadcasted_iota(jnp.int32, (C2, C2), 0)
    sj = lax.broadcasted_iota(jnp.int32, (C2, C2), 1)
    merge_masks = []
    s = 1
    while s < C:
        merge_masks.append(jnp.logical_and(si // (2 * s) == sj // (2 * s),
                                           jnp.logical_and(si % (2 * s) >= s, sj % (2 * s) < s)))
        s *= 2
    zeros_sq = jnp.zeros((C2, C2), BF16)

    def chunk_terms(chunks):
        n = range(len(chunks))
        a_t, r_t, v_s, aa, lhs_top, p_col = [], [], [], [], [], []
        for i, c in enumerate(chunks):
            rows = slice(c * C, (c + 1) * C)
            lw = lw_ref[rows, :]
            r = r_ref[rows, :].astype(F32)
            k = k_ref[rows, :].astype(F32)
            kk = kk_ref[rows, :].astype(F32)
            b = b_ref[rows, :].astype(F32)
            cum = jnp.dot(tri, lw, precision=HIGHEST, preferred_element_type=F32)
            tot = cum[C - 1:C, :]
            e_neg = jnp.exp(-cum)
            r_t.append(_stack_heads(r * jnp.exp(cum), m0).astype(BF16))
            a_t.append(_stack_heads(-kk * jnp.exp(cum - lw), m0).astype(BF16))
            b_t = _stack_heads(b * e_neg, m0).astype(BF16)
            k_t = _stack_heads(k * e_neg, m0).astype(BF16)
            e_end = jnp.exp(tot - cum)
            b_h = _stack_heads(b * e_end, m0)
            k_h = _stack_heads(k * e_end, m0)
            lhs_top.append(jnp.concatenate([b_h.T, k_h.T], axis=1).astype(BF16))
            v_s.append(_stack_heads(v_ref[rows, :], m0))
            p_col.append(jnp.sum(jnp.where(eye, jnp.exp(tot), 0.0), axis=1, keepdims=True))
            full = _dot_nt(jnp.concatenate([a_t[i], r_t[i]], axis=0), jnp.concatenate([b_t, k_t], axis=0))
            aa.append(jnp.where(causal, full, 0.0))
        yield

        a_ab = [aa[i][:C2, :C2] for i in n]
        t_inv = [eye_f + jnp.where(merge_masks[0], a_ab[i], 0.0) for i in n]
        for mask in merge_masks[1:]:
            tb = [t_inv[i].astype(BF16) for i in n]
            half = [_dot(tb[i], jnp.where(mask, a_ab[i], 0.0).astype(BF16)).astype(BF16) for i in n]
            yield
            t_inv = [t_inv[i] + _dot(half[i], tb[i]) for i in n]
            yield
        akv = [_dot(aa[i][:C2, C2:].astype(BF16), v_s[i]).astype(BF16) for i in n]
        yield
        au = [_dot(t_inv[i].astype(BF16), jnp.concatenate([a_t[i], akv[i]], axis=1)).astype(BF16)
              for i in n]
        yield
        big = []
        for i in n:
            rhs = jnp.concatenate([au[i], jnp.concatenate([zeros_sq, v_s[i]], axis=1)], axis=0)
            lhs = jnp.concatenate([lhs_top[i], aa[i][C2:, :].astype(BF16)], axis=0)
            big.append(_dot(lhs, rhs))
        qg = [jnp.concatenate([big[i][C2:, :C2] + r_t[i].astype(F32), big[i][:C2, :C2]], axis=0).astype(BF16)
              for i in n]
        return qg, big, p_col

    def state_steps(st, chunks, terms):
        qg, big, p_col = terms
        for i, c in enumerate(chunks):
            res = _dot(qg[i], st[0].astype(BF16))
            o_st = res[:C2] + big[i][C2:, C2:]
            os_ref[c * C:(c + 1) * C, :] = o_st[:C] + o_st[C:]
            st[0] = p_col[i] * st[0] + res[C2:] + big[i][:C2, C2:]
            yield

    st = [st_ref[...]]
    n_chunks = tt // C
    pending = iter(())
    for g0 in range(0, n_chunks, group):
        chunks = list(range(g0, min(g0 + group, n_chunks)))
        gen = chunk_terms(chunks)
        while True:
            try:
                next(gen)
            except StopIteration as done:
                terms = done.value
                break
            next(pending, None)
        for _ in pending:
            pass
        pending = state_steps(st, chunks, terms)
    for _ in pending:
        pass
    st_ref[...] = st[0]

    o = os_ref[...]
    blk = ((lax.broadcasted_iota(jnp.int32, (LANES, LANES), 0) < HEAD_DIM)
           == (lax.broadcasted_iota(jnp.int32, (LANES, LANES), 1) < HEAD_DIM)).astype(BF16)

    def head_sum(x):
        hi = x.astype(BF16)
        lo = (x - hi.astype(F32)).astype(BF16)
        return _dot(hi, blk) + _dot(lo, blk)

    mean = head_sum(o) * (1.0 / HEAD_DIM)
    cen = o - mean
    var = head_sum(cen * cen) * (1.0 / HEAD_DIM)
    y = cen * lax.rsqrt(var + RW_LNX_EPS) * lng_ref[...] + lnb_ref[...]
    r = r_ref[...].astype(F32)
    k = k_ref[...].astype(F32)
    bonus = head_sum(r * k * rk_ref[...]) * v_ref[...].astype(F32)
    o_ref[...] = (y + bonus).astype(BF16)


def _rw_scan(r, lw, k, v, kk, b, r_k, lnx_g, lnx_b, *, B, T, D, tt):
    hp = D // LANES
    blk = pl.BlockSpec((None, tt, LANES), lambda bi, h, t: (bi, t, h))
    vec = pl.BlockSpec((1, LANES), lambda bi, h, t: (0, h))
    r3 = lambda t: t.reshape(B, T, D)
    out = pl.pallas_call(
        functools.partial(_rw_scan_kernel, tt=tt, group=RW_GROUP),
        grid=(B, hp, T // tt),
        in_specs=[blk] * 6 + [vec] * 3,
        out_specs=blk,
        out_shape=jax.ShapeDtypeStruct((B, T, D), BF16),
        scratch_shapes=[pltpu.VMEM((LANES, LANES), F32), pltpu.VMEM((tt, LANES), F32)],
        compiler_params=_cparams("parallel", "parallel", "arbitrary"),
        name="rw_scan",
    )(r3(r), r3(lw), r3(k), r3(v), r3(kk), r3(b), r_k.reshape(1, D), lnx_g.reshape(1, D), lnx_b.reshape(1, D))
    return out.reshape(B * T, D)


def _diff_lambda_init(layer):
    return 0.8 - 0.6 * math.exp(-0.3 * layer)


def _pick_tile(n, pref):
    t = min(n, pref)
    while n % t:
        t //= 2
    return t


def kernel(x, c, norm_g, ada_w, ada_b, ffn_w_in, ffn_w_out, da_w_qkv, da_w_o, da_q_norm_g, da_k_norm_g, da_lambda, da_subln_g, rw_mu, rw_w_rkv, rw_w_o, rw_decay_w0, rw_decay_w1, rw_decay_w2, rw_iclr_a0, rw_iclr_a1, rw_iclr_a2, rw_gate_g1, rw_gate_g2, rw_k_k, rw_k_a, rw_r_k, rw_lnx_g, rw_lnx_b, sw_w_qkv, sw_w_o, sw_q_norm_g, sw_k_norm_g, sw_sinks):
    B, T, D = x.shape
    L = norm_g.shape[0]
    tm = _pick_tile(T, 512)
    tpb = T // tm
    kw = dict(tm=tm, tiles_per_batch=tpb)
    tf = _pick_tile(T, 2 * FFN_ROWS)
    fkw = dict(tm=tf, tiles_per_batch=T // tf)
    scale = HEAD_DIM ** -0.5

    mod = _ada(c, ada_w, ada_b).reshape(L, B, 3, 3, D)
    x2d = x.reshape(B * T, D)
    for i in range(L):
        kind, j = i % 3, i // 3
        x2d = _ffn(x2d, mod[i, :, 0], norm_g[i, 0], ffn_w_in[i, 0].astype(BF16), ffn_w_out[i, 0].astype(BF16), **fkw)
        m2 = mod[i, :, 1]
        if kind == 0:
            qg = jnp.tile(da_q_norm_g[j] * (scale * math.log2(math.e)), D // HEAD_DIM).reshape(1, D)
            kg = jnp.tile(da_k_norm_g[j], D // HEAD_DIM).reshape(1, D)
            q, k, vt = _da_qkv(x2d, m2, norm_g[i, 1], da_w_qkv[j].astype(BF16), qg, kg, B=B, T=T, tm=tm)
            o = _da_attn(q, k, vt, da_lambda[j], da_subln_g[j], B=B, T=T, D=D, tq=_pick_tile(T, 512),
                         lambda_init=_diff_lambda_init(i))
            mix = (o, None, m2, da_w_o[j].astype(BF16))
        elif kind == 1:
            p = dict(mu=rw_mu[j], w_rkv=rw_w_rkv[j], decay_w0=rw_decay_w0[j], decay_w1=rw_decay_w1[j],
                     decay_w2=rw_decay_w2[j], iclr_a0=rw_iclr_a0[j], iclr_a1=rw_iclr_a1[j], iclr_a2=rw_iclr_a2[j],
                     gate_g1=rw_gate_g1[j], gate_g2=rw_gate_g2[j], k_k=rw_k_k[j], k_a=rw_k_a[j])
            r, lw, k, v, kk, b, g = _rw_prep(x2d, m2, norm_g[i, 1], p, **kw)
            y = _rw_scan(r, lw, k, v, kk, b, rw_r_k[j], rw_lnx_g[j], rw_lnx_b[j], B=B, T=T, D=D,
                         tt=_pick_tile(T, 2048))
            mix = (y, g, m2, rw_w_o[j].astype(BF16))
        else:
            kv = (sw_w_qkv.shape[-1] - D) // 2
            n_kv = kv // HEAD_DIM
            qg = jnp.tile(sw_q_norm_g[j] * (scale * math.log2(math.e)), D // HEAD_DIM).reshape(1, D)
            dup = lambda w: jnp.repeat(w.reshape(D, n_kv, 1, HEAD_DIM), 2, axis=2).reshape(D, 2 * kv)
            w = sw_w_qkv[j]
            w = jnp.concatenate([w[:, :D], dup(w[:, D:D + kv]), dup(w[:, D + kv:])], axis=1).astype(BF16)
            kg = jnp.tile(sw_k_norm_g[j], 2 * n_kv).reshape(1, 2 * kv)
            q, k, v = _sw_qkv(x2d, m2, norm_g[i, 1], w, qg, kg, **kw)
            x2d = _sw_attn(q, k, v, sw_sinks[j], x2d, m2, sw_w_o[j].astype(BF16), B=B, T=T,
                           tq=_pick_tile(T, 512), group=(D // HEAD_DIM) // n_kv)
            mix = None
        x2d = _ffn(x2d, mod[i, :, 2], norm_g[i, 2], ffn_w_in[i, 1].astype(BF16), ffn_w_out[i, 1].astype(BF16),
                   mix=mix, **fkw)
    return x2d.reshape(B, T, D)
```

```python
import functools
import math

import jax
import jax.numpy as jnp
from jax import lax
from jax.experimental import pallas as pl
from jax.experimental.pallas import tpu as pltpu

F32 = jnp.float32
BF16 = jnp.bfloat16
HIGHEST = lax.Precision.HIGHEST

HEAD_DIM = 64
LANES = 128
NORM_EPS = 1e-6
SUBLN_EPS = 1e-5
RW_LNX_EPS = 64e-5
SW_WINDOW = 128
N_MOD = 9
PROJ_ROWS = 512
FFN_ROWS = 512
RW_GROUP = 8
RW_CHUNK = 64
VMEM_LIMIT = 56 * 1024 * 1024


def _cparams(*sem):
    return pltpu.CompilerParams(dimension_semantics=sem, vmem_limit_bytes=VMEM_LIMIT)


def _sigmoid(x):
    return 1.0 / (1.0 + jnp.exp(-x))


def _dot(a, b):
    return jnp.dot(a, b, preferred_element_type=F32)


def _dot_nt(a, b):
    return lax.dot_general(a, b, (((1,), (1,)), ((), ())), preferred_element_type=F32)


def _norm_mod(x, g, shift, scale):
    ms = jnp.mean(x * x, axis=-1, keepdims=True)
    return (x * lax.rsqrt(ms + NORM_EPS) * g) * (1.0 + scale) + shift


def _sub_tiles(n):
    step = min(n, PROJ_ROWS)
    return [slice(r0, r0 + step) for r0 in range(0, n, step)]


def _const_spec(shape):
    nd = len(shape)
    return pl.BlockSpec(shape, lambda *_: (0,) * nd, pipeline_mode=pl.Buffered(1))


def _group_mats(d):
    grp = jnp.arange(d) // HEAD_DIM
    gsum = (grp[:, None] == jnp.arange(LANES)[None, :]).astype(BF16)
    return gsum, jnp.concatenate([gsum.T, gsum.T], axis=0)


def _group_bcast(v, gexp):
    hi = v.astype(BF16)
    lo = (v - hi.astype(F32)).astype(BF16)
    return _dot(jnp.concatenate([hi, lo], axis=1), gexp)


def _ada_kernel(c_ref, w_ref, b_ref, o_ref):
    c = c_ref[...]
    cond = c * _sigmoid(c)
    o_ref[...] = jnp.dot(cond, w_ref[...], precision=HIGHEST, preferred_element_type=F32) + b_ref[...]


def _ada(c, ada_w, ada_b):
    L, D, ND = ada_w.shape
    B = c.shape[0]
    return pl.pallas_call(
        _ada_kernel,
        grid=(L, ND // D),
        in_specs=[pl.BlockSpec((B, D), lambda l, j: (0, 0)),
                  pl.BlockSpec((None, D, D), lambda l, j: (l, 0, j)),
                  pl.BlockSpec((None, 1, D), lambda l, j: (l, 0, j))],
        out_specs=pl.BlockSpec((None, B, D), lambda l, j: (l, 0, j)),
        out_shape=jax.ShapeDtypeStruct((L, B, ND), F32),
        compiler_params=_cparams("parallel", "parallel"),
        name="ada_mod",
    )(c, ada_w, ada_b.reshape(L, 1, ND))


def _ffn_kernel(*refs, d_ff, ck, rows, mixer):
    if mixer is None:
        x_ref, mod_ref, g_ref, win_ref, wout_ref, o_ref = refs
    elif mixer == "plain":
        x_ref, mod_ref, g_ref, win_ref, wout_ref, y_ref, mixmod_ref, wo_ref, o_ref = refs
    else:
        x_ref, mod_ref, g_ref, win_ref, wout_ref, y_ref, ygate_ref, mixmod_ref, wo_ref, o_ref = refs
    tm = x_ref.shape[0]
    subs = list(range(0, tm, rows))
    chunks = list(range(0, d_ff, ck))

    def residual_in(r0):
        x = x_ref[r0:r0 + rows, :]
        if mixer is None:
            return x
        y = y_ref[r0:r0 + rows, :]
        if mixer == "gated":
            y = (y.astype(F32) * ygate_ref[r0:r0 + rows, :].astype(F32)).astype(BF16)
        return x + mixmod_ref[2:3, :] * _dot(y, wo_ref[...])

    xs = {r0: residual_in(r0) for r0 in subs}
    hidden = {r0: _norm_mod(xs[r0], g_ref[...], mod_ref[0:1, :], mod_ref[1:2, :]).astype(BF16) for r0 in subs}

    def gate_up(r0, c0):
        h = hidden[r0]
        return _dot(h, win_ref[:, c0:c0 + ck]), _dot(h, win_ref[:, d_ff + c0:d_ff + c0 + ck])

    jobs = [(r0, c0) for r0 in subs for c0 in chunks]
    pending = gate_up(*jobs[0])
    acc = None
    for n, (r0, c0) in enumerate(jobs):
        gate, up = pending
        if n + 1 < len(jobs):
            pending = gate_up(*jobs[n + 1])
        act = (gate * _sigmoid(gate) * up).astype(BF16)
        y = _dot(act, wout_ref[c0:c0 + ck, :])
        acc = y if c0 == 0 else acc + y
        if c0 == chunks[-1]:
            o_ref[r0:r0 + rows, :] = xs[r0] + (0.5 * mod_ref[2:3, :]) * acc


def _ffn_chunk(d_ff):
    for n in (2, 1, 4, 11, 22):
        if d_ff % n == 0 and (d_ff // n) % LANES == 0:
            return d_ff // n
    return d_ff


def _ffn(x2d, mod3, g, w_in, w_out, *, tm, tiles_per_batch, mix=None):
    N, D = x2d.shape
    d_ff = w_out.shape[0]
    tile = pl.BlockSpec((tm, D), lambda i: (i, 0))
    modspec = pl.BlockSpec((None, 3, D), lambda i: (i // tiles_per_batch, 0, 0))
    ins = [x2d, mod3, g.reshape(1, D), w_in, w_out]
    specs = [tile, modspec, _const_spec((1, D)), _const_spec(w_in.shape), _const_spec(w_out.shape)]
    mixer = None
    if mix is not None:
        y, y_gate, mixmod, w_o = mix
        mixer = "plain" if y_gate is None else "gated"
        ins += [y] + ([] if y_gate is None else [y_gate]) + [mixmod, w_o]
        specs += [tile] + ([] if y_gate is None else [tile]) + [modspec, _const_spec(w_o.shape)]
    kern = functools.partial(_ffn_kernel, d_ff=d_ff, ck=_ffn_chunk(d_ff), rows=min(tm, FFN_ROWS), mixer=mixer)
    return pl.pallas_call(
        kern,
        grid=(N // tm,),
        in_specs=specs,
        out_specs=tile,
        out_shape=jax.ShapeDtypeStruct((N, D), F32),
        compiler_params=_cparams("parallel"),
        name="ffn_half" if mixer is None else "ffn_half_" + mixer,
    )(*ins)


def _group_rms_scale(y, gsum, gexp):
    ss = _dot((y * y).astype(BF16), gsum)
    return _group_bcast(lax.rsqrt(ss * (1.0 / HEAD_DIM) + NORM_EPS), gexp)


def _da_qkv_kernel(x_ref, mod_ref, g_ref, w_ref, gsum_ref, gexp_ref, qg_ref, kg_ref, q_ref, k_ref, vt_ref):
    D = x_ref.shape[1]
    for rows in _sub_tiles(x_ref.shape[0]):
        h = _norm_mod(x_ref[rows, :], g_ref[...], mod_ref[0:1, :], mod_ref[1:2, :]).astype(BF16)
        q = _dot(h, w_ref[:, 0:D])
        k = _dot(h, w_ref[:, D:2 * D])
        v = _dot(h, w_ref[:, 2 * D:3 * D])
        vt_ref[:, rows] = v.T.astype(BF16)
        q_ref[rows, :] = (q * _group_rms_scale(q, gsum_ref[...], gexp_ref[...]) * qg_ref[...]).astype(BF16)
        k_ref[rows, :] = (k * _group_rms_scale(k, gsum_ref[...], gexp_ref[...]) * kg_ref[...]).astype(BF16)


def _da_qkv(x2d, mod3, g, w_qkv, q_gain, k_gain, *, B, T, tm):
    N, D = x2d.shape
    tpb = T // tm
    gsum, gexp = _group_mats(D)
    tile = pl.BlockSpec((tm, D), lambda i: (i, 0))
    return pl.pallas_call(
        _da_qkv_kernel,
        grid=(N // tm,),
        in_specs=[tile,
                  pl.BlockSpec((None, 3, D), lambda i: (i // tpb, 0, 0)),
                  _const_spec((1, D)), _const_spec(w_qkv.shape), _const_spec(gsum.shape),
                  _const_spec(gexp.shape), _const_spec((1, D)), _const_spec((1, D))],
        out_specs=[tile, tile, pl.BlockSpec((None, D, tm), lambda i: (i // tpb, 0, i % tpb))],
        out_shape=[jax.ShapeDtypeStruct((N, D), BF16), jax.ShapeDtypeStruct((N, D), BF16),
                   jax.ShapeDtypeStruct((B, D, T), BF16)],
        compiler_params=_cparams("parallel"),
        name="da_qkv",
    )(x2d, mod3, g.reshape(1, D), w_qkv, gsum, gexp, q_gain, k_gain)


def _da_attn_kernel(lam_ref, sg_ref, q_ref, k_ref, vt_ref, o_ref, acc_ref, sa_ref, sb_ref, *, tq):
    nq = q_ref.shape[0] // tq
    lane = lax.broadcasted_iota(jnp.int32, (1, LANES), 1)
    lam = lam_ref[...]
    lambda_init = lam[4:5, 0:1]
    lam_full = (jnp.exp(jnp.sum(lam[0:1] * lam[1:2], axis=1, keepdims=True))
                - jnp.exp(jnp.sum(lam[2:3] * lam[3:4], axis=1, keepdims=True)) + lambda_init)
    kpos = lax.broadcasted_iota(jnp.int32, (tq, tq), 0)
    qpos = lax.broadcasted_iota(jnp.int32, (tq, tq), 1)
    causal = kpos <= qpos
    bufs = (sa_ref, sb_ref)
    q_maps = {}

    def q_masked(qi):
        if qi not in q_maps:
            q = q_ref[qi * tq:(qi + 1) * tq, :]
            zero = jnp.zeros_like(q)
            q_maps[qi] = (jnp.where(lane < HEAD_DIM, q, zero), jnp.where(lane >= HEAD_DIM, q, zero))
        return q_maps[qi]

    def scores_into(s_ref, qi, j):
        k = k_ref[j * tq:(j + 1) * tq, :]
        for c in range(2):
            s_ref[c] = _dot_nt(k, q_masked(qi)[c])

    def consume(s_ref, acc, qi, j, carry):
        vt = vt_ref[:, j * tq:(j + 1) * tq]
        out = []
        for c in range(2):
            s = s_ref[c]
            if j == qi:
                s = jnp.where(causal, s, -jnp.inf)
            m_cur = jnp.max(s, axis=0, keepdims=True)
            if j == 0:
                m_new = m_cur
                p = jnp.exp2(s - m_new)
                l_new = jnp.sum(p, axis=0, keepdims=True)
                acc[c] = _dot(vt, p.astype(BF16))
            else:
                m_prev, l_prev = carry[2 * c], carry[2 * c + 1]
                m_new = jnp.maximum(m_prev, m_cur)
                alpha = jnp.exp2(m_prev - m_new)
                p = jnp.exp2(s - m_new)
                l_new = alpha * l_prev + jnp.sum(p, axis=0, keepdims=True)
                acc[c] = alpha * acc[c] + _dot(vt, p.astype(BF16))
            out += [m_new, l_new]
        return tuple(out)

    def finish(acc, qi, carry):
        ot = acc[0] / carry[1] - lam_full * (acc[1] / carry[3])
        ms = jnp.mean(ot * ot, axis=0, keepdims=True)
        ot = ot * lax.rsqrt(ms + SUBLN_EPS)
        o_ref[qi * tq:(qi + 1) * tq, :] = (ot.T * sg_ref[...] * (1.0 - lambda_init)).astype(BF16)

    tasks = [(qi, j) for qi in range(nq) for j in range(qi + 1)]
    scores_into(bufs[0], *tasks[0])
    carry = None
    for t, (qi, j) in enumerate(tasks):
        if t + 1 < len(tasks):
            scores_into(bufs[(t + 1) % 2], *tasks[t + 1])
        acc = acc_ref.at[qi % 2]
        carry = consume(bufs[t % 2], acc, qi, j, carry)
        if j == qi:
            finish(acc, qi, carry)


def _da_attn(q, k, vt, lambdas, subln_g, *, B, T, D, tq, lambda_init):
    H = D // LANES
    q3, k3 = q.reshape(B, T, D), k.reshape(B, T, D)
    seq = pl.BlockSpec((None, T, LANES), lambda b, h: (b, 0, h))
    out = pl.pallas_call(
        functools.partial(_da_attn_kernel, tq=tq),
        grid=(B, H),
        in_specs=[pl.BlockSpec((5, HEAD_DIM), lambda b, h: (0, 0)),
                  pl.BlockSpec((1, LANES), lambda b, h: (0, 0)),
                  seq, seq,
                  pl.BlockSpec((None, LANES, T), lambda b, h: (b, h, 0))],
        out_specs=seq,
        out_shape=jax.ShapeDtypeStruct((B, T, D), BF16),
        scratch_shapes=[pltpu.VMEM((2, 2, LANES, tq), F32),
                        pltpu.VMEM((2, tq, tq), F32), pltpu.VMEM((2, tq, tq), F32)],
        compiler_params=_cparams("parallel", "parallel"),
        name="da_attn",
    )(jnp.concatenate([lambdas, jnp.full((1, HEAD_DIM), lambda_init, F32)], axis=0),
      subln_g.reshape(1, LANES), q3, k3, vt)
    return out.reshape(B * T, D)


def _sw_qkv_kernel(x_ref, mod_ref, g_ref, w_ref, gsum_ref, gexp_ref, ksum_ref, kexp_ref, qg_ref, kg_ref,
                   q_ref, k_ref, v_ref):
    D = x_ref.shape[1]
    kv = k_ref.shape[1]
    for rows in _sub_tiles(x_ref.shape[0]):
        h = _norm_mod(x_ref[rows, :], g_ref[...], mod_ref[0:1, :], mod_ref[1:2, :]).astype(BF16)
        q = _dot(h, w_ref[:, 0:D])
        k = _dot(h, w_ref[:, D:D + kv])
        v_ref[rows, :] = _dot(h, w_ref[:, D + kv:D + 2 * kv]).astype(BF16)
        q_ref[rows, :] = (q * _group_rms_scale(q, gsum_ref[...], gexp_ref[...]) * qg_ref[...]).astype(BF16)
        k_ref[rows, :] = (k * _group_rms_scale(k, ksum_ref[...], kexp_ref[...]) * kg_ref[...]).astype(BF16)


def _sw_qkv(x2d, mod3, g, w_qkv, q_gain, k_gain, *, tm, tiles_per_batch):
    N, D = x2d.shape
    kv = (w_qkv.shape[1] - D) // 2
    gsum, gexp = _group_mats(D)
    ksum, kexp = _group_mats(kv)
    tile = pl.BlockSpec((tm, D), lambda i: (i, 0))
    kvtile = pl.BlockSpec((tm, kv), lambda i: (i, 0))
    return pl.pallas_call(
        _sw_qkv_kernel,
        grid=(N // tm,),
        in_specs=[tile, pl.BlockSpec((None, 3, D), lambda i: (i // tiles_per_batch, 0, 0)),
                  _const_spec((1, D)), _const_spec(w_qkv.shape), _const_spec(gsum.shape),
                  _const_spec(gexp.shape), _const_spec(ksum.shape), _const_spec(kexp.shape),
                  _const_spec((1, D)), _const_spec((1, kv))],
        out_specs=[tile, kvtile, kvtile],
        out_shape=[jax.ShapeDtypeStruct((N, D), BF16), jax.ShapeDtypeStruct((N, kv), BF16),
                   jax.ShapeDtypeStruct((N, kv), BF16)],
        compiler_params=_cparams("parallel"),
        name="sw_qkv",
    )(x2d, mod3, g.reshape(1, D), w_qkv, gsum, gexp, ksum, kexp, q_gain, k_gain)


def _sw_attn_kernel(sink_ref, q_ref, k_ref, v_ref, x_ref, mod_ref, wo_ref, o_ref, att_ref, *, tq, group):
    i = pl.program_id(1)
    D = q_ref.shape[1]
    W = SW_WINDOW
    n_kv = (D // HEAD_DIM) // group
    tiles_per_kv = group // 2
    lane = lax.broadcasted_iota(jnp.int32, (1, LANES), 1)
    first = lane < HEAD_DIM
    jobs = [(sub, kvh) for sub in range(tq // W) for kvh in range(n_kv)]

    def band_start(sub):
        return pl.multiple_of(jnp.maximum(i * tq + (sub - 1) * W, 0), W)

    def scores(sub, kvh):
        k = k_ref[pl.ds(band_start(sub), 2 * W), kvh * LANES:(kvh + 1) * LANES]
        parts = []
        for b in range(tiles_per_kv):
            blk = kvh * tiles_per_kv + b
            q = q_ref[sub * W:(sub + 1) * W, blk * LANES:(blk + 1) * LANES]
            zero = jnp.zeros_like(q)
            parts += [jnp.where(first, q, zero), jnp.where(first, zero, q)]
        return _dot_nt(jnp.concatenate(parts, axis=0), k)

    pending = scores(*jobs[0])
    for n, (sub, kvh) in enumerate(jobs):
        s_all = pending
        if n + 1 < len(jobs):
            pending = scores(*jobs[n + 1])
        qpos = i * tq + sub * W + lax.broadcasted_iota(jnp.int32, (W, 2 * W), 0)
        kpos = band_start(sub) + lax.broadcasted_iota(jnp.int32, (W, 2 * W), 1)
        valid = jnp.logical_and(kpos <= qpos, kpos > qpos - W)
        probs, inv = [], []
        for g in range(group):
            s = jnp.where(valid, s_all[g * W:(g + 1) * W], -jnp.inf)
            snk = sink_ref[kvh * group + g] * math.log2(math.e)
            m = jnp.maximum(jnp.max(s, axis=1, keepdims=True), snk)
            p = jnp.exp2(s - m)
            inv.append(1.0 / (jnp.sum(p, axis=1, keepdims=True) + jnp.exp2(snk - m)))
            probs.append(p.astype(BF16))
        v = v_ref[pl.ds(band_start(sub), 2 * W), kvh * LANES:(kvh + 1) * LANES]
        o_all = _dot(jnp.concatenate(probs, axis=0), v)
        for b in range(tiles_per_kv):
            blk = kvh * tiles_per_kv + b
            o0 = o_all[2 * b * W:(2 * b + 1) * W] * inv[2 * b]
            o1 = o_all[(2 * b + 1) * W:(2 * b + 2) * W] * inv[2 * b + 1]
            att_ref[sub * W:(sub + 1) * W, blk * LANES:(blk + 1) * LANES] = jnp.where(first, o0, o1).astype(BF16)
    o_ref[...] = x_ref[...] + mod_ref[2:3, :] * _dot(att_ref[...], wo_ref[...])


def _sw_attn(q, kd, vd, sinks, x2d, mod3, w_o, *, B, T, tq, group):
    N, D = x2d.shape
    nq = T // tq
    kvd = kd.shape[1]
    tile = pl.BlockSpec((tq, D), lambda b, i: (b * nq + i, 0))
    seq = pl.BlockSpec((None, T, kvd), lambda b, i: (b, 0, 0))
    return pl.pallas_call(
        functools.partial(_sw_attn_kernel, tq=tq, group=group),
        grid=(B, nq),
        in_specs=[pl.BlockSpec(memory_space=pltpu.SMEM), tile, seq, seq, tile,
                  pl.BlockSpec((None, 3, D), lambda b, i: (b, 0, 0)), _const_spec(w_o.shape)],
        out_specs=tile,
        out_shape=jax.ShapeDtypeStruct((N, D), F32),
        scratch_shapes=[pltpu.VMEM((tq, D), BF16)],
        compiler_params=_cparams("parallel", "arbitrary"),
        name="sw_attn",
    )(sinks, q, kd.reshape(B, T, kvd), vd.reshape(B, T, kvd), x2d, mod3, w_o)


def _rw_prep_kernel(x_ref, xp_ref, mod_ref, g_ref, mu_ref, wrkv_ref, w0_ref, w1_ref, w2_ref, a0_ref, a1_ref,
                    a2_ref, g1_ref, g2_ref, kk_ref, ka_ref, gsum_ref, gexp_ref,
                    r_out, lw_out, k_out, v_out, kk_out, b_out, g_out, *, tiles_per_batch):
    shift, scale, gn = mod_ref[0:1, :], mod_ref[1:2, :], g_ref[...]
    for rows in _sub_tiles(x_ref.shape[0]):
        h = _norm_mod(x_ref[rows, :], gn, shift, scale)
        if rows.start == 0:
            hp = _norm_mod(xp_ref[...], gn, shift, scale)[7:8, :]
            hp = jnp.where(pl.program_id(0) % tiles_per_batch == 0, jnp.zeros_like(hp), hp)
        else:
            hp = _norm_mod(x_ref[rows.start - 8:rows.start, :], gn, shift, scale)[7:8, :]
        row = lax.broadcasted_iota(jnp.int32, (rows.stop - rows.start, 1), 0)
        xx = jnp.where(row == 0, hp, pltpu.roll(h, 1, axis=0)) - h

        def mixed(n):
            return (h + xx * mu_ref[n:n + 1, :]).astype(BF16)

        d1 = _dot(mixed(3), w1_ref[...])
        a1 = _dot(mixed(4), a1_ref[...])
        g1 = _dot(mixed(5), g1_ref[...])
        r = _dot(mixed(0), wrkv_ref[0])
        k = _dot(mixed(1), wrkv_ref[1])
        v = _dot(mixed(2), wrkv_ref[2])
        d = _dot(jnp.tanh(d1).astype(BF16), w2_ref[...]) + w0_ref[...]
        a = _sigmoid(a0_ref[...] + _dot(a1.astype(BF16), a2_ref[...]))
        g_out[rows, :] = _dot(_sigmoid(g1).astype(BF16), g2_ref[...]).astype(BF16)
        z = -d
        softplus = jnp.maximum(z, 0.0) + jnp.log(1.0 + jnp.exp(-jnp.abs(z)))
        lw_out[rows, :] = -jnp.exp(-softplus - 0.5)
        kk = k * kk_ref[...]
        ss = _dot((kk * kk).astype(BF16), gsum_ref[...])
        kk = kk * _group_bcast(1.0 / jnp.maximum(jnp.sqrt(ss), 1e-12), gexp_ref[...])
        r_out[rows, :] = r.astype(BF16)
        k_out[rows, :] = (k * (1.0 + (a - 1.0) * ka_ref[...])).astype(BF16)
        v_out[rows, :] = v.astype(BF16)
        kk_out[rows, :] = kk.astype(BF16)
        b_out[rows, :] = (kk * a).astype(BF16)


def _pad_to(w, axis, n):
    pad = [(0, 0)] * w.ndim
    pad[axis] = (0, n - w.shape[axis])
    return jnp.pad(w, pad)


def _round_up(n, m):
    return -(-n // m) * m


def _rw_prep(x2d, mod3, g, p, *, tm, tiles_per_batch):
    N, D = x2d.shape
    gsum, gexp = _group_mats(D)
    row = lambda t: t.reshape(1, D)
    lora = lambda w1, w2: (_pad_to(w1, 1, _round_up(w1.shape[1], LANES)).astype(BF16),
                           _pad_to(w2, 0, _round_up(w2.shape[0], LANES)).astype(BF16))
    w1, w2 = lora(p["decay_w1"], p["decay_w2"])
    a1, a2 = lora(p["iclr_a1"], p["iclr_a2"])
    g1, g2 = lora(p["gate_g1"], p["gate_g2"])
    consts = [row(g), p["mu"], p["w_rkv"].astype(BF16), row(p["decay_w0"]), w1, w2, row(p["iclr_a0"]), a1, a2,
              g1, g2, row(p["k_k"]), row(p["k_a"]), gsum, gexp]
    tile = pl.BlockSpec((tm, D), lambda i: (i, 0))
    outs = [jax.ShapeDtypeStruct((N, D), BF16)] * 7
    outs[1] = jax.ShapeDtypeStruct((N, D), F32)
    return pl.pallas_call(
        functools.partial(_rw_prep_kernel, tiles_per_batch=tiles_per_batch),
        grid=(N // tm,),
        in_specs=[tile,
                  pl.BlockSpec((8, D), lambda i: (jnp.maximum(i * (tm // 8) - 1, 0), 0)),
                  pl.BlockSpec((None, 3, D), lambda i: (i // tiles_per_batch, 0, 0))]
                 + [_const_spec(c.shape) for c in consts],
        out_specs=[tile] * 7,
        out_shape=outs,
        compiler_params=_cparams("parallel"),
        name="rw_prep",
    )(x2d, x2d, mod3, *consts)


def _stack_heads(x, m0):
    z = jnp.zeros_like(x)
    return jnp.concatenate([jnp.where(m0, x, z), jnp.where(m0, z, x)], axis=0)


def _rw_scan_kernel(r_ref, lw_ref, k_ref, v_ref, kk_ref, b_ref, rk_ref, lng_ref, lnb_ref, o_ref, st_ref, os_ref,
                    *, tt, group):
    C = RW_CHUNK
    C2 = 2 * C

    @pl.when(pl.program_id(2) == 0)
    def _():
        st_ref[...] = jnp.zeros_like(st_ref)

    lane = lax.broadcasted_iota(jnp.int32, (1, LANES), 1)
    m0 = lane < HEAD_DIM
    ti = lax.broadcasted_iota(jnp.int32, (C, C), 0)
    tj = lax.broadcasted_iota(jnp.int32, (C, C), 1)
    tri = (ti >= tj).astype(F32)
    ri = lax.broadcasted_iota(jnp.int32, (2 * C2, 2 * C2), 0)
    ci = lax.broadcasted_iota(jnp.int32, (2 * C2, 2 * C2), 1)
    rt, ct = ri % C, ci % C
    causal = rt - ct + (ri >= C2).astype(jnp.int32) > 0
    eye = (lax.broadcasted_iota(jnp.int32, (C2, C2), 0) == lax.broadcasted_iota(jnp.int32, (C2, C2), 1))
    eye_f = eye.astype(F32)
    si = lax.broadcasted_iota(jnp.int32, (C2, C2), 0)
    sj = lax.broadcasted_iota(jnp.int32, (C2, C2), 1)
    merge_masks = []
    s = 1
    while s < C:
        merge_masks.append(jnp.logical_and(si // (2 * s) == sj // (2 * s),
                                           jnp.logical_and(si % (2 * s) >= s, sj % (2 * s) < s)))
        s *= 2
    zeros_sq = jnp.zeros((C2, C2), BF16)

    def chunk_terms(chunks):
        n = range(len(chunks))
        a_t, r_t, v_s, aa, lhs_top, p_col = [], [], [], [], [], []
        for i, c in enumerate(chunks):
            rows = slice(c * C, (c + 1) * C)
            lw = lw_ref[rows, :]
            r = r_ref[rows, :].astype(F32)
            k = k_ref[rows, :].astype(F32)
            kk = kk_ref[rows, :].astype(F32)
            b = b_ref[rows, :].astype(F32)
            cum = jnp.dot(tri, lw, precision=HIGHEST, preferred_element_type=F32)
            tot = cum[C - 1:C, :]
            e_neg = jnp.exp(-cum)
            r_t.append(_stack_heads(r * jnp.exp(cum), m0).astype(BF16))
            a_t.append(_stack_heads(-kk * jnp.exp(cum - lw), m0).astype(BF16))
            b_t = _stack_heads(b * e_neg, m0).astype(BF16)
            k_t = _stack_heads(k * e_neg, m0).astype(BF16)
            e_end = jnp.exp(tot - cum)
            b_h = _stack_heads(b * e_end, m0)
            k_h = _stack_heads(k * e_end, m0)
            lhs_top.append(jnp.concatenate([b_h.T, k_h.T], axis=1).astype(BF16))
            v_s.append(_stack_heads(v_ref[rows, :], m0))
            p_col.append(jnp.sum(jnp.where(eye, jnp.exp(tot), 0.0), axis=1, keepdims=True))
            full = _dot_nt(jnp.concatenate([a_t[i], r_t[i]], axis=0), jnp.concatenate([b_t, k_t], axis=0))
            aa.append(jnp.where(causal, full, 0.0))
        yield

        a_ab = [aa[i][:C2, :C2] for i in n]
        t_inv = [eye_f + jnp.where(merge_masks[0], a_ab[i], 0.0) for i in n]
        for mask in merge_masks[1:]:
            tb = [t_inv[i].astype(BF16) for i in n]
            half = [_dot(tb[i], jnp.where(mask, a_ab[i], 0.0).astype(BF16)).astype(BF16) for i in n]
            yield
            t_inv = [t_inv[i] + _dot(half[i], tb[i]) for i in n]
            yield
        akv = [_dot(aa[i][:C2, C2:].astype(BF16), v_s[i]).astype(BF16) for i in n]
        yield
        au = [_dot(t_inv[i].astype(BF16), jnp.concatenate([a_t[i], akv[i]], axis=1)).astype(BF16)
              for i in n]
        yield
        big = []
        for i in n:
            rhs = jnp.concatenate([au[i], jnp.concatenate([zeros_sq, v_s[i]], axis=1)], axis=0)
            lhs = jnp.concatenate([lhs_top[i], aa[i][C2:, :].astype(BF16)], axis=0)
            big.append(_dot(lhs, rhs))
        qg = [jnp.concatenate([big[i][C2:, :C2] + r_t[i].astype(F32), big[i][:C2, :C2]], axis=0).astype(BF16)
              for i in n]
        return qg, big, p_col

    def state_steps(st, chunks, terms):
        qg, big, p_col = terms
        for i, c in enumerate(chunks):
            res = _dot(qg[i], st[0].astype(BF16))
            o_st = res[:C2] + big[i][C2:, C2:]
            os_ref[c * C:(c + 1) * C, :] = o_st[:C] + o_st[C:]
            st[0] = p_col[i] * st[0] + res[C2:] + big[i][:C2, C2:]
            yield

    st = [st_ref[...]]
    n_chunks = tt // C
    pending = iter(())
    for g0 in range(0, n_chunks, group):
        chunks = list(range(g0, min(g0 + group, n_chunks)))
        gen = chunk_terms(chunks)
        while True:
            try:
                next(gen)
            except StopIteration as done:
                terms = done.value
                break
            next(pending, None)
        for _ in pending:
            pass
        pending = state_steps(st, chunks, terms)
    for _ in pending:
        pass
    st_ref[...] = st[0]

    o = os_ref[...]
    blk = ((lax.broadcasted_iota(jnp.int32, (LANES, LANES), 0) < HEAD_DIM)
           == (lax.broadcasted_iota(jnp.int32, (LANES, LANES), 1) < HEAD_DIM)).astype(BF16)

    def head_sum(x):
        hi = x.astype(BF16)
        lo = (x - hi.astype(F32)).astype(BF16)
        return _dot(hi, blk) + _dot(lo, blk)

    mean = head_sum(o) * (1.0 / HEAD_DIM)
    cen = o - mean
    var = head_sum(cen * cen) * (1.0 / HEAD_DIM)
    y = cen * lax.rsqrt(var + RW_LNX_EPS) * lng_ref[...] + lnb_ref[...]
    r = r_ref[...].astype(F32)
    k = k_ref[...].astype(F32)
    bonus = head_sum(r * k * rk_ref[...]) * v_ref[...].astype(F32)
    o_ref[...] = (y + bonus).astype(BF16)


def _rw_scan(r, lw, k, v, kk, b, r_k, lnx_g, lnx_b, *, B, T, D, tt):
    hp = D // LANES
    blk = pl.BlockSpec((None, tt, LANES), lambda bi, h, t: (bi, t, h))
    vec = pl.BlockSpec((1, LANES), lambda bi, h, t: (0, h))
    r3 = lambda t: t.reshape(B, T, D)
    out = pl.pallas_call(
        functools.partial(_rw_scan_kernel, tt=tt, group=RW_GROUP),
        grid=(B, hp, T // tt),
        in_specs=[blk] * 6 + [vec] * 3,
        out_specs=blk,
        out_shape=jax.ShapeDtypeStruct((B, T, D), BF16),
        scratch_shapes=[pltpu.VMEM((LANES, LANES), F32), pltpu.VMEM((tt, LANES), F32)],
        compiler_params=_cparams("parallel", "parallel", "arbitrary"),
        name="rw_scan",
    )(r3(r), r3(lw), r3(k), r3(v), r3(kk), r3(b), r_k.reshape(1, D), lnx_g.reshape(1, D), lnx_b.reshape(1, D))
    return out.reshape(B * T, D)


def _diff_lambda_init(layer):
    return 0.8 - 0.6 * math.exp(-0.3 * layer)


def _pick_tile(n, pref):
    t = min(n, pref)
    while n % t:
        t //= 2
    return t


def kernel(x, c, norm_g, ada_w, ada_b, ffn_w_in, ffn_w_out, da_w_qkv, da_w_o, da_q_norm_g, da_k_norm_g, da_lambda, da_subln_g, rw_mu, rw_w_rkv, rw_w_o, rw_decay_w0, rw_decay_w1, rw_decay_w2, rw_iclr_a0, rw_iclr_a1, rw_iclr_a2, rw_gate_g1, rw_gate_g2, rw_k_k, rw_k_a, rw_r_k, rw_lnx_g, rw_lnx_b, sw_w_qkv, sw_w_o, sw_q_norm_g, sw_k_norm_g, sw_sinks):
    B, T, D = x.shape
    L = norm_g.shape[0]
    tm = _pick_tile(T, 2 * PROJ_ROWS)
    tpb = T // tm
    kw = dict(tm=tm, tiles_per_batch=tpb)
    tf = _pick_tile(T, 2 * FFN_ROWS)
    fkw = dict(tm=tf, tiles_per_batch=T // tf)
    scale = HEAD_DIM ** -0.5

    mod = _ada(c, ada_w, ada_b).reshape(L, B, 3, 3, D)
    x2d = x.reshape(B * T, D)
    for i in range(L):
        kind, j = i % 3, i // 3
        x2d = _ffn(x2d, mod[i, :, 0], norm_g[i, 0], ffn_w_in[i, 0].astype(BF16), ffn_w_out[i, 0].astype(BF16), **fkw)
        m2 = mod[i, :, 1]
        if kind == 0:
            qg = jnp.tile(da_q_norm_g[j] * (scale * math.log2(math.e)), D // HEAD_DIM).reshape(1, D)
            kg = jnp.tile(da_k_norm_g[j], D // HEAD_DIM).reshape(1, D)
            q, k, vt = _da_qkv(x2d, m2, norm_g[i, 1], da_w_qkv[j].astype(BF16), qg, kg, B=B, T=T, tm=tm)
            o = _da_attn(q, k, vt, da_lambda[j], da_subln_g[j], B=B, T=T, D=D, tq=_pick_tile(T, 512),
                         lambda_init=_diff_lambda_init(i))
            mix = (o, None, m2, da_w_o[j].astype(BF16))
        elif kind == 1:
            p = dict(mu=rw_mu[j], w_rkv=rw_w_rkv[j], decay_w0=rw_decay_w0[j], decay_w1=rw_decay_w1[j],
                     decay_w2=rw_decay_w2[j], iclr_a0=rw_iclr_a0[j], iclr_a1=rw_iclr_a1[j], iclr_a2=rw_iclr_a2[j],
                     gate_g1=rw_gate_g1[j], gate_g2=rw_gate_g2[j], k_k=rw_k_k[j], k_a=rw_k_a[j])
            r, lw, k, v, kk, b, g = _rw_prep(x2d, m2, norm_g[i, 1], p, **kw)
            y = _rw_scan(r, lw, k, v, kk, b, rw_r_k[j], rw_lnx_g[j], rw_lnx_b[j], B=B, T=T, D=D,
                         tt=_pick_tile(T, 2048))
            mix = (y, g, m2, rw_w_o[j].astype(BF16))
        else:
            kv = (sw_w_qkv.shape[-1] - D) // 2
            n_kv = kv // HEAD_DIM
            qg = jnp.tile(sw_q_norm_g[j] * (scale * math.log2(math.e)), D // HEAD_DIM).reshape(1, D)
            dup = lambda w: jnp.repeat(w.reshape(D, n_kv, 1, HEAD_DIM), 2, axis=2).reshape(D, 2 * kv)
            w = sw_w_qkv[j]
            w = jnp.concatenate([w[:, :D], dup(w[:, D:D + kv]), dup(w[:, D + kv:])], axis=1).astype(BF16)
            kg = jnp.tile(sw_k_norm_g[j], 2 * n_kv).reshape(1, 2 * kv)
            q, k, v = _sw_qkv(x2d, m2, norm_g[i, 1], w, qg, kg, **kw)
            x2d = _sw_attn(q, k, v, sw_sinks[j], x2d, m2, sw_w_o[j].astype(BF16), B=B, T=T,
                           tq=_pick_tile(T, 512), group=(D // HEAD_DIM) // n_kv)
            mix = None
        x2d = _ffn(x2d, mod[i, :, 2], norm_g[i, 2], ffn_w_in[i, 1].astype(BF16), ffn_w_out[i, 1].astype(BF16),
                   mix=mix, **fkw)
    return x2d.reshape(B, T, D)
```

```python
import functools
import math

import jax
import jax.numpy as jnp
from jax import lax
from jax.experimental import pallas as pl
from jax.experimental.pallas import tpu as pltpu

F32 = jnp.float32
BF16 = jnp.bfloat16
HIGHEST = lax.Precision.HIGHEST

HEAD_DIM = 64
LANES = 128
NORM_EPS = 1e-6
SUBLN_EPS = 1e-5
RW_LNX_EPS = 64e-5
SW_WINDOW = 128
N_MOD = 9
PROJ_ROWS = 512
FFN_ROWS = 512
RW_GROUP = 8
RW_CHUNK = 64
VMEM_LIMIT = 56 * 1024 * 1024


def _cparams(*sem):
    return pltpu.CompilerParams(dimension_semantics=sem, vmem_limit_bytes=VMEM_LIMIT)


def _sigmoid(x):
    return 1.0 / (1.0 + jnp.exp(-x))


def _dot(a, b):
    return jnp.dot(a, b, preferred_element_type=F32)


def _dot_nt(a, b):
    return lax.dot_general(a, b, (((1,), (1,)), ((), ())), preferred_element_type=F32)


def _norm_mod(x, g, shift, scale):
    ms = jnp.mean(x * x, axis=-1, keepdims=True)
    return (x * lax.rsqrt(ms + NORM_EPS) * g) * (1.0 + scale) + shift


def _sub_tiles(n):
    step = min(n, PROJ_ROWS)
    return [slice(r0, r0 + step) for r0 in range(0, n, step)]


def _const_spec(shape):
    nd = len(shape)
    return pl.BlockSpec(shape, lambda *_: (0,) * nd, pipeline_mode=pl.Buffered(1))


def _group_mats(d):
    grp = jnp.arange(d) // HEAD_DIM
    gsum = (grp[:, None] == jnp.arange(LANES)[None, :]).astype(BF16)
    return gsum, jnp.concatenate([gsum.T, gsum.T], axis=0)


def _group_bcast(v, gexp):
    hi = v.astype(BF16)
    lo = (v - hi.astype(F32)).astype(BF16)
    return _dot(jnp.concatenate([hi, lo], axis=1), gexp)


def _ada_kernel(c_ref, w_ref, b_ref, o_ref):
    c = c_ref[...]
    cond = c * _sigmoid(c)
    o_ref[...] = jnp.dot(cond, w_ref[...], precision=HIGHEST, preferred_element_type=F32) + b_ref[...]


def _ada(c, ada_w, ada_b):
    L, D, ND = ada_w.shape
    B = c.shape[0]
    return pl.pallas_call(
        _ada_kernel,
        grid=(L, ND // D),
        in_specs=[pl.BlockSpec((B, D), lambda l, j: (0, 0)),
                  pl.BlockSpec((None, D, D), lambda l, j: (l, 0, j)),
                  pl.BlockSpec((None, 1, D), lambda l, j: (l, 0, j))],
        out_specs=pl.BlockSpec((None, B, D), lambda l, j: (l, 0, j)),
        out_shape=jax.ShapeDtypeStruct((L, B, ND), F32),
        compiler_params=_cparams("parallel", "parallel"),
        name="ada_mod",
    )(c, ada_w, ada_b.reshape(L, 1, ND))


def _ffn_kernel(*refs, d_ff, ck, rows, mixer):
    if mixer is None:
        x_ref, mod_ref, g_ref, win_ref, wout_ref, o_ref = refs
    elif mixer == "plain":
        x_ref, mod_ref, g_ref, win_ref, wout_ref, y_ref, mixmod_ref, wo_ref, o_ref = refs
    else:
        x_ref, mod_ref, g_ref, win_ref, wout_ref, y_ref, ygate_ref, mixmod_ref, wo_ref, o_ref = refs
    tm = x_ref.shape[0]
    subs = list(range(0, tm, rows))
    chunks = list(range(0, d_ff, ck))

    def residual_in(r0):
        x = x_ref[r0:r0 + rows, :]
        if mixer is None:
            return x
        y = y_ref[r0:r0 + rows, :]
        if mixer == "gated":
            y = (y.astype(F32) * ygate_ref[r0:r0 + rows, :].astype(F32)).astype(BF16)
        return x + mixmod_ref[2:3, :] * _dot(y, wo_ref[...])

    xs = {r0: residual_in(r0) for r0 in subs}
    hidden = {r0: _norm_mod(xs[r0], g_ref[...], mod_ref[0:1, :], mod_ref[1:2, :]).astype(BF16) for r0 in subs}

    def gate_up(r0, c0):
        h = hidden[r0]
        return _dot(h, win_ref[:, c0:c0 + ck]), _dot(h, win_ref[:, d_ff + c0:d_ff + c0 + ck])

    jobs = [(r0, c0) for r0 in subs for c0 in chunks]
    pending = gate_up(*jobs[0])
    acc = None
    for n, (r0, c0) in enumerate(jobs):
        gate, up = pending
        if n + 1 < len(jobs):
            pending = gate_up(*jobs[n + 1])
        act = (gate * _sigmoid(gate) * up).astype(BF16)
        y = _dot(act, wout_ref[c0:c0 + ck, :])
        acc = y if c0 == 0 else acc + y
        if c0 == chunks[-1]:
            o_ref[r0:r0 + rows, :] = xs[r0] + (0.5 * mod_ref[2:3, :]) * acc


def _ffn_chunk(d_ff):
    for n in (2, 1, 4, 11, 22):
        if d_ff % n == 0 and (d_ff // n) % LANES == 0:
            return d_ff // n
    return d_ff


def _ffn(x2d, mod3, g, w_in, w_out, *, tm, tiles_per_batch, mix=None):
    N, D = x2d.shape
    d_ff = w_out.shape[0]
    tile = pl.BlockSpec((tm, D), lambda i: (i, 0))
    modspec = pl.BlockSpec((None, 3, D), lambda i: (i // tiles_per_batch, 0, 0))
    ins = [x2d, mod3, g.reshape(1, D), w_in, w_out]
    specs = [tile, modspec, _const_spec((1, D)), _const_spec(w_in.shape), _const_spec(w_out.shape)]
    mixer = None
    if mix is not None:
        y, y_gate, mixmod, w_o = mix
        mixer = "plain" if y_gate is None else "gated"
        ins += [y] + ([] if y_gate is None else [y_gate]) + [mixmod, w_o]
        specs += [tile] + ([] if y_gate is None else [tile]) + [modspec, _const_spec(w_o.shape)]
    kern = functools.partial(_ffn_kernel, d_ff=d_ff, ck=_ffn_chunk(d_ff), rows=min(tm, FFN_ROWS), mixer=mixer)
    return pl.pallas_call(
        kern,
        grid=(N // tm,),
        in_specs=specs,
        out_specs=tile,
        out_shape=jax.ShapeDtypeStruct((N, D), F32),
        compiler_params=_cparams("parallel"),
        name="ffn_half" if mixer is None else "ffn_half_" + mixer,
    )(*ins)


def _group_rms_scale(y, gsum, gexp):
    ss = _dot((y * y).astype(BF16), gsum)
    return _group_bcast(lax.rsqrt(ss * (1.0 / HEAD_DIM) + NORM_EPS), gexp)


def _da_qkv_kernel(x_ref, mod_ref, g_ref, w_ref, gsum_ref, gexp_ref, qg_ref, kg_ref, q_ref, k_ref, vt_ref):
    D = x_ref.shape[1]
    for rows in _sub_tiles(x_ref.shape[0]):
        h = _norm_mod(x_ref[rows, :], g_ref[...], mod_ref[0:1, :], mod_ref[1:2, :]).astype(BF16)
        q = _dot(h, w_ref[:, 0:D])
        k = _dot(h, w_ref[:, D:2 * D])
        v = _dot(h, w_ref[:, 2 * D:3 * D])
        vt_ref[:, rows] = v.T.astype(BF16)
        q_ref[rows, :] = (q * _group_rms_scale(q, gsum_ref[...], gexp_ref[...]) * qg_ref[...]).astype(BF16)
        k_ref[rows, :] = (k * _group_rms_scale(k, gsum_ref[...], gexp_ref[...]) * kg_ref[...]).astype(BF16)


def _da_qkv(x2d, mod3, g, w_qkv, q_gain, k_gain, *, B, T, tm):
    N, D = x2d.shape
    tpb = T // tm
    gsum, gexp = _group_mats(D)
    tile = pl.BlockSpec((tm, D), lambda i: (i, 0))
    return pl.pallas_call(
        _da_qkv_kernel,
        grid=(N // tm,),
        in_specs=[tile,
                  pl.BlockSpec((None, 3, D), lambda i: (i // tpb, 0, 0)),
                  _const_spec((1, D)), _const_spec(w_qkv.shape), _const_spec(gsum.shape),
                  _const_spec(gexp.shape), _const_spec((1, D)), _const_spec((1, D))],
        out_specs=[tile, tile, pl.BlockSpec((None, D, tm), lambda i: (i // tpb, 0, i % tpb))],
        out_shape=[jax.ShapeDtypeStruct((N, D), BF16), jax.ShapeDtypeStruct((N, D), BF16),
                   jax.ShapeDtypeStruct((B, D, T), BF16)],
        compiler_params=_cparams("parallel"),
        name="da_qkv",
    )(x2d, mod3, g.reshape(1, D), w_qkv, gsum, gexp, q_gain, k_gain)


def _da_attn_kernel(lam_ref, sg_ref, q_ref, k_ref, vt_ref, o_ref, acc_ref, sa_ref, sb_ref, *, tq):
    nq = q_ref.shape[0] // tq
    lane = lax.broadcasted_iota(jnp.int32, (1, LANES), 1)
    lam = lam_ref[...]
    lambda_init = lam[4:5, 0:1]
    lam_full = (jnp.exp(jnp.sum(lam[0:1] * lam[1:2], axis=1, keepdims=True))
                - jnp.exp(jnp.sum(lam[2:3] * lam[3:4], axis=1, keepdims=True)) + lambda_init)
    kpos = lax.broadcasted_iota(jnp.int32, (tq, tq), 0)
    qpos = lax.broadcasted_iota(jnp.int32, (tq, tq), 1)
    causal = kpos <= qpos
    bufs = (sa_ref, sb_ref)
    q_maps = {}

    def q_masked(qi):
        if qi not in q_maps:
            q = q_ref[qi * tq:(qi + 1) * tq, :]
            zero = jnp.zeros_like(q)
            q_maps[qi] = (jnp.where(lane < HEAD_DIM, q, zero), jnp.where(lane >= HEAD_DIM, q, zero))
        return q_maps[qi]

    def scores_into(s_ref, qi, j):
        k = k_ref[j * tq:(j + 1) * tq, :]
        for c in range(2):
            s_ref[c] = _dot_nt(k, q_masked(qi)[c])

    def consume(s_ref, acc, qi, j, carry):
        vt = vt_ref[:, j * tq:(j + 1) * tq]
        out = []
        for c in range(2):
            s = s_ref[c]
            if j == qi:
                s = jnp.where(causal, s, -jnp.inf)
            m_cur = jnp.max(s, axis=0, keepdims=True)
            if j == 0:
                m_new = m_cur
                p = jnp.exp2(s - m_new)
                l_new = jnp.sum(p, axis=0, keepdims=True)
                acc[c] = _dot(vt, p.astype(BF16))
            else:
                m_prev, l_prev = carry[2 * c], carry[2 * c + 1]
                m_new = jnp.maximum(m_prev, m_cur)
                alpha = jnp.exp2(m_prev - m_new)
                p = jnp.exp2(s - m_new)
                l_new = alpha * l_prev + jnp.sum(p, axis=0, keepdims=True)
                acc[c] = alpha * acc[c] + _dot(vt, p.astype(BF16))
            out += [m_new, l_new]
        return tuple(out)

    def finish(acc, qi, carry):
        ot = acc[0] / carry[1] - lam_full * (acc[1] / carry[3])
        ms = jnp.mean(ot * ot, axis=0, keepdims=True)
        ot = ot * lax.rsqrt(ms + SUBLN_EPS)
        o_ref[qi * tq:(qi + 1) * tq, :] = (ot.T * sg_ref[...] * (1.0 - lambda_init)).astype(BF16)

    tasks = [(qi, j) for qi in range(nq) for j in range(qi + 1)]
    scores_into(bufs[0], *tasks[0])
    carry = None
    for t, (qi, j) in enumerate(tasks):
        if t + 1 < len(tasks):
            scores_into(bufs[(t + 1) % 2], *tasks[t + 1])
        acc = acc_ref.at[qi % 2]
        carry = consume(bufs[t % 2], acc, qi, j, carry)
        if j == qi:
            finish(acc, qi, carry)


def _da_attn(q, k, vt, lambdas, subln_g, *, B, T, D, tq, lambda_init):
    H = D // LANES
    q3, k3 = q.reshape(B, T, D), k.reshape(B, T, D)
    seq = pl.BlockSpec((None, T, LANES), lambda b, h: (b, 0, h))
    out = pl.pallas_call(
        functools.partial(_da_attn_kernel, tq=tq),
        grid=(B, H),
        in_specs=[pl.BlockSpec((5, HEAD_DIM), lambda b, h: (0, 0)),
                  pl.BlockSpec((1, LANES), lambda b, h: (0, 0)),
                  seq, seq,
                  pl.BlockSpec((None, LANES, T), lambda b, h: (b, h, 0))],
        out_specs=seq,
        out_shape=jax.ShapeDtypeStruct((B, T, D), BF16),
        scratch_shapes=[pltpu.VMEM((2, 2, LANES, tq), F32),
                        pltpu.VMEM((2, tq, tq), F32), pltpu.VMEM((2, tq, tq), F32)],
        compiler_params=_cparams("parallel", "parallel"),
        name="da_attn",
    )(jnp.concatenate([lambdas, jnp.full((1, HEAD_DIM), lambda_init, F32)], axis=0),
      subln_g.reshape(1, LANES), q3, k3, vt)
    return out.reshape(B * T, D)


def _sw_qkv_kernel(x_ref, mod_ref, g_ref, w_ref, gsum_ref, gexp_ref, ksum_ref, kexp_ref, qg_ref, kg_ref,
                   q_ref, k_ref, v_ref):
    D = x_ref.shape[1]
    kv = k_ref.shape[1]
    for rows in _sub_tiles(x_ref.shape[0]):
        h = _norm_mod(x_ref[rows, :], g_ref[...], mod_ref[0:1, :], mod_ref[1:2, :]).astype(BF16)
        q = _dot(h, w_ref[:, 0:D])
        k = _dot(h, w_ref[:, D:D + kv])
        v_ref[rows, :] = _dot(h, w_ref[:, D + kv:D + 2 * kv]).astype(BF16)
        q_ref[rows, :] = (q * _group_rms_scale(q, gsum_ref[...], gexp_ref[...]) * qg_ref[...]).astype(BF16)
        k_ref[rows, :] = (k * _group_rms_scale(k, ksum_ref[...], kexp_ref[...]) * kg_ref[...]).astype(BF16)


def _sw_qkv(x2d, mod3, g, w_qkv, q_gain, k_gain, *, tm, tiles_per_batch):
    N, D = x2d.shape
    kv = (w_qkv.shape[1] - D) // 2
    gsum, gexp = _group_mats(D)
    ksum, kexp = _group_mats(kv)
    tile = pl.BlockSpec((tm, D), lambda i: (i, 0))
    kvtile = pl.BlockSpec((tm, kv), lambda i: (i, 0))
    return pl.pallas_call(
        _sw_qkv_kernel,
        grid=(N // tm,),
        in_specs=[tile, pl.BlockSpec((None, 3, D), lambda i: (i // tiles_per_batch, 0, 0)),
                  _const_spec((1, D)), _const_spec(w_qkv.shape), _const_spec(gsum.shape),
                  _const_spec(gexp.shape), _const_spec(ksum.shape), _const_spec(kexp.shape),
                  _const_spec((1, D)), _const_spec((1, kv))],
        out_specs=[tile, kvtile, kvtile],
        out_shape=[jax.ShapeDtypeStruct((N, D), BF16), jax.ShapeDtypeStruct((N, kv), BF16),
                   jax.ShapeDtypeStruct((N, kv), BF16)],
        compiler_params=_cparams("parallel"),
        name="sw_qkv",
    )(x2d, mod3, g.reshape(1, D), w_qkv, gsum, gexp, ksum, kexp, q_gain, k_gain)


def _sw_attn_kernel(sink_ref, q_ref, k_ref, v_ref, x_ref, mod_ref, wo_ref, o_ref, att_ref, *, tq, group):
    i = pl.program_id(1)
    D = q_ref.shape[1]
    W = SW_WINDOW
    n_kv = (D // HEAD_DIM) // group
    tiles_per_kv = group // 2
    lane = lax.broadcasted_iota(jnp.int32, (1, LANES), 1)
    first = lane < HEAD_DIM
    jobs = [(sub, kvh) for sub in range(tq // W) for kvh in range(n_kv)]

    def band_start(sub):
        return pl.multiple_of(jnp.maximum(i * tq + (sub - 1) * W, 0), W)

    def scores(sub, kvh):
        k = k_ref[pl.ds(band_start(sub), 2 * W), kvh * LANES:(kvh + 1) * LANES]
        parts = []
        for b in range(tiles_per_kv):
            blk = kvh * tiles_per_kv + b
            q = q_ref[sub * W:(sub + 1) * W, blk * LANES:(blk + 1) * LANES]
            zero = jnp.zeros_like(q)
            parts += [jnp.where(first, q, zero), jnp.where(first, zero, q)]
        return _dot_nt(jnp.concatenate(parts, axis=0), k)

    pending = scores(*jobs[0])
    for n, (sub, kvh) in enumerate(jobs):
        s_all = pending
        if n + 1 < len(jobs):
            pending = scores(*jobs[n + 1])
        qpos = i * tq + sub * W + lax.broadcasted_iota(jnp.int32, (W, 2 * W), 0)
        kpos = band_start(sub) + lax.broadcasted_iota(jnp.int32, (W, 2 * W), 1)
        valid = jnp.logical_and(kpos <= qpos, kpos > qpos - W)
        probs, inv = [], []
        for g in range(group):
            s = jnp.where(valid, s_all[g * W:(g + 1) * W], -jnp.inf)
            snk = sink_ref[kvh * group + g] * math.log2(math.e)
            m = jnp.maximum(jnp.max(s, axis=1, keepdims=True), snk)
            p = jnp.exp2(s - m)
            inv.append(1.0 / (jnp.sum(p, axis=1, keepdims=True) + jnp.exp2(snk - m)))
            probs.append(p.astype(BF16))
        v = v_ref[pl.ds(band_start(sub), 2 * W), kvh * LANES:(kvh + 1) * LANES]
        o_all = _dot(jnp.concatenate(probs, axis=0), v)
        for b in range(tiles_per_kv):
            blk = kvh * tiles_per_kv + b
            o0 = o_all[2 * b * W:(2 * b + 1) * W] * inv[2 * b]
            o1 = o_all[(2 * b + 1) * W:(2 * b + 2) * W] * inv[2 * b + 1]
            att_ref[sub * W:(sub + 1) * W, blk * LANES:(blk + 1) * LANES] = jnp.where(first, o0, o1).astype(BF16)
    o_ref[...] = x_ref[...] + mod_ref[2:3, :] * _dot(att_ref[...], wo_ref[...])


def _sw_attn(q, kd, vd, sinks, x2d, mod3, w_o, *, B, T, tq, group):
    N, D = x2d.shape
    nq = T // tq
    kvd = kd.shape[1]
    tile = pl.BlockSpec((tq, D), lambda b, i: (b * nq + i, 0))
    seq = pl.BlockSpec((None, T, kvd), lambda b, i: (b, 0, 0))
    return pl.pallas_call(
        functools.partial(_sw_attn_kernel, tq=tq, group=group),
        grid=(B, nq),
        in_specs=[pl.BlockSpec(memory_space=pltpu.SMEM), tile, seq, seq, tile,
                  pl.BlockSpec((None, 3, D), lambda b, i: (b, 0, 0)), _const_spec(w_o.shape)],
        out_specs=tile,
        out_shape=jax.ShapeDtypeStruct((N, D), F32),
        scratch_shapes=[pltpu.VMEM((tq, D), BF16)],
        compiler_params=_cparams("parallel", "arbitrary"),
        name="sw_attn",
    )(sinks, q, kd.reshape(B, T, kvd), vd.reshape(B, T, kvd), x2d, mod3, w_o)


def _rw_prep_kernel(x_ref, xp_ref, mod_ref, g_ref, mu_ref, wrkv_ref, w0_ref, w1_ref, w2_ref, a0_ref, a1_ref,
                    a2_ref, g1_ref, g2_ref, kk_ref, ka_ref, gsum_ref, gexp_ref,
                    r_out, lw_out, k_out, v_out, kk_out, b_out, g_out, *, tiles_per_batch):
    shift, scale, gn = mod_ref[0:1, :], mod_ref[1:2, :], g_ref[...]
    for rows in _sub_tiles(x_ref.shape[0]):
        h = _norm_mod(x_ref[rows, :], gn, shift, scale)
        if rows.start == 0:
            hp = _norm_mod(xp_ref[...], gn, shift, scale)[7:8, :]
            hp = jnp.where(pl.program_id(0) % tiles_per_batch == 0, jnp.zeros_like(hp), hp)
        else:
            hp = _norm_mod(x_ref[rows.start - 8:rows.start, :], gn, shift, scale)[7:8, :]
        row = lax.broadcasted_iota(jnp.int32, (rows.stop - rows.start, 1), 0)
        xx = jnp.where(row == 0, hp, pltpu.roll(h, 1, axis=0)) - h

        def mixed(n):
            return (h + xx * mu_ref[n:n + 1, :]).astype(BF16)

        d1 = _dot(mixed(3), w1_ref[...])
        a1 = _dot(mixed(4), a1_ref[...])
        g1 = _dot(mixed(5), g1_ref[...])
        r = _dot(mixed(0), wrkv_ref[0])
        k = _dot(mixed(1), wrkv_ref[1])
        v = _dot(mixed(2), wrkv_ref[2])
        d = _dot(jnp.tanh(d1).astype(BF16), w2_ref[...]) + w0_ref[...]
        a = _sigmoid(a0_ref[...] + _dot(a1.astype(BF16), a2_ref[...]))
        g_out[rows, :] = _dot(_sigmoid(g1).astype(BF16), g2_ref[...]).astype(BF16)
        z = -d
        softplus = jnp.maximum(z, 0.0) + jnp.log(1.0 + jnp.exp(-jnp.abs(z)))
        lw_out[rows, :] = -jnp.exp(-softplus - 0.5)
        kk = k * kk_ref[...]
        ss = _dot((kk * kk).astype(BF16), gsum_ref[...])
        kk = kk * _group_bcast(1.0 / jnp.maximum(jnp.sqrt(ss), 1e-12), gexp_ref[...])
        r_out[rows, :] = r.astype(BF16)
        k_out[rows, :] = (k * (1.0 + (a - 1.0) * ka_ref[...])).astype(BF16)
        v_out[rows, :] = v.astype(BF16)
        kk_out[rows, :] = kk.astype(BF16)
        b_out[rows, :] = (kk * a).astype(BF16)


def _pad_to(w, axis, n):
    pad = [(0, 0)] * w.ndim
    pad[axis] = (0, n - w.shape[axis])
    return jnp.pad(w, pad)


def _round_up(n, m):
    return -(-n // m) * m


def _rw_prep(x2d, mod3, g, p, *, tm, tiles_per_batch):
    N, D = x2d.shape
    gsum, gexp = _group_mats(D)
    row = lambda t: t.reshape(1, D)
    lora = lambda w1, w2: (_pad_to(w1, 1, _round_up(w1.shape[1], LANES)).astype(BF16),
                           _pad_to(w2, 0, _round_up(w2.shape[0], LANES)).astype(BF16))
    w1, w2 = lora(p["decay_w1"], p["decay_w2"])
    a1, a2 = lora(p["iclr_a1"], p["iclr_a2"])
    g1, g2 = lora(p["gate_g1"], p["gate_g2"])
    consts = [row(g), p["mu"], p["w_rkv"].astype(BF16), row(p["decay_w0"]), w1, w2, row(p["iclr_a0"]), a1, a2,
              g1, g2, row(p["k_k"]), row(p["k_a"]), gsum, gexp]
    tile = pl.BlockSpec((tm, D), lambda i: (i, 0))
    outs = [jax.ShapeDtypeStruct((N, D), BF16)] * 7
    outs[1] = jax.ShapeDtypeStruct((N, D), F32)
    return pl.pallas_call(
        functools.partial(_rw_prep_kernel, tiles_per_batch=tiles_per_batch),
        grid=(N // tm,),
        in_specs=[tile,
                  pl.BlockSpec((8, D), lambda i: (jnp.maximum(i * (tm // 8) - 1, 0), 0)),
                  pl.BlockSpec((None, 3, D), lambda i: (i // tiles_per_batch, 0, 0))]
                 + [_const_spec(c.shape) for c in consts],
        out_specs=[tile] * 7,
        out_shape=outs,
        compiler_params=_cparams("parallel"),
        name="rw_prep",
    )(x2d, x2d, mod3, *consts)


def _stack_heads(x, m0):
    z = jnp.zeros_like(x)
    return jnp.concatenate([jnp.where(m0, x, z), jnp.where(m0, z, x)], axis=0)


def _rw_scan_kernel(r_ref, lw_ref, k_ref, v_ref, kk_ref, b_ref, rk_ref, lng_ref, lnb_ref, o_ref, st_ref, os_ref,
                    *, tt, group):
    C = RW_CHUNK
    C2 = 2 * C

    @pl.when(pl.program_id(2) == 0)
    def _():
        st_ref[...] = jnp.zeros_like(st_ref)

    lane = lax.broadcasted_iota(jnp.int32, (1, LANES), 1)
    m0 = lane < HEAD_DIM
    ti = lax.broadcasted_iota(jnp.int32, (C, C), 0)
    tj = lax.broadcasted_iota(jnp.int32, (C, C), 1)
    tri = (ti >= tj).astype(F32)
    ri = lax.broadcasted_iota(jnp.int32, (2 * C2, 2 * C2), 0)
    ci = lax.broadcasted_iota(jnp.int32, (2 * C2, 2 * C2), 1)
    rt, ct = ri % C, ci % C
    causal = rt - ct + (ri >= C2).astype(jnp.int32) > 0
    eye = (lax.broadcasted_iota(jnp.int32, (C2, C2), 0) == lax.broadcasted_iota(jnp.int32, (C2, C2), 1))
    eye_f = eye.astype(F32)
    si = lax.broadcasted_iota(jnp.int32, (C2, C2), 0)
    sj = lax.broadcasted_iota(jnp.int32, (C2, C2), 1)
    merge_masks = []
    s = 1
    while s < C:
        merge_masks.append(jnp.logical_and(si // (2 * s) == sj // (2 * s),
                                           jnp.logical_and(si % (2 * s) >= s, sj % (2 * s) < s)))
        s *= 2
    zeros_sq = jnp.zeros((C2, C2), BF16)

    def chunk_terms(chunks):
        n = range(len(chunks))
        a_t, r_t, v_s, aa, lhs_top, p_col = [], [], [], [], [], []
        for i, c in enumerate(chunks):
            rows = slice(c * C, (c + 1) * C)
            lw = lw_ref[rows, :]
            r = r_ref[rows, :].astype(F32)
            k = k_ref[rows, :].astype(F32)
            kk = kk_ref[rows, :].astype(F32)
            b = b_ref[rows, :].astype(F32)
            cum = jnp.dot(tri, lw, precision=HIGHEST, preferred_element_type=F32)
            tot = cum[C - 1:C, :]
            e_neg = jnp.exp(-cum)
            r_t.append(_stack_heads(r * jnp.exp(cum), m0).astype(BF16))
            a_t.append(_stack_heads(-kk * jnp.exp(cum - lw), m0).astype(BF16))
            b_t = _stack_heads(b * e_neg, m0).astype(BF16)
            k_t = _stack_heads(k * e_neg, m0).astype(BF16)
            e_end = jnp.exp(tot - cum)
            b_h = _stack_heads(b * e_end, m0)
            k_h = _stack_heads(k * e_end, m0)
            lhs_top.append(jnp.concatenate([b_h.T, k_h.T], axis=1).astype(BF16))
            v_s.append(_stack_heads(v_ref[rows, :], m0))
            p_col.append(jnp.sum(jnp.where(eye, jnp.exp(tot), 0.0), axis=1, keepdims=True))
            full = _dot_nt(jnp.concatenate([a_t[i], r_t[i]], axis=0), jnp.concatenate([b_t, k_t], axis=0))
            aa.append(jnp.where(causal, full, 0.0))
        yield

        a_ab = [aa[i][:C2, :C2] for i in n]
        t_inv = [eye_f + jnp.where(merge_masks[0], a_ab[i], 0.0) for i in n]
        for mask in merge_masks[1:]:
            tb = [t_inv[i].astype(BF16) for i in n]
            half = [_dot(tb[i], jnp.where(mask, a_ab[i], 0.0).astype(BF16)).astype(BF16) for i in n]
            yield
            t_inv = [t_inv[i] + _dot(half[i], tb[i]) for i in n]
            yield
        akv = [_dot(aa[i][:C2, C2:].astype(BF16), v_s[i]).astype(BF16) for i in n]
        yield
        au = [_dot(t_inv[i].astype(BF16), jnp.concatenate([a_t[i], akv[i]], axis=1)).astype(BF16)
              for i in n]
        yield
        big = []
        for i in n:
            rhs = jnp.concatenate([au[i], jnp.concatenate([zeros_sq, v_s[i]], axis=1)], axis=0)
            lhs = jnp.concatenate([lhs_top[i], aa[i][C2:, :].astype(BF16)], axis=0)
            big.append(_dot(lhs, rhs))
        qg = [jnp.concatenate([big[i][C2:, :C2] + r_t[i].astype(F32), big[i][:C2, :C2]], axis=0).astype(BF16)
              for i in n]
        return qg, big, p_col

    blk2 = jnp.tile(((lax.broadcasted_iota(jnp.int32, (LANES, LANES), 0) < HEAD_DIM)
                     == (lax.broadcasted_iota(jnp.int32, (LANES, LANES), 1) < HEAD_DIM)).astype(BF16), (2, 1))

    def head_sum(x):
        hi = x.astype(BF16)
        lo = (x - hi.astype(F32)).astype(BF16)
        return _dot(jnp.concatenate([hi, lo], axis=1), blk2)

    def state_steps(st, chunks, terms):
        qg, big, p_col = terms
        for i, c in enumerate(chunks):
            res = _dot(qg[i], st[0].astype(BF16))
            o_st = res[:C2] + big[i][C2:, C2:]
            os_ref[c * C:(c + 1) * C, :] = o_st[:C] + o_st[C:]
            st[0] = p_col[i] * st[0] + res[C2:] + big[i][:C2, C2:]
            yield
        rows = slice(chunks[0] * C, (chunks[-1] + 1) * C)
        o = os_ref[rows, :]
        mean = head_sum(o) * (1.0 / HEAD_DIM)
        cen = o - mean
        var = head_sum(cen * cen) * (1.0 / HEAD_DIM)
        y = cen * lax.rsqrt(var + RW_LNX_EPS) * lng_ref[...] + lnb_ref[...]
        r = r_ref[rows, :].astype(F32)
        k = k_ref[rows, :].astype(F32)
        bonus = head_sum(r * k * rk_ref[...]) * v_ref[rows, :].astype(F32)
        o_ref[rows, :] = (y + bonus).astype(BF16)
        yield

    st = [st_ref[...]]
    n_chunks = tt // C
    pending = iter(())
    for g0 in range(0, n_chunks, group):
        chunks = list(range(g0, min(g0 + group, n_chunks)))
        gen = chunk_terms(chunks)
        while True:
            try:
                next(gen)
            except StopIteration as done:
                terms = done.value
                break
            next(pending, None)
        for _ in pending:
            pass
        pending = state_steps(st, chunks, terms)
    for _ in pending:
        pass
    st_ref[...] = st[0]


def _rw_scan(r, lw, k, v, kk, b, r_k, lnx_g, lnx_b, *, B, T, D, tt):
    hp = D // LANES
    blk = pl.BlockSpec((None, tt, LANES), lambda bi, h, t: (bi, t, h))
    vec = pl.BlockSpec((1, LANES), lambda bi, h, t: (0, h))
    r3 = lambda t: t.reshape(B, T, D)
    out = pl.pallas_call(
        functools.partial(_rw_scan_kernel, tt=tt, group=RW_GROUP),
        grid=(B, hp, T // tt),
        in_specs=[blk] * 6 + [vec] * 3,
        out_specs=blk,
        out_shape=jax.ShapeDtypeStruct((B, T, D), BF16),
        scratch_shapes=[pltpu.VMEM((LANES, LANES), F32), pltpu.VMEM((tt, LANES), F32)],
        compiler_params=_cparams("parallel", "parallel", "arbitrary"),
        name="rw_scan",
    )(r3(r), r3(lw), r3(k), r3(v), r3(kk), r3(b), r_k.reshape(1, D), lnx_g.reshape(1, D), lnx_b.reshape(1, D))
    return out.reshape(B * T, D)


def _diff_lambda_init(layer):
    return 0.8 - 0.6 * math.exp(-0.3 * layer)


def _pick_tile(n, pref):
    t = min(n, pref)
    while n % t:
        t //= 2
    return t


def kernel(x, c, norm_g, ada_w, ada_b, ffn_w_in, ffn_w_out, da_w_qkv, da_w_o, da_q_norm_g, da_k_norm_g, da_lambda, da_subln_g, rw_mu, rw_w_rkv, rw_w_o, rw_decay_w0, rw_decay_w1, rw_decay_w2, rw_iclr_a0, rw_iclr_a1, rw_iclr_a2, rw_gate_g1, rw_gate_g2, rw_k_k, rw_k_a, rw_r_k, rw_lnx_g, rw_lnx_b, sw_w_qkv, sw_w_o, sw_q_norm_g, sw_k_norm_g, sw_sinks):
    B, T, D = x.shape
    L = norm_g.shape[0]
    tm = _pick_tile(T, 2 * PROJ_ROWS)
    tpb = T // tm
    kw = dict(tm=tm, tiles_per_batch=tpb)
    tf = _pick_tile(T, 2 * FFN_ROWS)
    fkw = dict(tm=tf, tiles_per_batch=T // tf)
    scale = HEAD_DIM ** -0.5

    mod = _ada(c, ada_w, ada_b).reshape(L, B, 3, 3, D)
    x2d = x.reshape(B * T, D)
    for i in range(L):
        kind, j = i % 3, i // 3
        x2d = _ffn(x2d, mod[i, :, 0], norm_g[i, 0], ffn_w_in[i, 0].astype(BF16), ffn_w_out[i, 0].astype(BF16), **fkw)
        m2 = mod[i, :, 1]
        if kind == 0:
            qg = jnp.tile(da_q_norm_g[j] * (scale * math.log2(math.e)), D // HEAD_DIM).reshape(1, D)
            kg = jnp.tile(da_k_norm_g[j], D // HEAD_DIM).reshape(1, D)
            q, k, vt = _da_qkv(x2d, m2, norm_g[i, 1], da_w_qkv[j].astype(BF16), qg, kg, B=B, T=T, tm=tm)
            o = _da_attn(q, k, vt, da_lambda[j], da_subln_g[j], B=B, T=T, D=D, tq=_pick_tile(T, 512),
                         lambda_init=_diff_lambda_init(i))
            mix = (o, None, m2, da_w_o[j].astype(BF16))
        elif kind == 1:
            p = dict(mu=rw_mu[j], w_rkv=rw_w_rkv[j], decay_w0=rw_decay_w0[j], decay_w1=rw_decay_w1[j],
                     decay_w2=rw_decay_w2[j], iclr_a0=rw_iclr_a0[j], iclr_a1=rw_iclr_a1[j], iclr_a2=rw_iclr_a2[j],
                     gate_g1=rw_gate_g1[j], gate_g2=rw_gate_g2[j], k_k=rw_k_k[j], k_a=rw_k_a[j])
            r, lw, k, v, kk, b, g = _rw_prep(x2d, m2, norm_g[i, 1], p, **kw)
            y = _rw_scan(r, lw, k, v, kk, b, rw_r_k[j], rw_lnx_g[j], rw_lnx_b[j], B=B, T=T, D=D,
                         tt=_pick_tile(T, 4096))
            mix = (y, g, m2, rw_w_o[j].astype(BF16))
        else:
            kv = (sw_w_qkv.shape[-1] - D) // 2
            n_kv = kv // HEAD_DIM
            qg = jnp.tile(sw_q_norm_g[j] * (scale * math.log2(math.e)), D // HEAD_DIM).reshape(1, D)
            dup = lambda w: jnp.repeat(w.reshape(D, n_kv, 1, HEAD_DIM), 2, axis=2).reshape(D, 2 * kv)
            w = sw_w_qkv[j]
            w = jnp.concatenate([w[:, :D], dup(w[:, D:D + kv]), dup(w[:, D + kv:])], axis=1).astype(BF16)
            kg = jnp.tile(sw_k_norm_g[j], 2 * n_kv).reshape(1, 2 * kv)
            q, k, v = _sw_qkv(x2d, m2, norm_g[i, 1], w, qg, kg, **kw)
            x2d = _sw_attn(q, k, v, sw_sinks[j], x2d, m2, sw_w_o[j].astype(BF16), B=B, T=T,
                           tq=_pick_tile(T, 512), group=(D // HEAD_DIM) // n_kv)
            mix = None
        x2d = _ffn(x2d, mod[i, :, 2], norm_g[i, 2], ffn_w_in[i, 1].astype(BF16), ffn_w_out[i, 1].astype(BF16),
                   mix=mix, **fkw)
    return x2d.reshape(B, T, D)
```

```python
import functools
import math

import jax
import jax.numpy as jnp
from jax import lax
from jax.experimental import pallas as pl
from jax.experimental.pallas import tpu as pltpu

F32 = jnp.float32
BF16 = jnp.bfloat16
HIGHEST = lax.Precision.HIGHEST

HEAD_DIM = 64
LANES = 128
NORM_EPS = 1e-6
SUBLN_EPS = 1e-5
RW_LNX_EPS = 64e-5
SW_WINDOW = 128
N_MOD = 9
PROJ_ROWS = 512
FFN_ROWS = 512
RW_GROUP = 16
RW_CHUNK = 64
VMEM_LIMIT = 56 * 1024 * 1024


def _cparams(*sem):
    return pltpu.CompilerParams(dimension_semantics=sem, vmem_limit_bytes=VMEM_LIMIT)


def _sigmoid(x):
    return 1.0 / (1.0 + jnp.exp(-x))


def _dot(a, b):
    return jnp.dot(a, b, preferred_element_type=F32)


def _dot_nt(a, b):
    return lax.dot_general(a, b, (((1,), (1,)), ((), ())), preferred_element_type=F32)


def _norm_mod(x, g, shift, scale):
    ms = jnp.mean(x * x, axis=-1, keepdims=True)
    return (x * lax.rsqrt(ms + NORM_EPS) * g) * (1.0 + scale) + shift


def _sub_tiles(n):
    step = min(n, PROJ_ROWS)
    return [slice(r0, r0 + step) for r0 in range(0, n, step)]


def _const_spec(shape):
    nd = len(shape)
    return pl.BlockSpec(shape, lambda *_: (0,) * nd, pipeline_mode=pl.Buffered(1))


def _group_mats(d):
    grp = jnp.arange(d) // HEAD_DIM
    gsum = (grp[:, None] == jnp.arange(LANES)[None, :]).astype(BF16)
    return gsum, jnp.concatenate([gsum.T, gsum.T], axis=0)


def _group_bcast(v, gexp):
    hi = v.astype(BF16)
    lo = (v - hi.astype(F32)).astype(BF16)
    return _dot(jnp.concatenate([hi, lo], axis=1), gexp)


def _ada_kernel(c_ref, w_ref, b_ref, o_ref):
    c = c_ref[...]
    cond = c * _sigmoid(c)
    o_ref[...] = jnp.dot(cond, w_ref[...], precision=HIGHEST, preferred_element_type=F32) + b_ref[...]


def _ada(c, ada_w, ada_b):
    L, D, ND = ada_w.shape
    B = c.shape[0]
    return pl.pallas_call(
        _ada_kernel,
        grid=(L, ND // D),
        in_specs=[pl.BlockSpec((B, D), lambda l, j: (0, 0)),
                  pl.BlockSpec((None, D, D), lambda l, j: (l, 0, j)),
                  pl.BlockSpec((None, 1, D), lambda l, j: (l, 0, j))],
        out_specs=pl.BlockSpec((None, B, D), lambda l, j: (l, 0, j)),
        out_shape=jax.ShapeDtypeStruct((L, B, ND), F32),
        compiler_params=_cparams("parallel", "parallel"),
        name="ada_mod",
    )(c, ada_w, ada_b.reshape(L, 1, ND))


def _ffn_kernel(*refs, d_ff, ck, rows, mixer):
    if mixer is None:
        x_ref, mod_ref, g_ref, win_ref, wout_ref, o_ref = refs
    elif mixer == "plain":
        x_ref, mod_ref, g_ref, win_ref, wout_ref, y_ref, mixmod_ref, wo_ref, o_ref = refs
    else:
        x_ref, mod_ref, g_ref, win_ref, wout_ref, y_ref, ygate_ref, mixmod_ref, wo_ref, o_ref = refs
    tm = x_ref.shape[0]
    subs = list(range(0, tm, rows))
    chunks = list(range(0, d_ff, ck))

    def residual_in(r0):
        x = x_ref[r0:r0 + rows, :]
        if mixer is None:
            return x
        y = y_ref[r0:r0 + rows, :]
        if mixer == "gated":
            y = (y.astype(F32) * ygate_ref[r0:r0 + rows, :].astype(F32)).astype(BF16)
        return x + mixmod_ref[2:3, :] * _dot(y, wo_ref[...])

    xs = {r0: residual_in(r0) for r0 in subs}
    hidden = {r0: _norm_mod(xs[r0], g_ref[...], mod_ref[0:1, :], mod_ref[1:2, :]).astype(BF16) for r0 in subs}

    def gate_up(r0, c0):
        h = hidden[r0]
        return _dot(h, win_ref[:, c0:c0 + ck]), _dot(h, win_ref[:, d_ff + c0:d_ff + c0 + ck])

    jobs = [(r0, c0) for r0 in subs for c0 in chunks]
    pending = gate_up(*jobs[0])
    acc = None
    for n, (r0, c0) in enumerate(jobs):
        gate, up = pending
        if n + 1 < len(jobs):
            pending = gate_up(*jobs[n + 1])
        act = (gate * _sigmoid(gate) * up).astype(BF16)
        y = _dot(act, wout_ref[c0:c0 + ck, :])
        acc = y if c0 == 0 else acc + y
        if c0 == chunks[-1]:
            o_ref[r0:r0 + rows, :] = xs[r0] + (0.5 * mod_ref[2:3, :]) * acc


def _ffn_chunk(d_ff):
    for n in (2, 1, 4, 11, 22):
        if d_ff % n == 0 and (d_ff // n) % LANES == 0:
            return d_ff // n
    return d_ff


def _ffn(x2d, mod3, g, w_in, w_out, *, tm, tiles_per_batch, mix=None):
    N, D = x2d.shape
    d_ff = w_out.shape[0]
    tile = pl.BlockSpec((tm, D), lambda i: (i, 0))
    modspec = pl.BlockSpec((None, 3, D), lambda i: (i // tiles_per_batch, 0, 0))
    ins = [x2d, mod3, g.reshape(1, D), w_in, w_out]
    specs = [tile, modspec, _const_spec((1, D)), _const_spec(w_in.shape), _const_spec(w_out.shape)]
    mixer = None
    if mix is not None:
        y, y_gate, mixmod, w_o = mix
        mixer = "plain" if y_gate is None else "gated"
        ins += [y] + ([] if y_gate is None else [y_gate]) + [mixmod, w_o]
        specs += [tile] + ([] if y_gate is None else [tile]) + [modspec, _const_spec(w_o.shape)]
    kern = functools.partial(_ffn_kernel, d_ff=d_ff, ck=_ffn_chunk(d_ff), rows=min(tm, FFN_ROWS), mixer=mixer)
    return pl.pallas_call(
        kern,
        grid=(N // tm,),
        in_specs=specs,
        out_specs=tile,
        out_shape=jax.ShapeDtypeStruct((N, D), F32),
        compiler_params=_cparams("parallel"),
        name="ffn_half" if mixer is None else "ffn_half_" + mixer,
    )(*ins)


def _group_rms_scale(y, gsum, gexp):
    ss = _dot((y * y).astype(BF16), gsum)
    return _group_bcast(lax.rsqrt(ss * (1.0 / HEAD_DIM) + NORM_EPS), gexp)


def _da_qkv_kernel(x_ref, mod_ref, g_ref, w_ref, gsum_ref, gexp_ref, qg_ref, kg_ref, q_ref, k_ref, vt_ref):
    D = x_ref.shape[1]
    for rows in _sub_tiles(x_ref.shape[0]):
        h = _norm_mod(x_ref[rows, :], g_ref[...], mod_ref[0:1, :], mod_ref[1:2, :]).astype(BF16)
        q = _dot(h, w_ref[:, 0:D])
        k = _dot(h, w_ref[:, D:2 * D])
        v = _dot(h, w_ref[:, 2 * D:3 * D])
        vt_ref[:, rows] = v.T.astype(BF16)
        q_ref[rows, :] = (q * _group_rms_scale(q, gsum_ref[...], gexp_ref[...]) * qg_ref[...]).astype(BF16)
        k_ref[rows, :] = (k * _group_rms_scale(k, gsum_ref[...], gexp_ref[...]) * kg_ref[...]).astype(BF16)


def _da_qkv(x2d, mod3, g, w_qkv, q_gain, k_gain, *, B, T, tm):
    N, D = x2d.shape
    tpb = T // tm
    gsum, gexp = _group_mats(D)
    tile = pl.BlockSpec((tm, D), lambda i: (i, 0))
    return pl.pallas_call(
        _da_qkv_kernel,
        grid=(N // tm,),
        in_specs=[tile,
                  pl.BlockSpec((None, 3, D), lambda i: (i // tpb, 0, 0)),
                  _const_spec((1, D)), _const_spec(w_qkv.shape), _const_spec(gsum.shape),
                  _const_spec(gexp.shape), _const_spec((1, D)), _const_spec((1, D))],
        out_specs=[tile, tile, pl.BlockSpec((None, D, tm), lambda i: (i // tpb, 0, i % tpb))],
        out_shape=[jax.ShapeDtypeStruct((N, D), BF16), jax.ShapeDtypeStruct((N, D), BF16),
                   jax.ShapeDtypeStruct((B, D, T), BF16)],
        compiler_params=_cparams("parallel"),
        name="da_qkv",
    )(x2d, mod3, g.reshape(1, D), w_qkv, gsum, gexp, q_gain, k_gain)


def _da_attn_kernel(lam_ref, sg_ref, q_ref, k_ref, vt_ref, o_ref, acc_ref, sa_ref, sb_ref, *, tq):
    nq = q_ref.shape[0] // tq
    lane = lax.broadcasted_iota(jnp.int32, (1, LANES), 1)
    lam = lam_ref[...]
    lambda_init = lam[4:5, 0:1]
    lam_full = (jnp.exp(jnp.sum(lam[0:1] * lam[1:2], axis=1, keepdims=True))
                - jnp.exp(jnp.sum(lam[2:3] * lam[3:4], axis=1, keepdims=True)) + lambda_init)
    kpos = lax.broadcasted_iota(jnp.int32, (tq, tq), 0)
    qpos = lax.broadcasted_iota(jnp.int32, (tq, tq), 1)
    causal = kpos <= qpos
    bufs = (sa_ref, sb_ref)
    q_maps = {}

    def q_masked(qi):
        if qi not in q_maps:
            q = q_ref[qi * tq:(qi + 1) * tq, :]
            zero = jnp.zeros_like(q)
            q_maps[qi] = (jnp.where(lane < HEAD_DIM, q, zero), jnp.where(lane >= HEAD_DIM, q, zero))
        return q_maps[qi]

    def scores_into(s_ref, qi, j):
        k = k_ref[j * tq:(j + 1) * tq, :]
        for c in range(2):
            s_ref[c] = _dot_nt(k, q_masked(qi)[c])

    def consume(s_ref, acc, qi, j, carry):
        vt = vt_ref[:, j * tq:(j + 1) * tq]
        out = []
        for c in range(2):
            s = s_ref[c]
            if j == qi:
                s = jnp.where(causal, s, -jnp.inf)
            m_cur = jnp.max(s, axis=0, keepdims=True)
            if j == 0:
                m_new = m_cur
                p = jnp.exp2(s - m_new)
                l_new = jnp.sum(p, axis=0, keepdims=True)
                acc[c] = _dot(vt, p.astype(BF16))
            else:
                m_prev, l_prev = carry[2 * c], carry[2 * c + 1]
                m_new = jnp.maximum(m_prev, m_cur)
                alpha = jnp.exp2(m_prev - m_new)
                p = jnp.exp2(s - m_new)
                l_new = alpha * l_prev + jnp.sum(p, axis=0, keepdims=True)
                acc[c] = alpha * acc[c] + _dot(vt, p.astype(BF16))
            out += [m_new, l_new]
        return tuple(out)

    def finish(acc, qi, carry):
        ot = acc[0] / carry[1] - lam_full * (acc[1] / carry[3])
        ms = jnp.mean(ot * ot, axis=0, keepdims=True)
        ot = ot * lax.rsqrt(ms + SUBLN_EPS)
        o_ref[qi * tq:(qi + 1) * tq, :] = (ot.T * sg_ref[...] * (1.0 - lambda_init)).astype(BF16)

    tasks = [(qi, j) for qi in range(nq) for j in range(qi + 1)]
    scores_into(bufs[0], *tasks[0])
    carry = None
    for t, (qi, j) in enumerate(tasks):
        if t + 1 < len(tasks):
            scores_into(bufs[(t + 1) % 2], *tasks[t + 1])
        acc = acc_ref.at[qi % 2]
        carry = consume(bufs[t % 2], acc, qi, j, carry)
        if j == qi:
            finish(acc, qi, carry)


def _da_attn(q, k, vt, lambdas, subln_g, *, B, T, D, tq, lambda_init):
    H = D // LANES
    q3, k3 = q.reshape(B, T, D), k.reshape(B, T, D)
    seq = pl.BlockSpec((None, T, LANES), lambda b, h: (b, 0, h))
    out = pl.pallas_call(
        functools.partial(_da_attn_kernel, tq=tq),
        grid=(B, H),
        in_specs=[pl.BlockSpec((5, HEAD_DIM), lambda b, h: (0, 0)),
                  pl.BlockSpec((1, LANES), lambda b, h: (0, 0)),
                  seq, seq,
                  pl.BlockSpec((None, LANES, T), lambda b, h: (b, h, 0))],
        out_specs=seq,
        out_shape=jax.ShapeDtypeStruct((B, T, D), BF16),
        scratch_shapes=[pltpu.VMEM((2, 2, LANES, tq), F32),
                        pltpu.VMEM((2, tq, tq), F32), pltpu.VMEM((2, tq, tq), F32)],
        compiler_params=_cparams("parallel", "parallel"),
        name="da_attn",
    )(jnp.concatenate([lambdas, jnp.full((1, HEAD_DIM), lambda_init, F32)], axis=0),
      subln_g.reshape(1, LANES), q3, k3, vt)
    return out.reshape(B * T, D)


def _sw_qkv_kernel(x_ref, mod_ref, g_ref, w_ref, gsum_ref, gexp_ref, ksum_ref, kexp_ref, qg_ref, kg_ref,
                   q_ref, k_ref, v_ref):
    D = x_ref.shape[1]
    kv = k_ref.shape[1]
    for rows in _sub_tiles(x_ref.shape[0]):
        h = _norm_mod(x_ref[rows, :], g_ref[...], mod_ref[0:1, :], mod_ref[1:2, :]).astype(BF16)
        q = _dot(h, w_ref[:, 0:D])
        k = _dot(h, w_ref[:, D:D + kv])
        v_ref[rows, :] = _dot(h, w_ref[:, D + kv:D + 2 * kv]).astype(BF16)
        q_ref[rows, :] = (q * _group_rms_scale(q, gsum_ref[...], gexp_ref[...]) * qg_ref[...]).astype(BF16)
        k_ref[rows, :] = (k * _group_rms_scale(k, ksum_ref[...], kexp_ref[...]) * kg_ref[...]).astype(BF16)


def _sw_qkv(x2d, mod3, g, w_qkv, q_gain, k_gain, *, tm, tiles_per_batch):
    N, D = x2d.shape
    kv = (w_qkv.shape[1] - D) // 2
    gsum, gexp = _group_mats(D)
    ksum, kexp = _group_mats(kv)
    tile = pl.BlockSpec((tm, D), lambda i: (i, 0))
    kvtile = pl.BlockSpec((tm, kv), lambda i: (i, 0))
    return pl.pallas_call(
        _sw_qkv_kernel,
        grid=(N // tm,),
        in_specs=[tile, pl.BlockSpec((None, 3, D), lambda i: (i // tiles_per_batch, 0, 0)),
                  _const_spec((1, D)), _const_spec(w_qkv.shape), _const_spec(gsum.shape),
                  _const_spec(gexp.shape), _const_spec(ksum.shape), _const_spec(kexp.shape),
                  _const_spec((1, D)), _const_spec((1, kv))],
        out_specs=[tile, kvtile, kvtile],
        out_shape=[jax.ShapeDtypeStruct((N, D), BF16), jax.ShapeDtypeStruct((N, kv), BF16),
                   jax.ShapeDtypeStruct((N, kv), BF16)],
        compiler_params=_cparams("parallel"),
        name="sw_qkv",
    )(x2d, mod3, g.reshape(1, D), w_qkv, gsum, gexp, ksum, kexp, q_gain, k_gain)


def _sw_attn_kernel(sink_ref, q_ref, k_ref, v_ref, x_ref, mod_ref, wo_ref, o_ref, att_ref, *, tq, group):
    i = pl.program_id(1)
    D = q_ref.shape[1]
    W = SW_WINDOW
    n_kv = (D // HEAD_DIM) // group
    tiles_per_kv = group // 2
    lane = lax.broadcasted_iota(jnp.int32, (1, LANES), 1)
    first = lane < HEAD_DIM
    jobs = [(sub, kvh) for sub in range(tq // W) for kvh in range(n_kv)]

    def band_start(sub):
        return pl.multiple_of(jnp.maximum(i * tq + (sub - 1) * W, 0), W)

    def scores(sub, kvh):
        k = k_ref[pl.ds(band_start(sub), 2 * W), kvh * LANES:(kvh + 1) * LANES]
        parts = []
        for b in range(tiles_per_kv):
            blk = kvh * tiles_per_kv + b
            q = q_ref[sub * W:(sub + 1) * W, blk * LANES:(blk + 1) * LANES]
            zero = jnp.zeros_like(q)
            parts += [jnp.where(first, q, zero), jnp.where(first, zero, q)]
        return _dot_nt(jnp.concatenate(parts, axis=0), k)

    pending = scores(*jobs[0])
    for n, (sub, kvh) in enumerate(jobs):
        s_all = pending
        if n + 1 < len(jobs):
            pending = scores(*jobs[n + 1])
        qpos = i * tq + sub * W + lax.broadcasted_iota(jnp.int32, (W, 2 * W), 0)
        kpos = band_start(sub) + lax.broadcasted_iota(jnp.int32, (W, 2 * W), 1)
        valid = jnp.logical_and(kpos <= qpos, kpos > qpos - W)
        probs, inv = [], []
        for g in range(group):
            s = jnp.where(valid, s_all[g * W:(g + 1) * W], -jnp.inf)
            snk = sink_ref[kvh * group + g] * math.log2(math.e)
            m = jnp.maximum(jnp.max(s, axis=1, keepdims=True), snk)
            p = jnp.exp2(s - m)
            inv.append(1.0 / (jnp.sum(p, axis=1, keepdims=True) + jnp.exp2(snk - m)))
            probs.append(p.astype(BF16))
        v = v_ref[pl.ds(band_start(sub), 2 * W), kvh * LANES:(kvh + 1) * LANES]
        o_all = _dot(jnp.concatenate(probs, axis=0), v)
        for b in range(tiles_per_kv):
            blk = kvh * tiles_per_kv + b
            o0 = o_all[2 * b * W:(2 * b + 1) * W] * inv[2 * b]
            o1 = o_all[(2 * b + 1) * W:(2 * b + 2) * W] * inv[2 * b + 1]
            att_ref[sub * W:(sub + 1) * W, blk * LANES:(blk + 1) * LANES] = jnp.where(first, o0, o1).astype(BF16)
    o_ref[...] = x_ref[...] + mod_ref[2:3, :] * _dot(att_ref[...], wo_ref[...])


def _sw_attn(q, kd, vd, sinks, x2d, mod3, w_o, *, B, T, tq, group):
    N, D = x2d.shape
    nq = T // tq
    kvd = kd.shape[1]
    tile = pl.BlockSpec((tq, D), lambda b, i: (b * nq + i, 0))
    seq = pl.BlockSpec((None, T, kvd), lambda b, i: (b, 0, 0))
    return pl.pallas_call(
        functools.partial(_sw_attn_kernel, tq=tq, group=group),
        grid=(B, nq),
        in_specs=[pl.BlockSpec(memory_space=pltpu.SMEM), tile, seq, seq, tile,
                  pl.BlockSpec((None, 3, D), lambda b, i: (b, 0, 0)), _const_spec(w_o.shape)],
        out_specs=tile,
        out_shape=jax.ShapeDtypeStruct((N, D), F32),
        scratch_shapes=[pltpu.VMEM((tq, D), BF16)],
        compiler_params=_cparams("parallel", "arbitrary"),
        name="sw_attn",
    )(sinks, q, kd.reshape(B, T, kvd), vd.reshape(B, T, kvd), x2d, mod3, w_o)


def _rw_prep_kernel(x_ref, xp_ref, mod_ref, g_ref, mu_ref, wrkv_ref, w0_ref, w1_ref, w2_ref, a0_ref, a1_ref,
                    a2_ref, g1_ref, g2_ref, kk_ref, ka_ref, gsum_ref, gexp_ref,
                    r_out, lw_out, k_out, v_out, kk_out, b_out, g_out, *, tiles_per_batch):
    shift, scale, gn = mod_ref[0:1, :], mod_ref[1:2, :], g_ref[...]
    for rows in _sub_tiles(x_ref.shape[0]):
        h = _norm_mod(x_ref[rows, :], gn, shift, scale)
        if rows.start == 0:
            hp = _norm_mod(xp_ref[...], gn, shift, scale)[7:8, :]
            hp = jnp.where(pl.program_id(0) % tiles_per_batch == 0, jnp.zeros_like(hp), hp)
        else:
            hp = _norm_mod(x_ref[rows.start - 8:rows.start, :], gn, shift, scale)[7:8, :]
        row = lax.broadcasted_iota(jnp.int32, (rows.stop - rows.start, 1), 0)
        xx = jnp.where(row == 0, hp, pltpu.roll(h, 1, axis=0)) - h

        def mixed(n):
            return (h + xx * mu_ref[n:n + 1, :]).astype(BF16)

        d1 = _dot(mixed(3), w1_ref[...])
        a1 = _dot(mixed(4), a1_ref[...])
        g1 = _dot(mixed(5), g1_ref[...])
        r = _dot(mixed(0), wrkv_ref[0])
        k = _dot(mixed(1), wrkv_ref[1])
        v = _dot(mixed(2), wrkv_ref[2])
        d = _dot(jnp.tanh(d1).astype(BF16), w2_ref[...]) + w0_ref[...]
        a = _sigmoid(a0_ref[...] + _dot(a1.astype(BF16), a2_ref[...]))
        g_out[rows, :] = _dot(_sigmoid(g1).astype(BF16), g2_ref[...]).astype(BF16)
        z = -d
        softplus = jnp.maximum(z, 0.0) + jnp.log(1.0 + jnp.exp(-jnp.abs(z)))
        lw_out[rows, :] = -jnp.exp(-softplus - 0.5)
        kk = k * kk_ref[...]
        ss = _dot((kk * kk).astype(BF16), gsum_ref[...])
        kk = kk * _group_bcast(1.0 / jnp.maximum(jnp.sqrt(ss), 1e-12), gexp_ref[...])
        r_out[rows, :] = r.astype(BF16)
        k_out[rows, :] = (k * (1.0 + (a - 1.0) * ka_ref[...])).astype(BF16)
        v_out[rows, :] = v.astype(BF16)
        kk_out[rows, :] = kk.astype(BF16)
        b_out[rows, :] = (kk * a).astype(BF16)


def _pad_to(w, axis, n):
    pad = [(0, 0)] * w.ndim
    pad[axis] = (0, n - w.shape[axis])
    return jnp.pad(w, pad)


def _round_up(n, m):
    return -(-n // m) * m


def _rw_prep(x2d, mod3, g, p, *, tm, tiles_per_batch):
    N, D = x2d.shape
    gsum, gexp = _group_mats(D)
    row = lambda t: t.reshape(1, D)
    lora = lambda w1, w2: (_pad_to(w1, 1, _round_up(w1.shape[1], LANES)).astype(BF16),
                           _pad_to(w2, 0, _round_up(w2.shape[0], LANES)).astype(BF16))
    w1, w2 = lora(p["decay_w1"], p["decay_w2"])
    a1, a2 = lora(p["iclr_a1"], p["iclr_a2"])
    g1, g2 = lora(p["gate_g1"], p["gate_g2"])
    consts = [row(g), p["mu"], p["w_rkv"].astype(BF16), row(p["decay_w0"]), w1, w2, row(p["iclr_a0"]), a1, a2,
              g1, g2, row(p["k_k"]), row(p["k_a"]), gsum, gexp]
    tile = pl.BlockSpec((tm, D), lambda i: (i, 0))
    outs = [jax.ShapeDtypeStruct((N, D), BF16)] * 7
    outs[1] = jax.ShapeDtypeStruct((N, D), F32)
    return pl.pallas_call(
        functools.partial(_rw_prep_kernel, tiles_per_batch=tiles_per_batch),
        grid=(N // tm,),
        in_specs=[tile,
                  pl.BlockSpec((8, D), lambda i: (jnp.maximum(i * (tm // 8) - 1, 0), 0)),
                  pl.BlockSpec((None, 3, D), lambda i: (i // tiles_per_batch, 0, 0))]
                 + [_const_spec(c.shape) for c in consts],
        out_specs=[tile] * 7,
        out_shape=outs,
        compiler_params=_cparams("parallel"),
        name="rw_prep",
    )(x2d, x2d, mod3, *consts)


def _stack_heads(x, m0):
    z = jnp.zeros_like(x)
    return jnp.concatenate([jnp.where(m0, x, z), jnp.where(m0, z, x)], axis=0)


def _rw_scan_kernel(r_ref, lw_ref, k_ref, v_ref, kk_ref, b_ref, rk_ref, lng_ref, lnb_ref, o_ref, st_ref, os_ref,
                    *, tt, group):
    C = RW_CHUNK
    C2 = 2 * C

    @pl.when(pl.program_id(2) == 0)
    def _():
        st_ref[...] = jnp.zeros_like(st_ref)

    lane = lax.broadcasted_iota(jnp.int32, (1, LANES), 1)
    m0 = lane < HEAD_DIM
    ti = lax.broadcasted_iota(jnp.int32, (C, C), 0)
    tj = lax.broadcasted_iota(jnp.int32, (C, C), 1)
    tri = (ti >= tj).astype(F32)
    ri = lax.broadcasted_iota(jnp.int32, (2 * C2, 2 * C2), 0)
    ci = lax.broadcasted_iota(jnp.int32, (2 * C2, 2 * C2), 1)
    rt, ct = ri % C, ci % C
    causal = rt - ct + (ri >= C2).astype(jnp.int32) > 0
    eye = (lax.broadcasted_iota(jnp.int32, (C2, C2), 0) == lax.broadcasted_iota(jnp.int32, (C2, C2), 1))
    eye_f = eye.astype(F32)
    si = lax.broadcasted_iota(jnp.int32, (C2, C2), 0)
    sj = lax.broadcasted_iota(jnp.int32, (C2, C2), 1)
    merge_masks = []
    s = 1
    while s < C:
        merge_masks.append(jnp.logical_and(si // (2 * s) == sj // (2 * s),
                                           jnp.logical_and(si % (2 * s) >= s, sj % (2 * s) < s)))
        s *= 2
    zeros_sq = jnp.zeros((C2, C2), BF16)

    def chunk_terms(chunks):
        n = range(len(chunks))
        a_t, r_t, v_s, aa, lhs_top, p_col = [], [], [], [], [], []
        for i, c in enumerate(chunks):
            rows = slice(c * C, (c + 1) * C)
            lw = lw_ref[rows, :]
            r = r_ref[rows, :].astype(F32)
            k = k_ref[rows, :].astype(F32)
            kk = kk_ref[rows, :].astype(F32)
            b = b_ref[rows, :].astype(F32)
            cum = jnp.dot(tri, lw, precision=HIGHEST, preferred_element_type=F32)
            tot = cum[C - 1:C, :]
            e_neg = jnp.exp(-cum)
            r_t.append(_stack_heads(r * jnp.exp(cum), m0).astype(BF16))
            a_t.append(_stack_heads(-kk * jnp.exp(cum - lw), m0).astype(BF16))
            b_t = _stack_heads(b * e_neg, m0).astype(BF16)
            k_t = _stack_heads(k * e_neg, m0).astype(BF16)
            e_end = jnp.exp(tot - cum)
            b_h = _stack_heads(b * e_end, m0)
            k_h = _stack_heads(k * e_end, m0)
            lhs_top.append(jnp.concatenate([b_h.T, k_h.T], axis=1).astype(BF16))
            v_s.append(_stack_heads(v_ref[rows, :], m0))
            p_col.append(jnp.sum(jnp.where(eye, jnp.exp(tot), 0.0), axis=1, keepdims=True))
            full = _dot_nt(jnp.concatenate([a_t[i], r_t[i]], axis=0), jnp.concatenate([b_t, k_t], axis=0))
            aa.append(jnp.where(causal, full, 0.0))
        yield

        a_ab = [aa[i][:C2, :C2] for i in n]
        t_inv = [eye_f + jnp.where(merge_masks[0], a_ab[i], 0.0) for i in n]
        for mask in merge_masks[1:]:
            tb = [t_inv[i].astype(BF16) for i in n]
            half = [_dot(tb[i], jnp.where(mask, a_ab[i], 0.0).astype(BF16)).astype(BF16) for i in n]
            yield
            t_inv = [t_inv[i] + _dot(half[i], tb[i]) for i in n]
            yield
        akv = [_dot(aa[i][:C2, C2:].astype(BF16), v_s[i]).astype(BF16) for i in n]
        yield
        au = [_dot(t_inv[i].astype(BF16), jnp.concatenate([a_t[i], akv[i]], axis=1)).astype(BF16)
              for i in n]
        yield
        big = []
        for i in n:
            rhs = jnp.concatenate([au[i], jnp.concatenate([zeros_sq, v_s[i]], axis=1)], axis=0)
            lhs = jnp.concatenate([lhs_top[i], aa[i][C2:, :].astype(BF16)], axis=0)
            big.append(_dot(lhs, rhs))
        qg = [jnp.concatenate([big[i][C2:, :C2] + r_t[i].astype(F32), big[i][:C2, :C2]], axis=0).astype(BF16)
              for i in n]
        return qg, big, p_col

    blk2 = jnp.tile(((lax.broadcasted_iota(jnp.int32, (LANES, LANES), 0) < HEAD_DIM)
                     == (lax.broadcasted_iota(jnp.int32, (LANES, LANES), 1) < HEAD_DIM)).astype(BF16), (2, 1))

    def head_sum(x):
        hi = x.astype(BF16)
        lo = (x - hi.astype(F32)).astype(BF16)
        return _dot(jnp.concatenate([hi, lo], axis=1), blk2)

    def state_steps(st, chunks, terms):
        qg, big, p_col = terms
        for i, c in enumerate(chunks):
            res = _dot(qg[i], st[0].astype(BF16))
            o_st = res[:C2] + big[i][C2:, C2:]
            os_ref[c * C:(c + 1) * C, :] = o_st[:C] + o_st[C:]
            st[0] = p_col[i] * st[0] + res[C2:] + big[i][:C2, C2:]
            yield
        rows = slice(chunks[0] * C, (chunks[-1] + 1) * C)
        o = os_ref[rows, :]
        mean = head_sum(o) * (1.0 / HEAD_DIM)
        cen = o - mean
        var = head_sum(cen * cen) * (1.0 / HEAD_DIM)
        y = cen * lax.rsqrt(var + RW_LNX_EPS) * lng_ref[...] + lnb_ref[...]
        r = r_ref[rows, :].astype(F32)
        k = k_ref[rows, :].astype(F32)
        bonus = head_sum(r * k * rk_ref[...]) * v_ref[rows, :].astype(F32)
        o_ref[rows, :] = (y + bonus).astype(BF16)
        yield

    st = [st_ref[...]]
    n_chunks = tt // C
    sizes = [group] * (n_chunks // group) + ([n_chunks % group] if n_chunks % group else [])
    if sizes[-1] > 1:
        sizes[-1:] = [sizes[-1] - sizes[-1] // 2, sizes[-1] // 2]
    pending = iter(())
    g0 = 0
    for size in sizes:
        chunks = list(range(g0, g0 + size))
        g0 += size
        gen = chunk_terms(chunks)
        while True:
            try:
                next(gen)
            except StopIteration as done:
                terms = done.value
                break
            next(pending, None)
            next(pending, None)
        for _ in pending:
            pass
        pending = state_steps(st, chunks, terms)
    for _ in pending:
        pass
    st_ref[...] = st[0]


def _rw_scan(r, lw, k, v, kk, b, r_k, lnx_g, lnx_b, *, B, T, D, tt):
    hp = D // LANES
    blk = pl.BlockSpec((None, tt, LANES), lambda bi, h, t: (bi, t, h))
    vec = pl.BlockSpec((1, LANES), lambda bi, h, t: (0, h))
    r3 = lambda t: t.reshape(B, T, D)
    out = pl.pallas_call(
        functools.partial(_rw_scan_kernel, tt=tt, group=RW_GROUP),
        grid=(B, hp, T // tt),
        in_specs=[blk] * 6 + [vec] * 3,
        out_specs=blk,
        out_shape=jax.ShapeDtypeStruct((B, T, D), BF16),
        scratch_shapes=[pltpu.VMEM((LANES, LANES), F32), pltpu.VMEM((tt, LANES), F32)],
        compiler_params=_cparams("parallel", "parallel", "arbitrary"),
        name="rw_scan",
    )(r3(r), r3(lw), r3(k), r3(v), r3(kk), r3(b), r_k.reshape(1, D), lnx_g.reshape(1, D), lnx_b.reshape(1, D))
    return out.reshape(B * T, D)


def _diff_lambda_init(layer):
    return 0.8 - 0.6 * math.exp(-0.3 * layer)


def _pick_tile(n, pref):
    t = min(n, pref)
    while n % t:
        t //= 2
    return t


def kernel(x, c, norm_g, ada_w, ada_b, ffn_w_in, ffn_w_out, da_w_qkv, da_w_o, da_q_norm_g, da_k_norm_g, da_lambda, da_subln_g, rw_mu, rw_w_rkv, rw_w_o, rw_decay_w0, rw_decay_w1, rw_decay_w2, rw_iclr_a0, rw_iclr_a1, rw_iclr_a2, rw_gate_g1, rw_gate_g2, rw_k_k, rw_k_a, rw_r_k, rw_lnx_g, rw_lnx_b, sw_w_qkv, sw_w_o, sw_q_norm_g, sw_k_norm_g, sw_sinks):
    B, T, D = x.shape
    L = norm_g.shape[0]
    tm = _pick_tile(T, 2 * PROJ_ROWS)
    tpb = T // tm
    kw = dict(tm=tm, tiles_per_batch=tpb)
    tf = _pick_tile(T, 2 * FFN_ROWS)
    fkw = dict(tm=tf, tiles_per_batch=T // tf)
    scale = HEAD_DIM ** -0.5

    mod = _ada(c, ada_w, ada_b).reshape(L, B, 3, 3, D)
    x2d = x.reshape(B * T, D)
    for i in range(L):
        kind, j = i % 3, i // 3
        x2d = _ffn(x2d, mod[i, :, 0], norm_g[i, 0], ffn_w_in[i, 0].astype(BF16), ffn_w_out[i, 0].astype(BF16), **fkw)
        m2 = mod[i, :, 1]
        if kind == 0:
            qg = jnp.tile(da_q_norm_g[j] * (scale * math.log2(math.e)), D // HEAD_DIM).reshape(1, D)
            kg = jnp.tile(da_k_norm_g[j], D // HEAD_DIM).reshape(1, D)
            q, k, vt = _da_qkv(x2d, m2, norm_g[i, 1], da_w_qkv[j].astype(BF16), qg, kg, B=B, T=T, tm=tm)
            o = _da_attn(q, k, vt, da_lambda[j], da_subln_g[j], B=B, T=T, D=D, tq=_pick_tile(T, 512),
                         lambda_init=_diff_lambda_init(i))
            mix = (o, None, m2, da_w_o[j].astype(BF16))
        elif kind == 1:
            p = dict(mu=rw_mu[j], w_rkv=rw_w_rkv[j], decay_w0=rw_decay_w0[j], decay_w1=rw_decay_w1[j],
                     decay_w2=rw_decay_w2[j], iclr_a0=rw_iclr_a0[j], iclr_a1=rw_iclr_a1[j], iclr_a2=rw_iclr_a2[j],
                     gate_g1=rw_gate_g1[j], gate_g2=rw_gate_g2[j], k_k=rw_k_k[j], k_a=rw_k_a[j])
            r, lw, k, v, kk, b, g = _rw_prep(x2d, m2, norm_g[i, 1], p, **kw)
            y = _rw_scan(r, lw, k, v, kk, b, rw_r_k[j], rw_lnx_g[j], rw_lnx_b[j], B=B, T=T, D=D,
                         tt=_pick_tile(T, 4096))
            mix = (y, g, m2, rw_w_o[j].astype(BF16))
        else:
            kv = (sw_w_qkv.shape[-1] - D) // 2
            n_kv = kv // HEAD_DIM
            qg = jnp.tile(sw_q_norm_g[j] * (scale * math.log2(math.e)), D // HEAD_DIM).reshape(1, D)
            dup = lambda w: jnp.repeat(w.reshape(D, n_kv, 1, HEAD_DIM), 2, axis=2).reshape(D, 2 * kv)
            w = sw_w_qkv[j]
            w = jnp.concatenate([w[:, :D], dup(w[:, D:D + kv]), dup(w[:, D + kv:])], axis=1).astype(BF16)
            kg = jnp.tile(sw_k_norm_g[j], 2 * n_kv).reshape(1, 2 * kv)
            q, k, v = _sw_qkv(x2d, m2, norm_g[i, 1], w, qg, kg, **kw)
            x2d = _sw_attn(q, k, v, sw_sinks[j], x2d, m2, sw_w_o[j].astype(BF16), B=B, T=T,
                           tq=_pick_tile(T, 512), group=(D // HEAD_DIM) // n_kv)
            mix = None
        x2d = _ffn(x2d, mod[i, :, 2], norm_g[i, 2], ffn_w_in[i, 1].astype(BF16), ffn_w_out[i, 1].astype(BF16),
                   mix=mix, **fkw)
    return x2d.reshape(B, T, D)
```

```python
import functools
import math

import jax
import jax.numpy as jnp
from jax import lax
from jax.experimental import pallas as pl
from jax.experimental.pallas import tpu as pltpu

F32 = jnp.float32
BF16 = jnp.bfloat16
HIGHEST = lax.Precision.HIGHEST

HEAD_DIM = 64
LANES = 128
NORM_EPS = 1e-6
SUBLN_EPS = 1e-5
RW_LNX_EPS = 64e-5
SW_WINDOW = 128
N_MOD = 9
PROJ_ROWS = 512
FFN_ROWS = 512
RW_GROUP = 16
RW_CHUNK = 64
VMEM_LIMIT = 56 * 1024 * 1024


def _cparams(*sem):
    return pltpu.CompilerParams(dimension_semantics=sem, vmem_limit_bytes=VMEM_LIMIT)


def _sigmoid(x):
    return 1.0 / (1.0 + jnp.exp(-x))


def _dot(a, b):
    return jnp.dot(a, b, preferred_element_type=F32)


def _dot_nt(a, b):
    return lax.dot_general(a, b, (((1,), (1,)), ((), ())), preferred_element_type=F32)


def _norm_mod(x, g, shift, scale):
    ms = jnp.mean(x * x, axis=-1, keepdims=True)
    return (x * lax.rsqrt(ms + NORM_EPS) * g) * (1.0 + scale) + shift


def _sub_tiles(n):
    step = min(n, PROJ_ROWS)
    return [slice(r0, r0 + step) for r0 in range(0, n, step)]


def _const_spec(shape):
    nd = len(shape)
    return pl.BlockSpec(shape, lambda *_: (0,) * nd, pipeline_mode=pl.Buffered(1))


def _group_mats(d):
    grp = jnp.arange(d) // HEAD_DIM
    gsum = (grp[:, None] == jnp.arange(LANES)[None, :]).astype(BF16)
    return gsum, jnp.concatenate([gsum.T, gsum.T], axis=0)


def _group_bcast(v, gexp):
    hi = v.astype(BF16)
    lo = (v - hi.astype(F32)).astype(BF16)
    return _dot(jnp.concatenate([hi, lo], axis=1), gexp)


def _ada_kernel(c_ref, w_ref, b_ref, o_ref):
    c = c_ref[...]
    cond = c * _sigmoid(c)
    o_ref[...] = jnp.dot(cond, w_ref[...], precision=HIGHEST, preferred_element_type=F32) + b_ref[...]


def _ada(c, ada_w, ada_b):
    L, D, ND = ada_w.shape
    B = c.shape[0]
    return pl.pallas_call(
        _ada_kernel,
        grid=(L, ND // D),
        in_specs=[pl.BlockSpec((B, D), lambda l, j: (0, 0)),
                  pl.BlockSpec((None, D, D), lambda l, j: (l, 0, j)),
                  pl.BlockSpec((None, 1, D), lambda l, j: (l, 0, j))],
        out_specs=pl.BlockSpec((None, B, D), lambda l, j: (l, 0, j)),
        out_shape=jax.ShapeDtypeStruct((L, B, ND), F32),
        compiler_params=_cparams("parallel", "parallel"),
        name="ada_mod",
    )(c, ada_w, ada_b.reshape(L, 1, ND))


def _ffn_kernel(*refs, d_ff, chunks, rows, mixer):
    if mixer is None:
        x_ref, mod_ref, g_ref, win_ref, wout_ref, o_ref = refs
    elif mixer == "plain":
        x_ref, mod_ref, g_ref, win_ref, wout_ref, y_ref, mixmod_ref, wo_ref, o_ref = refs
    else:
        x_ref, mod_ref, g_ref, win_ref, wout_ref, y_ref, ygate_ref, mixmod_ref, wo_ref, o_ref = refs
    tm = x_ref.shape[0]
    subs = list(range(0, tm, rows))

    def residual_in(r0):
        x = x_ref[r0:r0 + rows, :]
        if mixer is None:
            return x
        y = y_ref[r0:r0 + rows, :]
        if mixer == "gated":
            y = (y.astype(F32) * ygate_ref[r0:r0 + rows, :].astype(F32)).astype(BF16)
        return x + mixmod_ref[2:3, :] * _dot(y, wo_ref[...])

    xs, hidden = {}, {}

    def prepare(r0):
        xs[r0] = residual_in(r0)
        hidden[r0] = _norm_mod(xs[r0], g_ref[...], mod_ref[0:1, :], mod_ref[1:2, :]).astype(BF16)

    def gate_up(r0, c0, c1):
        h = hidden[r0]
        return _dot(h, win_ref[:, c0:c1]), _dot(h, win_ref[:, d_ff + c0:d_ff + c1])

    jobs = [(r0, c0, c1) for r0 in subs for c0, c1 in chunks]
    prepare(subs[0])
    pending = gate_up(*jobs[0])
    acc = None
    for n, (r0, c0, c1) in enumerate(jobs):
        gate, up = pending
        if n + 1 < len(jobs):
            if jobs[n + 1][0] not in hidden:
                prepare(jobs[n + 1][0])
            pending = gate_up(*jobs[n + 1])
        if n == 0:
            for r in subs[1:]:
                if r not in hidden:
                    prepare(r)
        act = (gate * _sigmoid(gate) * up).astype(BF16)
        y = _dot(act, wout_ref[c0:c1, :])
        acc = y if c0 == 0 else acc + y
        if c1 == d_ff:
            o_ref[r0:r0 + rows, :] = xs[r0] + (0.5 * mod_ref[2:3, :]) * acc


def _ffn_chunks(d_ff):
    last = 2 * LANES
    if d_ff % LANES or d_ff < 4 * last:
        return [(0, d_ff)]
    mid = (d_ff // 2) // LANES * LANES
    return [(0, mid), (mid, d_ff - last), (d_ff - last, d_ff)]


def _ffn(x2d, mod3, g, w_in, w_out, *, tm, tiles_per_batch, mix=None):
    N, D = x2d.shape
    d_ff = w_out.shape[0]
    tile = pl.BlockSpec((tm, D), lambda i: (i, 0))
    modspec = pl.BlockSpec((None, 3, D), lambda i: (i // tiles_per_batch, 0, 0))
    ins = [x2d, mod3, g.reshape(1, D), w_in, w_out]
    specs = [tile, modspec, _const_spec((1, D)), _const_spec(w_in.shape), _const_spec(w_out.shape)]
    mixer = None
    if mix is not None:
        y, y_gate, mixmod, w_o = mix
        mixer = "plain" if y_gate is None else "gated"
        ins += [y] + ([] if y_gate is None else [y_gate]) + [mixmod, w_o]
        specs += [tile] + ([] if y_gate is None else [tile]) + [modspec, _const_spec(w_o.shape)]
    kern = functools.partial(_ffn_kernel, d_ff=d_ff, chunks=_ffn_chunks(d_ff), rows=min(tm, FFN_ROWS), mixer=mixer)
    return pl.pallas_call(
        kern,
        grid=(N // tm,),
        in_specs=specs,
        out_specs=tile,
        out_shape=jax.ShapeDtypeStruct((N, D), F32),
        compiler_params=_cparams("parallel"),
        name="ffn_half" if mixer is None else "ffn_half_" + mixer,
    )(*ins)


def _group_rms_scale(y, gsum, gexp):
    ss = _dot((y * y).astype(BF16), gsum)
    return _group_bcast(lax.rsqrt(ss * (1.0 / HEAD_DIM) + NORM_EPS), gexp)


def _da_qkv_kernel(x_ref, mod_ref, g_ref, w_ref, gsum_ref, gexp_ref, qg_ref, kg_ref, q_ref, k_ref, vt_ref):
    D = x_ref.shape[1]
    for rows in _sub_tiles(x_ref.shape[0]):
        h = _norm_mod(x_ref[rows, :], g_ref[...], mod_ref[0:1, :], mod_ref[1:2, :]).astype(BF16)
        q = _dot(h, w_ref[:, 0:D])
        k = _dot(h, w_ref[:, D:2 * D])
        v = _dot(h, w_ref[:, 2 * D:3 * D])
        vt_ref[:, rows] = v.T.astype(BF16)
        q_ref[rows, :] = (q * _group_rms_scale(q, gsum_ref[...], gexp_ref[...]) * qg_ref[...]).astype(BF16)
        k_ref[rows, :] = (k * _group_rms_scale(k, gsum_ref[...], gexp_ref[...]) * kg_ref[...]).astype(BF16)


def _da_qkv(x2d, mod3, g, w_qkv, q_gain, k_gain, *, B, T, tm):
    N, D = x2d.shape
    tpb = T // tm
    gsum, gexp = _group_mats(D)
    tile = pl.BlockSpec((tm, D), lambda i: (i, 0))
    return pl.pallas_call(
        _da_qkv_kernel,
        grid=(N // tm,),
        in_specs=[tile,
                  pl.BlockSpec((None, 3, D), lambda i: (i // tpb, 0, 0)),
                  _const_spec((1, D)), _const_spec(w_qkv.shape), _const_spec(gsum.shape),
                  _const_spec(gexp.shape), _const_spec((1, D)), _const_spec((1, D))],
        out_specs=[tile, tile, pl.BlockSpec((None, D, tm), lambda i: (i // tpb, 0, i % tpb))],
        out_shape=[jax.ShapeDtypeStruct((N, D), BF16), jax.ShapeDtypeStruct((N, D), BF16),
                   jax.ShapeDtypeStruct((B, D, T), BF16)],
        compiler_params=_cparams("parallel"),
        name="da_qkv",
    )(x2d, mod3, g.reshape(1, D), w_qkv, gsum, gexp, q_gain, k_gain)


def _da_attn_kernel(lam_ref, sg_ref, q_ref, k_ref, vt_ref, o_ref, acc_ref, sa_ref, sb_ref, *, tq):
    nq = q_ref.shape[0] // tq
    lane = lax.broadcasted_iota(jnp.int32, (1, LANES), 1)
    lam = lam_ref[...]
    lambda_init = lam[4:5, 0:1]
    lam_full = (jnp.exp(jnp.sum(lam[0:1] * lam[1:2], axis=1, keepdims=True))
                - jnp.exp(jnp.sum(lam[2:3] * lam[3:4], axis=1, keepdims=True)) + lambda_init)
    kpos = lax.broadcasted_iota(jnp.int32, (tq, tq), 0)
    qpos = lax.broadcasted_iota(jnp.int32, (tq, tq), 1)
    causal = kpos <= qpos
    bufs = (sa_ref, sb_ref)
    q_maps = {}

    def q_masked(qi):
        if qi not in q_maps:
            q = q_ref[qi * tq:(qi + 1) * tq, :]
            zero = jnp.zeros_like(q)
            q_maps[qi] = (jnp.where(lane < HEAD_DIM, q, zero), jnp.where(lane >= HEAD_DIM, q, zero))
        return q_maps[qi]

    def scores_into(s_ref, qi, j):
        k = k_ref[j * tq:(j + 1) * tq, :]
        for c in range(2):
            s_ref[c] = _dot_nt(k, q_masked(qi)[c])

    def consume(s_ref, acc, qi, j, carry):
        vt = vt_ref[:, j * tq:(j + 1) * tq]
        out = []
        for c in range(2):
            s = s_ref[c]
            if j == qi:
                s = jnp.where(causal, s, -jnp.inf)
            m_cur = jnp.max(s, axis=0, keepdims=True)
            if j == 0:
                m_new = m_cur
                p = jnp.exp2(s - m_new)
                l_new = jnp.sum(p, axis=0, keepdims=True)
                acc[c] = _dot(vt, p.astype(BF16))
            else:
                m_prev, l_prev = carry[2 * c], carry[2 * c + 1]
                m_new = jnp.maximum(m_prev, m_cur)
                alpha = jnp.exp2(m_prev - m_new)
                p = jnp.exp2(s - m_new)
                l_new = alpha * l_prev + jnp.sum(p, axis=0, keepdims=True)
                acc[c] = alpha * acc[c] + _dot(vt, p.astype(BF16))
            out += [m_new, l_new]
        return tuple(out)

    def finish(acc, qi, carry):
        ot = acc[0] / carry[1] - lam_full * (acc[1] / carry[3])
        ms = jnp.mean(ot * ot, axis=0, keepdims=True)
        ot = ot * lax.rsqrt(ms + SUBLN_EPS)
        o_ref[qi * tq:(qi + 1) * tq, :] = (ot.T * sg_ref[...] * (1.0 - lambda_init)).astype(BF16)

    tasks = [(qi, j) for qi in range(nq) for j in range(qi + 1)]
    scores_into(bufs[0], *tasks[0])
    carry = None
    for t, (qi, j) in enumerate(tasks):
        if t + 1 < len(tasks):
            scores_into(bufs[(t + 1) % 2], *tasks[t + 1])
        acc = acc_ref.at[qi % 2]
        carry = consume(bufs[t % 2], acc, qi, j, carry)
        if j == qi:
            finish(acc, qi, carry)


def _da_attn(q, k, vt, lambdas, subln_g, *, B, T, D, tq, lambda_init):
    H = D // LANES
    q3, k3 = q.reshape(B, T, D), k.reshape(B, T, D)
    seq = pl.BlockSpec((None, T, LANES), lambda b, h: (b, 0, h))
    out = pl.pallas_call(
        functools.partial(_da_attn_kernel, tq=tq),
        grid=(B, H),
        in_specs=[pl.BlockSpec((5, HEAD_DIM), lambda b, h: (0, 0)),
                  pl.BlockSpec((1, LANES), lambda b, h: (0, 0)),
                  seq, seq,
                  pl.BlockSpec((None, LANES, T), lambda b, h: (b, h, 0))],
        out_specs=seq,
        out_shape=jax.ShapeDtypeStruct((B, T, D), BF16),
        scratch_shapes=[pltpu.VMEM((2, 2, LANES, tq), F32),
                        pltpu.VMEM((2, tq, tq), F32), pltpu.VMEM((2, tq, tq), F32)],
        compiler_params=_cparams("parallel", "parallel"),
        name="da_attn",
    )(jnp.concatenate([lambdas, jnp.full((1, HEAD_DIM), lambda_init, F32)], axis=0),
      subln_g.reshape(1, LANES), q3, k3, vt)
    return out.reshape(B * T, D)


def _sw_qkv_kernel(x_ref, mod_ref, g_ref, w_ref, gsum_ref, gexp_ref, ksum_ref, kexp_ref, qg_ref, kg_ref,
                   q_ref, k_ref, v_ref):
    D = x_ref.shape[1]
    kv = k_ref.shape[1]
    for rows in _sub_tiles(x_ref.shape[0]):
        h = _norm_mod(x_ref[rows, :], g_ref[...], mod_ref[0:1, :], mod_ref[1:2, :]).astype(BF16)
        q = _dot(h, w_ref[:, 0:D])
        k = _dot(h, w_ref[:, D:D + kv])
        v_ref[rows, :] = _dot(h, w_ref[:, D + kv:D + 2 * kv]).astype(BF16)
        q_ref[rows, :] = (q * _group_rms_scale(q, gsum_ref[...], gexp_ref[...]) * qg_ref[...]).astype(BF16)
        k_ref[rows, :] = (k * _group_rms_scale(k, ksum_ref[...], kexp_ref[...]) * kg_ref[...]).astype(BF16)


def _sw_qkv(x2d, mod3, g, w_qkv, q_gain, k_gain, *, tm, tiles_per_batch):
    N, D = x2d.shape
    kv = (w_qkv.shape[1] - D) // 2
    gsum, gexp = _group_mats(D)
    ksum, kexp = _group_mats(kv)
    tile = pl.BlockSpec((tm, D), lambda i: (i, 0))
    kvtile = pl.BlockSpec((tm, kv), lambda i: (i, 0))
    return pl.pallas_call(
        _sw_qkv_kernel,
        grid=(N // tm,),
        in_specs=[tile, pl.BlockSpec((None, 3, D), lambda i: (i // tiles_per_batch, 0, 0)),
                  _const_spec((1, D)), _const_spec(w_qkv.shape), _const_spec(gsum.shape),
                  _const_spec(gexp.shape), _const_spec(ksum.shape), _const_spec(kexp.shape),
                  _const_spec((1, D)), _const_spec((1, kv))],
        out_specs=[tile, kvtile, kvtile],
        out_shape=[jax.ShapeDtypeStruct((N, D), BF16), jax.ShapeDtypeStruct((N, kv), BF16),
                   jax.ShapeDtypeStruct((N, kv), BF16)],
        compiler_params=_cparams("parallel"),
        name="sw_qkv",
    )(x2d, mod3, g.reshape(1, D), w_qkv, gsum, gexp, ksum, kexp, q_gain, k_gain)


def _sw_attn_kernel(sink_ref, q_ref, k_ref, v_ref, x_ref, mod_ref, wo_ref, o_ref, att_ref, *, tq, group):
    i = pl.program_id(1)
    D = q_ref.shape[1]
    W = SW_WINDOW
    n_kv = (D // HEAD_DIM) // group
    tiles_per_kv = group // 2
    lane = lax.broadcasted_iota(jnp.int32, (1, LANES), 1)
    first = lane < HEAD_DIM
    jobs = [(sub, kvh) for sub in range(tq // W) for kvh in range(n_kv)]

    def band_start(sub):
        return pl.multiple_of(jnp.maximum(i * tq + (sub - 1) * W, 0), W)

    def scores(sub, kvh):
        k = k_ref[pl.ds(band_start(sub), 2 * W), kvh * LANES:(kvh + 1) * LANES]
        parts = []
        for b in range(tiles_per_kv):
            blk = kvh * tiles_per_kv + b
            q = q_ref[sub * W:(sub + 1) * W, blk * LANES:(blk + 1) * LANES]
            zero = jnp.zeros_like(q)
            parts += [jnp.where(first, q, zero), jnp.where(first, zero, q)]
        return _dot_nt(jnp.concatenate(parts, axis=0), k)

    pending = scores(*jobs[0])
    for n, (sub, kvh) in enumerate(jobs):
        s_all = pending
        if n + 1 < len(jobs):
            pending = scores(*jobs[n + 1])
        qpos = i * tq + sub * W + lax.broadcasted_iota(jnp.int32, (W, 2 * W), 0)
        kpos = band_start(sub) + lax.broadcasted_iota(jnp.int32, (W, 2 * W), 1)
        valid = jnp.logical_and(kpos <= qpos, kpos > qpos - W)
        probs, inv = [], []
        for g in range(group):
            s = jnp.where(valid, s_all[g * W:(g + 1) * W], -jnp.inf)
            snk = sink_ref[kvh * group + g] * math.log2(math.e)
            m = jnp.maximum(jnp.max(s, axis=1, keepdims=True), snk)
            p = jnp.exp2(s - m)
            inv.append(1.0 / (jnp.sum(p, axis=1, keepdims=True) + jnp.exp2(snk - m)))
            probs.append(p.astype(BF16))
        v = v_ref[pl.ds(band_start(sub), 2 * W), kvh * LANES:(kvh + 1) * LANES]
        o_all = _dot(jnp.concatenate(probs, axis=0), v)
        for b in range(tiles_per_kv):
            blk = kvh * tiles_per_kv + b
            o0 = o_all[2 * b * W:(2 * b + 1) * W] * inv[2 * b]
            o1 = o_all[(2 * b + 1) * W:(2 * b + 2) * W] * inv[2 * b + 1]
            att_ref[sub * W:(sub + 1) * W, blk * LANES:(blk + 1) * LANES] = jnp.where(first, o0, o1).astype(BF16)
    o_ref[...] = x_ref[...] + mod_ref[2:3, :] * _dot(att_ref[...], wo_ref[...])


def _sw_attn(q, kd, vd, sinks, x2d, mod3, w_o, *, B, T, tq, group):
    N, D = x2d.shape
    nq = T // tq
    kvd = kd.shape[1]
    tile = pl.BlockSpec((tq, D), lambda b, i: (b * nq + i, 0))
    seq = pl.BlockSpec((None, T, kvd), lambda b, i: (b, 0, 0))
    return pl.pallas_call(
        functools.partial(_sw_attn_kernel, tq=tq, group=group),
        grid=(B, nq),
        in_specs=[pl.BlockSpec(memory_space=pltpu.SMEM), tile, seq, seq, tile,
                  pl.BlockSpec((None, 3, D), lambda b, i: (b, 0, 0)), _const_spec(w_o.shape)],
        out_specs=tile,
        out_shape=jax.ShapeDtypeStruct((N, D), F32),
        scratch_shapes=[pltpu.VMEM((tq, D), BF16)],
        compiler_params=_cparams("parallel", "arbitrary"),
        name="sw_attn",
    )(sinks, q, kd.reshape(B, T, kvd), vd.reshape(B, T, kvd), x2d, mod3, w_o)


def _rw_prep_kernel(x_ref, xp_ref, mod_ref, g_ref, mu_ref, wrkv_ref, w0_ref, w1_ref, w2_ref, a0_ref, a1_ref,
                    a2_ref, g1_ref, g2_ref, kk_ref, ka_ref, gsum_ref, gexp_ref,
                    r_out, lw_out, k_out, v_out, kk_out, b_out, g_out, *, tiles_per_batch):
    shift, scale, gn = mod_ref[0:1, :], mod_ref[1:2, :], g_ref[...]
    for rows in _sub_tiles(x_ref.shape[0]):
        h = _norm_mod(x_ref[rows, :], gn, shift, scale)
        if rows.start == 0:
            hp = _norm_mod(xp_ref[...], gn, shift, scale)[7:8, :]
            hp = jnp.where(pl.program_id(0) % tiles_per_batch == 0, jnp.zeros_like(hp), hp)
        else:
            hp = _norm_mod(x_ref[rows.start - 8:rows.start, :], gn, shift, scale)[7:8, :]
        row = lax.broadcasted_iota(jnp.int32, (rows.stop - rows.start, 1), 0)
        xx = jnp.where(row == 0, hp, pltpu.roll(h, 1, axis=0)) - h

        def mixed(n):
            return (h + xx * mu_ref[n:n + 1, :]).astype(BF16)

        d1 = _dot(mixed(3), w1_ref[...])
        a1 = _dot(mixed(4), a1_ref[...])
        g1 = _dot(mixed(5), g1_ref[...])
        r = _dot(mixed(0), wrkv_ref[0])
        k = _dot(mixed(1), wrkv_ref[1])
        v = _dot(mixed(2), wrkv_ref[2])
        d = _dot(jnp.tanh(d1).astype(BF16), w2_ref[...]) + w0_ref[...]
        a = _sigmoid(a0_ref[...] + _dot(a1.astype(BF16), a2_ref[...]))
        g_out[rows, :] = _dot(_sigmoid(g1).astype(BF16), g2_ref[...]).astype(BF16)
        z = -d
        softplus = jnp.maximum(z, 0.0) + jnp.log(1.0 + jnp.exp(-jnp.abs(z)))
        lw_out[rows, :] = -jnp.exp(-softplus - 0.5)
        kk = k * kk_ref[...]
        ss = _dot((kk * kk).astype(BF16), gsum_ref[...])
        kk = kk * _group_bcast(1.0 / jnp.maximum(jnp.sqrt(ss), 1e-12), gexp_ref[...])
        r_out[rows, :] = r.astype(BF16)
        k_out[rows, :] = (k * (1.0 + (a - 1.0) * ka_ref[...])).astype(BF16)
        v_out[rows, :] = v.astype(BF16)
        kk_out[rows, :] = kk.astype(BF16)
        b_out[rows, :] = (kk * a).astype(BF16)


def _pad_to(w, axis, n):
    pad = [(0, 0)] * w.ndim
    pad[axis] = (0, n - w.shape[axis])
    return jnp.pad(w, pad)


def _round_up(n, m):
    return -(-n // m) * m


def _rw_prep(x2d, mod3, g, p, *, tm, tiles_per_batch):
    N, D = x2d.shape
    gsum, gexp = _group_mats(D)
    row = lambda t: t.reshape(1, D)
    lora = lambda w1, w2: (_pad_to(w1, 1, _round_up(w1.shape[1], LANES)).astype(BF16),
                           _pad_to(w2, 0, _round_up(w2.shape[0], LANES)).astype(BF16))
    w1, w2 = lora(p["decay_w1"], p["decay_w2"])
    a1, a2 = lora(p["iclr_a1"], p["iclr_a2"])
    g1, g2 = lora(p["gate_g1"], p["gate_g2"])
    consts = [row(g), p["mu"], p["w_rkv"].astype(BF16), row(p["decay_w0"]), w1, w2, row(p["iclr_a0"]), a1, a2,
              g1, g2, row(p["k_k"]), row(p["k_a"]), gsum, gexp]
    tile = pl.BlockSpec((tm, D), lambda i: (i, 0))
    outs = [jax.ShapeDtypeStruct((N, D), BF16)] * 7
    outs[1] = jax.ShapeDtypeStruct((N, D), F32)
    return pl.pallas_call(
        functools.partial(_rw_prep_kernel, tiles_per_batch=tiles_per_batch),
        grid=(N // tm,),
        in_specs=[tile,
                  pl.BlockSpec((8, D), lambda i: (jnp.maximum(i * (tm // 8) - 1, 0), 0)),
                  pl.BlockSpec((None, 3, D), lambda i: (i // tiles_per_batch, 0, 0))]
                 + [_const_spec(c.shape) for c in consts],
        out_specs=[tile] * 7,
        out_shape=outs,
        compiler_params=_cparams("parallel"),
        name="rw_prep",
    )(x2d, x2d, mod3, *consts)


def _stack_heads(x, m0):
    z = jnp.zeros_like(x)
    return jnp.concatenate([jnp.where(m0, x, z), jnp.where(m0, z, x)], axis=0)


def _rw_scan_kernel(r_ref, lw_ref, k_ref, v_ref, kk_ref, b_ref, rk_ref, lng_ref, lnb_ref, o_ref, st_ref, os_ref,
                    *, tt, group):
    C = RW_CHUNK
    C2 = 2 * C

    @pl.when(pl.program_id(2) == 0)
    def _():
        st_ref[...] = jnp.zeros_like(st_ref)

    lane = lax.broadcasted_iota(jnp.int32, (1, LANES), 1)
    m0 = lane < HEAD_DIM
    ti = lax.broadcasted_iota(jnp.int32, (C, C), 0)
    tj = lax.broadcasted_iota(jnp.int32, (C, C), 1)
    tri = (ti >= tj).astype(F32)
    ri = lax.broadcasted_iota(jnp.int32, (2 * C2, 2 * C2), 0)
    ci = lax.broadcasted_iota(jnp.int32, (2 * C2, 2 * C2), 1)
    rt, ct = ri % C, ci % C
    causal = rt - ct + (ri >= C2).astype(jnp.int32) > 0
    eye = (lax.broadcasted_iota(jnp.int32, (C2, C2), 0) == lax.broadcasted_iota(jnp.int32, (C2, C2), 1))
    eye_f = eye.astype(F32)
    si = lax.broadcasted_iota(jnp.int32, (C2, C2), 0)
    sj = lax.broadcasted_iota(jnp.int32, (C2, C2), 1)
    merge_masks = []
    s = 1
    while s < C:
        merge_masks.append(jnp.logical_and(si // (2 * s) == sj // (2 * s),
                                           jnp.logical_and(si % (2 * s) >= s, sj % (2 * s) < s)))
        s *= 2
    zeros_sq = jnp.zeros((C2, C2), BF16)

    def chunk_terms(chunks):
        n = range(len(chunks))
        a_t, r_t, v_s, aa, lhs_top, p_col = [], [], [], [], [], []
        for i, c in enumerate(chunks):
            rows = slice(c * C, (c + 1) * C)
            lw = lw_ref[rows, :]
            r = r_ref[rows, :].astype(F32)
            k = k_ref[rows, :].astype(F32)
            kk = kk_ref[rows, :].astype(F32)
            b = b_ref[rows, :].astype(F32)
            cum = jnp.dot(tri, lw, precision=HIGHEST, preferred_element_type=F32)
            tot = cum[C - 1:C, :]
            e_neg = jnp.exp(-cum)
            r_t.append(_stack_heads(r * jnp.exp(cum), m0).astype(BF16))
            a_t.append(_stack_heads(-kk * jnp.exp(cum - lw), m0).astype(BF16))
            b_t = _stack_heads(b * e_neg, m0).astype(BF16)
            k_t = _stack_heads(k * e_neg, m0).astype(BF16)
            e_end = jnp.exp(tot - cum)
            b_h = _stack_heads(b * e_end, m0)
            k_h = _stack_heads(k * e_end, m0)
            lhs_top.append(jnp.concatenate([b_h.T, k_h.T], axis=1).astype(BF16))
            v_s.append(_stack_heads(v_ref[rows, :], m0))
            p_col.append(jnp.sum(jnp.where(eye, jnp.exp(tot), 0.0), axis=1, keepdims=True))
            full = _dot_nt(jnp.concatenate([a_t[i], r_t[i]], axis=0), jnp.concatenate([b_t, k_t], axis=0))
            aa.append(jnp.where(causal, full, 0.0))
        yield

        a_ab = [aa[i][:C2, :C2] for i in n]
        t_inv = [eye_f + jnp.where(merge_masks[0], a_ab[i], 0.0) for i in n]
        for mask in merge_masks[1:]:
            tb = [t_inv[i].astype(BF16) for i in n]
            half = [_dot(tb[i], jnp.where(mask, a_ab[i], 0.0).astype(BF16)).astype(BF16) for i in n]
            yield
            t_inv = [t_inv[i] + _dot(half[i], tb[i]) for i in n]
            yield
        akv = [_dot(aa[i][:C2, C2:].astype(BF16), v_s[i]).astype(BF16) for i in n]
        yield
        au = [_dot(t_inv[i].astype(BF16), jnp.concatenate([a_t[i], akv[i]], axis=1)).astype(BF16)
              for i in n]
        yield
        big = []
        for i in n:
            rhs = jnp.concatenate([au[i], jnp.concatenate([zeros_sq, v_s[i]], axis=1)], axis=0)
            lhs = jnp.concatenate([lhs_top[i], aa[i][C2:, :].astype(BF16)], axis=0)
            big.append(_dot(lhs, rhs))
        qg = [jnp.concatenate([big[i][C2:, :C2] + r_t[i].astype(F32), big[i][:C2, :C2]], axis=0).astype(BF16)
              for i in n]
        return qg, big, p_col

    blk2 = jnp.tile(((lax.broadcasted_iota(jnp.int32, (LANES, LANES), 0) < HEAD_DIM)
                     == (lax.broadcasted_iota(jnp.int32, (LANES, LANES), 1) < HEAD_DIM)).astype(BF16), (2, 1))

    def head_sum(x):
        hi = x.astype(BF16)
        lo = (x - hi.astype(F32)).astype(BF16)
        return _dot(jnp.concatenate([hi, lo], axis=1), blk2)

    def state_steps(st, chunks, terms):
        qg, big, p_col = terms
        for i, c in enumerate(chunks):
            res = _dot(qg[i], st[0].astype(BF16))
            o_st = res[:C2] + big[i][C2:, C2:]
            os_ref[c * C:(c + 1) * C, :] = o_st[:C] + o_st[C:]
            st[0] = p_col[i] * st[0] + res[C2:] + big[i][:C2, C2:]
            yield
        rows = slice(chunks[0] * C, (chunks[-1] + 1) * C)
        o = os_ref[rows, :]
        mean = head_sum(o) * (1.0 / HEAD_DIM)
        cen = o - mean
        var = head_sum(cen * cen) * (1.0 / HEAD_DIM)
        y = cen * lax.rsqrt(var + RW_LNX_EPS) * lng_ref[...] + lnb_ref[...]
        r = r_ref[rows, :].astype(F32)
        k = k_ref[rows, :].astype(F32)
        bonus = head_sum(r * k * rk_ref[...]) * v_ref[rows, :].astype(F32)
        o_ref[rows, :] = (y + bonus).astype(BF16)
        yield

    st = [st_ref[...]]
    n_chunks = tt // C
    sizes = [group] * (n_chunks // group) + ([n_chunks % group] if n_chunks % group else [])
    if sizes[-1] > 1:
        sizes[-1:] = [sizes[-1] - sizes[-1] // 2, sizes[-1] // 2]
    pending = iter(())
    g0 = 0
    for size in sizes:
        chunks = list(range(g0, g0 + size))
        g0 += size
        gen = chunk_terms(chunks)
        while True:
            try:
                next(gen)
            except StopIteration as done:
                terms = done.value
                break
            next(pending, None)
            next(pending, None)
        for _ in pending:
            pass
        pending = state_steps(st, chunks, terms)
    for _ in pending:
        pass
    st_ref[...] = st[0]


def _rw_scan(r, lw, k, v, kk, b, r_k, lnx_g, lnx_b, *, B, T, D, tt):
    hp = D // LANES
    blk = pl.BlockSpec((None, tt, LANES), lambda bi, h, t: (bi, t, h))
    vec = pl.BlockSpec((1, LANES), lambda bi, h, t: (0, h))
    r3 = lambda t: t.reshape(B, T, D)
    out = pl.pallas_call(
        functools.partial(_rw_scan_kernel, tt=tt, group=RW_GROUP),
        grid=(B, hp, T // tt),
        in_specs=[blk] * 6 + [vec] * 3,
        out_specs=blk,
        out_shape=jax.ShapeDtypeStruct((B, T, D), BF16),
        scratch_shapes=[pltpu.VMEM((LANES, LANES), F32), pltpu.VMEM((tt, LANES), F32)],
        compiler_params=_cparams("parallel", "parallel", "arbitrary"),
        name="rw_scan",
    )(r3(r), r3(lw), r3(k), r3(v), r3(kk), r3(b), r_k.reshape(1, D), lnx_g.reshape(1, D), lnx_b.reshape(1, D))
    return out.reshape(B * T, D)


def _diff_lambda_init(layer):
    return 0.8 - 0.6 * math.exp(-0.3 * layer)


def _pick_tile(n, pref):
    t = min(n, pref)
    while n % t:
        t //= 2
    return t


def kernel(x, c, norm_g, ada_w, ada_b, ffn_w_in, ffn_w_out, da_w_qkv, da_w_o, da_q_norm_g, da_k_norm_g, da_lambda, da_subln_g, rw_mu, rw_w_rkv, rw_w_o, rw_decay_w0, rw_decay_w1, rw_decay_w2, rw_iclr_a0, rw_iclr_a1, rw_iclr_a2, rw_gate_g1, rw_gate_g2, rw_k_k, rw_k_a, rw_r_k, rw_lnx_g, rw_lnx_b, sw_w_qkv, sw_w_o, sw_q_norm_g, sw_k_norm_g, sw_sinks):
    B, T, D = x.shape
    L = norm_g.shape[0]
    tm = _pick_tile(T, 2 * PROJ_ROWS)
    tpb = T // tm
    kw = dict(tm=tm, tiles_per_batch=tpb)
    tf = _pick_tile(T, 2 * FFN_ROWS)
    fkw = dict(tm=tf, tiles_per_batch=T // tf)
    scale = HEAD_DIM ** -0.5

    mod = _ada(c, ada_w, ada_b).reshape(L, B, 3, 3, D)
    x2d = x.reshape(B * T, D)
    for i in range(L):
        kind, j = i % 3, i // 3
        x2d = _ffn(x2d, mod[i, :, 0], norm_g[i, 0], ffn_w_in[i, 0].astype(BF16), ffn_w_out[i, 0].astype(BF16), **fkw)
        m2 = mod[i, :, 1]
        if kind == 0:
            qg = jnp.tile(da_q_norm_g[j] * (scale * math.log2(math.e)), D // HEAD_DIM).reshape(1, D)
            kg = jnp.tile(da_k_norm_g[j], D // HEAD_DIM).reshape(1, D)
            q, k, vt = _da_qkv(x2d, m2, norm_g[i, 1], da_w_qkv[j].astype(BF16), qg, kg, B=B, T=T, tm=tm)
            o = _da_attn(q, k, vt, da_lambda[j], da_subln_g[j], B=B, T=T, D=D, tq=_pick_tile(T, 512),
                         lambda_init=_diff_lambda_init(i))
            mix = (o, None, m2, da_w_o[j].astype(BF16))
        elif kind == 1:
            p = dict(mu=rw_mu[j], w_rkv=rw_w_rkv[j], decay_w0=rw_decay_w0[j], decay_w1=rw_decay_w1[j],
                     decay_w2=rw_decay_w2[j], iclr_a0=rw_iclr_a0[j], iclr_a1=rw_iclr_a1[j], iclr_a2=rw_iclr_a2[j],
                     gate_g1=rw_gate_g1[j], gate_g2=rw_gate_g2[j], k_k=rw_k_k[j], k_a=rw_k_a[j])
            r, lw, k, v, kk, b, g = _rw_prep(x2d, m2, norm_g[i, 1], p, **kw)
            y = _rw_scan(r, lw, k, v, kk, b, rw_r_k[j], rw_lnx_g[j], rw_lnx_b[j], B=B, T=T, D=D,
                         tt=_pick_tile(T, 4096))
            mix = (y, g, m2, rw_w_o[j].astype(BF16))
        else:
            kv = (sw_w_qkv.shape[-1] - D) // 2
            n_kv = kv // HEAD_DIM
            qg = jnp.tile(sw_q_norm_g[j] * (scale * math.log2(math.e)), D // HEAD_DIM).reshape(1, D)
            dup = lambda w: jnp.repeat(w.reshape(D, n_kv, 1, HEAD_DIM), 2, axis=2).reshape(D, 2 * kv)
            w = sw_w_qkv[j]
            w = jnp.concatenate([w[:, :D], dup(w[:, D:D + kv]), dup(w[:, D + kv:])], axis=1).astype(BF16)
            kg = jnp.tile(sw_k_norm_g[j], 2 * n_kv).reshape(1, 2 * kv)
            q, k, v = _sw_qkv(x2d, m2, norm_g[i, 1], w, qg, kg, **kw)
            x2d = _sw_attn(q, k, v, sw_sinks[j], x2d, m2, sw_w_o[j].astype(BF16), B=B, T=T,
                           tq=_pick_tile(T, 512), group=(D // HEAD_DIM) // n_kv)
            mix = None
        x2d = _ffn(x2d, mod[i, :, 2], norm_g[i, 2], ffn_w_in[i, 1].astype(BF16), ffn_w_out[i, 1].astype(BF16),
                   mix=mix, **fkw)
    return x2d.reshape(B, T, D)
```

```python
import functools
import math

import jax
import jax.numpy as jnp
from jax import lax
from jax.experimental import pallas as pl
from jax.experimental.pallas import tpu as pltpu

F32 = jnp.float32
BF16 = jnp.bfloat16
HIGHEST = lax.Precision.HIGHEST

HEAD_DIM = 64
LANES = 128
NORM_EPS = 1e-6
SUBLN_EPS = 1e-5
RW_LNX_EPS = 64e-5
SW_WINDOW = 128
N_MOD = 9
PROJ_ROWS = 512
FFN_ROWS = 512
RW_GROUP = 16
RW_CHUNK = 64
VMEM_LIMIT = 56 * 1024 * 1024


def _cparams(*sem):
    return pltpu.CompilerParams(dimension_semantics=sem, vmem_limit_bytes=VMEM_LIMIT)


def _sigmoid(x):
    return 1.0 / (1.0 + jnp.exp(-x))


def _dot(a, b):
    return jnp.dot(a, b, preferred_element_type=F32)


def _norm_mod(x, g, shift, scale):
    ms = jnp.mean(x * x, axis=-1, keepdims=True)
    return (x * lax.rsqrt(ms + NORM_EPS) * g) * (1.0 + scale) + shift


def _sub_tiles(n):
    step = min(n, PROJ_ROWS)
    return [slice(r0, r0 + step) for r0 in range(0, n, step)]


def _const_spec(shape):
    nd = len(shape)
    return pl.BlockSpec(shape, lambda *_: (0,) * nd, pipeline_mode=pl.Buffered(1))


def _group_mats(d):
    grp = jnp.arange(d) // HEAD_DIM
    gsum = (grp[:, None] == jnp.arange(LANES)[None, :]).astype(BF16)
    return gsum, jnp.concatenate([gsum.T, gsum.T], axis=0)


def _group_bcast(v, gexp):
    hi = v.astype(BF16)
    lo = (v - hi.astype(F32)).astype(BF16)
    return _dot(jnp.concatenate([hi, lo], axis=1), gexp)


def _ada_kernel(c_ref, w_ref, b_ref, o_ref):
    c = c_ref[...]
    cond = c * _sigmoid(c)
    o_ref[...] = jnp.dot(cond, w_ref[...], precision=HIGHEST, preferred_element_type=F32) + b_ref[...]


def _ada(c, ada_w, ada_b):
    L, D, ND = ada_w.shape
    B = c.shape[0]
    return pl.pallas_call(
        _ada_kernel,
        grid=(L, ND // D),
        in_specs=[pl.BlockSpec((B, D), lambda l, j: (0, 0)),
                  pl.BlockSpec((None, D, D), lambda l, j: (l, 0, j)),
                  pl.BlockSpec((None, 1, D), lambda l, j: (l, 0, j))],
        out_specs=pl.BlockSpec((None, B, D), lambda l, j: (l, 0, j)),
        out_shape=jax.ShapeDtypeStruct((L, B, ND), F32),
        compiler_params=_cparams("parallel", "parallel"),
        name="ada_mod",
    )(c, ada_w, ada_b.reshape(L, 1, ND))


def _ffn_kernel(*refs, d_ff, chunks, rows, mixer):
    if mixer is None:
        x_ref, mod_ref, g_ref, win_ref, wout_ref, o_ref = refs
    elif mixer == "plain":
        x_ref, mod_ref, g_ref, win_ref, wout_ref, y_ref, mixmod_ref, wo_ref, o_ref = refs
    else:
        x_ref, mod_ref, g_ref, win_ref, wout_ref, y_ref, ygate_ref, mixmod_ref, wo_ref, o_ref = refs
    tm = x_ref.shape[0]
    subs = list(range(0, tm, rows))

    def residual_in(r0):
        x = x_ref[r0:r0 + rows, :]
        if mixer is None:
            return x
        y = y_ref[r0:r0 + rows, :]
        if mixer == "gated":
            y = (y.astype(F32) * ygate_ref[r0:r0 + rows, :].astype(F32)).astype(BF16)
        return x + mixmod_ref[2:3, :] * _dot(y, wo_ref[...])

    xs, hidden = {}, {}

    def prepare(r0):
        xs[r0] = residual_in(r0)
        hidden[r0] = _norm_mod(xs[r0], g_ref[...], mod_ref[0:1, :], mod_ref[1:2, :]).astype(BF16)

    def gate_up(r0, c0, c1):
        h = hidden[r0]
        return _dot(h, win_ref[:, c0:c1]), _dot(h, win_ref[:, d_ff + c0:d_ff + c1])

    jobs = [(r0, c0, c1) for r0 in subs for c0, c1 in chunks]
    prepare(subs[0])
    pending = gate_up(*jobs[0])
    acc = None
    for n, (r0, c0, c1) in enumerate(jobs):
        gate, up = pending
        if n + 1 < len(jobs):
            if jobs[n + 1][0] not in hidden:
                prepare(jobs[n + 1][0])
            pending = gate_up(*jobs[n + 1])
        if n == 0:
            for r in subs[1:]:
                if r not in hidden:
                    prepare(r)
        act = (gate * _sigmoid(gate) * up).astype(BF16)
        y = _dot(act, wout_ref[c0:c1, :])
        acc = y if c0 == 0 else acc + y
        if c1 == d_ff:
            o_ref[r0:r0 + rows, :] = xs[r0] + (0.5 * mod_ref[2:3, :]) * acc


def _ffn_chunks(d_ff):
    last = 2 * LANES
    if d_ff % LANES or d_ff < 4 * last:
        return [(0, d_ff)]
    mid = (d_ff // 2) // LANES * LANES
    return [(0, mid), (mid, d_ff - last), (d_ff - last, d_ff)]


def _ffn(x2d, mod3, g, w_in, w_out, *, tm, tiles_per_batch, mix=None):
    N, D = x2d.shape
    d_ff = w_out.shape[0]
    tile = pl.BlockSpec((tm, D), lambda i: (i, 0))
    modspec = pl.BlockSpec((None, 3, D), lambda i: (i // tiles_per_batch, 0, 0))
    ins = [x2d, mod3, g.reshape(1, D), w_in, w_out]
    specs = [tile, modspec, _const_spec((1, D)), _const_spec(w_in.shape), _const_spec(w_out.shape)]
    mixer = None
    if mix is not None:
        y, y_gate, mixmod, w_o = mix
        mixer = "plain" if y_gate is None else "gated"
        ins += [y] + ([] if y_gate is None else [y_gate]) + [mixmod, w_o]
        specs += [tile] + ([] if y_gate is None else [tile]) + [modspec, _const_spec(w_o.shape)]
    kern = functools.partial(_ffn_kernel, d_ff=d_ff, chunks=_ffn_chunks(d_ff), rows=min(tm, FFN_ROWS), mixer=mixer)
    return pl.pallas_call(
        kern,
        grid=(N // tm,),
        in_specs=specs,
        out_specs=tile,
        out_shape=jax.ShapeDtypeStruct((N, D), F32),
        compiler_params=_cparams("parallel"),
        name="ffn_half" if mixer is None else "ffn_half_" + mixer,
    )(*ins)


def _group_rms_scale(y, gsum, gexp):
    ss = _dot((y * y).astype(BF16), gsum)
    return _group_bcast(lax.rsqrt(ss * (1.0 / HEAD_DIM) + NORM_EPS), gexp)


def _da_qkv_kernel(x_ref, mod_ref, g_ref, w_ref, gsum_ref, gexp_ref, qg_ref, kg_ref, qt_ref, k_ref, vt_ref):
    D = x_ref.shape[1]
    for rows in _sub_tiles(x_ref.shape[0]):
        h = _norm_mod(x_ref[rows, :], g_ref[...], mod_ref[0:1, :], mod_ref[1:2, :]).astype(BF16)
        q = _dot(h, w_ref[:, 0:D])
        k = _dot(h, w_ref[:, D:2 * D])
        v = _dot(h, w_ref[:, 2 * D:3 * D])
        vt_ref[:, rows] = v.T.astype(BF16)
        qt_ref[:, rows] = (q * _group_rms_scale(q, gsum_ref[...], gexp_ref[...]) * qg_ref[...]).T.astype(BF16)
        k_ref[rows, :] = (k * _group_rms_scale(k, gsum_ref[...], gexp_ref[...]) * kg_ref[...]).astype(BF16)


def _da_qkv(x2d, mod3, g, w_qkv, q_gain, k_gain, *, B, T, tm):
    N, D = x2d.shape
    tpb = T // tm
    gsum, gexp = _group_mats(D)
    tile = pl.BlockSpec((tm, D), lambda i: (i, 0))
    transposed = pl.BlockSpec((None, D, tm), lambda i: (i // tpb, 0, i % tpb))
    return pl.pallas_call(
        _da_qkv_kernel,
        grid=(N // tm,),
        in_specs=[tile,
                  pl.BlockSpec((None, 3, D), lambda i: (i // tpb, 0, 0)),
                  _const_spec((1, D)), _const_spec(w_qkv.shape), _const_spec(gsum.shape),
                  _const_spec(gexp.shape), _const_spec((1, D)), _const_spec((1, D))],
        out_specs=[transposed, tile, transposed],
        out_shape=[jax.ShapeDtypeStruct((B, D, T), BF16), jax.ShapeDtypeStruct((N, D), BF16),
                   jax.ShapeDtypeStruct((B, D, T), BF16)],
        compiler_params=_cparams("parallel"),
        name="da_qkv",
    )(x2d, mod3, g.reshape(1, D), w_qkv, gsum, gexp, q_gain, k_gain)


def _da_attn_kernel(lam_ref, sg_ref, qt_ref, k_ref, vt_ref, o_ref, acc_ref, sa_ref, sb_ref, *, tq):
    nq = k_ref.shape[0] // tq
    dim = lax.broadcasted_iota(jnp.int32, (LANES, 1), 0)
    lam = lam_ref[...]
    lambda_init = lam[4:5, 0:1]
    lam_full = (jnp.exp(jnp.sum(lam[0:1] * lam[1:2], axis=1, keepdims=True))
                - jnp.exp(jnp.sum(lam[2:3] * lam[3:4], axis=1, keepdims=True)) + lambda_init)
    kpos = lax.broadcasted_iota(jnp.int32, (tq, tq), 0)
    qpos = lax.broadcasted_iota(jnp.int32, (tq, tq), 1)
    causal = kpos <= qpos
    bufs = (sa_ref, sb_ref)
    q_maps = {}

    def q_masked(qi):
        if qi not in q_maps:
            qt = qt_ref[:, qi * tq:(qi + 1) * tq]
            zero = jnp.zeros_like(qt)
            q_maps[qi] = (jnp.where(dim < HEAD_DIM, qt, zero), jnp.where(dim >= HEAD_DIM, qt, zero))
        return q_maps[qi]

    def scores_into(s_ref, qi, j):
        k = k_ref[j * tq:(j + 1) * tq, :]
        for c in range(2):
            s_ref[c] = _dot(k, q_masked(qi)[c])

    def consume(s_ref, acc, qi, j, carry):
        vt = vt_ref[:, j * tq:(j + 1) * tq]
        out = []
        for c in range(2):
            s = s_ref[c]
            if j == qi:
                s = jnp.where(causal, s, -jnp.inf)
            m_cur = jnp.max(s, axis=0, keepdims=True)
            if j == 0:
                m_new = m_cur
                p = jnp.exp2(s - m_new)
                l_new = jnp.sum(p, axis=0, keepdims=True)
                acc[c] = _dot(vt, p.astype(BF16))
            else:
                m_prev, l_prev = carry[2 * c], carry[2 * c + 1]
                m_new = jnp.maximum(m_prev, m_cur)
                alpha = jnp.exp2(m_prev - m_new)
                p = jnp.exp2(s - m_new)
                l_new = alpha * l_prev + jnp.sum(p, axis=0, keepdims=True)
                acc[c] = alpha * acc[c] + _dot(vt, p.astype(BF16))
            out += [m_new, l_new]
        return tuple(out)

    def finish(acc, qi, carry):
        ot = acc[0] / carry[1] - lam_full * (acc[1] / carry[3])
        ms = jnp.mean(ot * ot, axis=0, keepdims=True)
        ot = ot * lax.rsqrt(ms + SUBLN_EPS)
        o_ref[qi * tq:(qi + 1) * tq, :] = (ot.T * sg_ref[...] * (1.0 - lambda_init)).astype(BF16)

    tasks = [(qi, j) for qi in range(nq) for j in range(qi + 1)]
    scores_into(bufs[0], *tasks[0])
    carry = None
    for t, (qi, j) in enumerate(tasks):
        if t + 1 < len(tasks):
            scores_into(bufs[(t + 1) % 2], *tasks[t + 1])
        acc = acc_ref.at[qi % 2]
        carry = consume(bufs[t % 2], acc, qi, j, carry)
        if j == qi:
            finish(acc, qi, carry)


def _da_attn(qt, k, vt, lambdas, subln_g, *, B, T, D, tq, lambda_init):
    H = D // LANES
    k3 = k.reshape(B, T, D)
    seq = pl.BlockSpec((None, T, LANES), lambda b, h: (b, 0, h))
    seq_t = pl.BlockSpec((None, LANES, T), lambda b, h: (b, h, 0))
    out = pl.pallas_call(
        functools.partial(_da_attn_kernel, tq=tq),
        grid=(B, H),
        in_specs=[pl.BlockSpec((5, HEAD_DIM), lambda b, h: (0, 0)),
                  pl.BlockSpec((1, LANES), lambda b, h: (0, 0)),
                  seq_t, seq, seq_t],
        out_specs=seq,
        out_shape=jax.ShapeDtypeStruct((B, T, D), BF16),
        scratch_shapes=[pltpu.VMEM((2, 2, LANES, tq), F32),
                        pltpu.VMEM((2, tq, tq), F32), pltpu.VMEM((2, tq, tq), F32)],
        compiler_params=_cparams("parallel", "parallel"),
        name="da_attn",
    )(jnp.concatenate([lambdas, jnp.full((1, HEAD_DIM), lambda_init, F32)], axis=0),
      subln_g.reshape(1, LANES), qt, k3, vt)
    return out.reshape(B * T, D)


def _sw_qkv_kernel(x_ref, mod_ref, g_ref, w_ref, gsum_ref, gexp_ref, ksum_ref, kexp_ref, qg_ref, kg_ref,
                   q_ref, kt_ref, v_ref):
    D = x_ref.shape[1]
    kv = v_ref.shape[1]
    for rows in _sub_tiles(x_ref.shape[0]):
        h = _norm_mod(x_ref[rows, :], g_ref[...], mod_ref[0:1, :], mod_ref[1:2, :]).astype(BF16)
        q = _dot(h, w_ref[:, 0:D])
        k = _dot(h, w_ref[:, D:D + kv])
        v_ref[rows, :] = _dot(h, w_ref[:, D + kv:D + 2 * kv]).astype(BF16)
        q_ref[rows, :] = (q * _group_rms_scale(q, gsum_ref[...], gexp_ref[...]) * qg_ref[...]).astype(BF16)
        kt_ref[:, rows] = (k * _group_rms_scale(k, ksum_ref[...], kexp_ref[...]) * kg_ref[...]).T.astype(BF16)


def _sw_qkv(x2d, mod3, g, w_qkv, q_gain, k_gain, *, B, T, tm, tiles_per_batch):
    N, D = x2d.shape
    kv = (w_qkv.shape[1] - D) // 2
    gsum, gexp = _group_mats(D)
    ksum, kexp = _group_mats(kv)
    tile = pl.BlockSpec((tm, D), lambda i: (i, 0))
    kvtile = pl.BlockSpec((tm, kv), lambda i: (i, 0))
    return pl.pallas_call(
        _sw_qkv_kernel,
        grid=(N // tm,),
        in_specs=[tile, pl.BlockSpec((None, 3, D), lambda i: (i // tiles_per_batch, 0, 0)),
                  _const_spec((1, D)), _const_spec(w_qkv.shape), _const_spec(gsum.shape),
                  _const_spec(gexp.shape), _const_spec(ksum.shape), _const_spec(kexp.shape),
                  _const_spec((1, D)), _const_spec((1, kv))],
        out_specs=[tile, pl.BlockSpec((None, kv, tm), lambda i: (i // tiles_per_batch, 0, i % tiles_per_batch)),
                   kvtile],
        out_shape=[jax.ShapeDtypeStruct((N, D), BF16), jax.ShapeDtypeStruct((B, kv, T), BF16),
                   jax.ShapeDtypeStruct((N, kv), BF16)],
        compiler_params=_cparams("parallel"),
        name="sw_qkv",
    )(x2d, mod3, g.reshape(1, D), w_qkv, gsum, gexp, ksum, kexp, q_gain, k_gain)


def _sw_attn_kernel(sink_ref, q_ref, kt_ref, v_ref, x_ref, mod_ref, wo_ref, o_ref, att_ref, *, tq, group):
    i = pl.program_id(1)
    D = q_ref.shape[1]
    W = SW_WINDOW
    n_kv = (D // HEAD_DIM) // group
    tiles_per_kv = group // 2
    lane = lax.broadcasted_iota(jnp.int32, (1, LANES), 1)
    first = lane < HEAD_DIM
    jobs = [(sub, kvh) for sub in range(tq // W) for kvh in range(n_kv)]

    def band_start(sub):
        return pl.multiple_of(jnp.maximum(i * tq + (sub - 1) * W, 0), W)

    def scores(sub, kvh):
        kt = kt_ref[kvh * LANES:(kvh + 1) * LANES, pl.ds(band_start(sub), 2 * W)]
        parts = []
        for b in range(tiles_per_kv):
            blk = kvh * tiles_per_kv + b
            q = q_ref[sub * W:(sub + 1) * W, blk * LANES:(blk + 1) * LANES]
            zero = jnp.zeros_like(q)
            parts += [jnp.where(first, q, zero), jnp.where(first, zero, q)]
        return _dot(jnp.concatenate(parts, axis=0), kt)

    pending = scores(*jobs[0])
    for n, (sub, kvh) in enumerate(jobs):
        s_all = pending
        if n + 1 < len(jobs):
            pending = scores(*jobs[n + 1])
        qpos = i * tq + sub * W + lax.broadcasted_iota(jnp.int32, (W, 2 * W), 0)
        kpos = band_start(sub) + lax.broadcasted_iota(jnp.int32, (W, 2 * W), 1)
        valid = jnp.logical_and(kpos <= qpos, kpos > qpos - W)
        probs, inv = [], []
        for g in range(group):
            s = jnp.where(valid, s_all[g * W:(g + 1) * W], -jnp.inf)
            snk = sink_ref[kvh * group + g] * math.log2(math.e)
            m = jnp.maximum(jnp.max(s, axis=1, keepdims=True), snk)
            p = jnp.exp2(s - m)
            inv.append(1.0 / (jnp.sum(p, axis=1, keepdims=True) + jnp.exp2(snk - m)))
            probs.append(p.astype(BF16))
        v = v_ref[pl.ds(band_start(sub), 2 * W), kvh * LANES:(kvh + 1) * LANES]
        o_all = _dot(jnp.concatenate(probs, axis=0), v)
        for b in range(tiles_per_kv):
            blk = kvh * tiles_per_kv + b
            o0 = o_all[2 * b * W:(2 * b + 1) * W] * inv[2 * b]
            o1 = o_all[(2 * b + 1) * W:(2 * b + 2) * W] * inv[2 * b + 1]
            att_ref[sub * W:(sub + 1) * W, blk * LANES:(blk + 1) * LANES] = jnp.where(first, o0, o1).astype(BF16)
    o_ref[...] = x_ref[...] + mod_ref[2:3, :] * _dot(att_ref[...], wo_ref[...])


def _sw_attn(q, kt, vd, sinks, x2d, mod3, w_o, *, B, T, tq, group):
    N, D = x2d.shape
    nq = T // tq
    kvd = vd.shape[1]
    tile = pl.BlockSpec((tq, D), lambda b, i: (b * nq + i, 0))
    seq = pl.BlockSpec((None, T, kvd), lambda b, i: (b, 0, 0))
    return pl.pallas_call(
        functools.partial(_sw_attn_kernel, tq=tq, group=group),
        grid=(B, nq),
        in_specs=[pl.BlockSpec(memory_space=pltpu.SMEM), tile,
                  pl.BlockSpec((None, kvd, T), lambda b, i: (b, 0, 0)), seq, tile,
                  pl.BlockSpec((None, 3, D), lambda b, i: (b, 0, 0)), _const_spec(w_o.shape)],
        out_specs=tile,
        out_shape=jax.ShapeDtypeStruct((N, D), F32),
        scratch_shapes=[pltpu.VMEM((tq, D), BF16)],
        compiler_params=_cparams("parallel", "arbitrary"),
        name="sw_attn",
    )(sinks, q, kt, vd.reshape(B, T, kvd), x2d, mod3, w_o)


def _rw_prep_kernel(x_ref, xp_ref, mod_ref, g_ref, mu_ref, wrkv_ref, w0_ref, w1_ref, w2_ref, a0_ref, a1_ref,
                    a2_ref, g1_ref, g2_ref, kk_ref, ka_ref, gsum_ref, gexp_ref,
                    r_out, lw_out, k_out, v_out, kk_out, b_out, g_out, *, tiles_per_batch):
    shift, scale, gn = mod_ref[0:1, :], mod_ref[1:2, :], g_ref[...]
    for rows in _sub_tiles(x_ref.shape[0]):
        h = _norm_mod(x_ref[rows, :], gn, shift, scale)
        if rows.start == 0:
            hp = _norm_mod(xp_ref[...], gn, shift, scale)[7:8, :]
            hp = jnp.where(pl.program_id(0) % tiles_per_batch == 0, jnp.zeros_like(hp), hp)
        else:
            hp = _norm_mod(x_ref[rows.start - 8:rows.start, :], gn, shift, scale)[7:8, :]
        row = lax.broadcasted_iota(jnp.int32, (rows.stop - rows.start, 1), 0)
        xx = jnp.where(row == 0, hp, pltpu.roll(h, 1, axis=0)) - h

        def mixed(n):
            return (h + xx * mu_ref[n:n + 1, :]).astype(BF16)

        d1 = _dot(mixed(3), w1_ref[...])
        a1 = _dot(mixed(4), a1_ref[...])
        g1 = _dot(mixed(5), g1_ref[...])
        r = _dot(mixed(0), wrkv_ref[0])
        k = _dot(mixed(1), wrkv_ref[1])
        v = _dot(mixed(2), wrkv_ref[2])
        d = _dot(jnp.tanh(d1).astype(BF16), w2_ref[...]) + w0_ref[...]
        a = _sigmoid(a0_ref[...] + _dot(a1.astype(BF16), a2_ref[...]))
        g_out[rows, :] = _dot(_sigmoid(g1).astype(BF16), g2_ref[...]).astype(BF16)
        z = -d
        softplus = jnp.maximum(z, 0.0) + jnp.log(1.0 + jnp.exp(-jnp.abs(z)))
        lw_out[rows, :] = -jnp.exp(-softplus - 0.5)
        kk = k * kk_ref[...]
        ss = _dot((kk * kk).astype(BF16), gsum_ref[...])
        kk = kk * _group_bcast(1.0 / jnp.maximum(jnp.sqrt(ss), 1e-12), gexp_ref[...])
        r_out[rows, :] = r.astype(BF16)
        k_out[rows, :] = (k * (1.0 + (a - 1.0) * ka_ref[...])).astype(BF16)
        v_out[rows, :] = v.astype(BF16)
        kk_out[rows, :] = kk.astype(BF16)
        b_out[rows, :] = (kk * a).astype(BF16)


def _pad_to(w, axis, n):
    pad = [(0, 0)] * w.ndim
    pad[axis] = (0, n - w.shape[axis])
    return jnp.pad(w, pad)


def _round_up(n, m):
    return -(-n // m) * m


def _rw_prep(x2d, mod3, g, p, *, tm, tiles_per_batch):
    N, D = x2d.shape
    gsum, gexp = _group_mats(D)
    row = lambda t: t.reshape(1, D)
    lora = lambda w1, w2: (_pad_to(w1, 1, _round_up(w1.shape[1], LANES)).astype(BF16),
                           _pad_to(w2, 0, _round_up(w2.shape[0], LANES)).astype(BF16))
    w1, w2 = lora(p["decay_w1"], p["decay_w2"])
    a1, a2 = lora(p["iclr_a1"], p["iclr_a2"])
    g1, g2 = lora(p["gate_g1"], p["gate_g2"])
    consts = [row(g), p["mu"], p["w_rkv"].astype(BF16), row(p["decay_w0"]), w1, w2, row(p["iclr_a0"]), a1, a2,
              g1, g2, row(p["k_k"]), row(p["k_a"]), gsum, gexp]
    tile = pl.BlockSpec((tm, D), lambda i: (i, 0))
    outs = [jax.ShapeDtypeStruct((N, D), BF16)] * 7
    outs[1] = jax.ShapeDtypeStruct((N, D), F32)
    return pl.pallas_call(
        functools.partial(_rw_prep_kernel, tiles_per_batch=tiles_per_batch),
        grid=(N // tm,),
        in_specs=[tile,
                  pl.BlockSpec((8, D), lambda i: (jnp.maximum(i * (tm // 8) - 1, 0), 0)),
                  pl.BlockSpec((None, 3, D), lambda i: (i // tiles_per_batch, 0, 0))]
                 + [_const_spec(c.shape) for c in consts],
        out_specs=[tile] * 7,
        out_shape=outs,
        compiler_params=_cparams("parallel"),
        name="rw_prep",
    )(x2d, x2d, mod3, *consts)


def _stack_heads(x, m0):
    z = jnp.zeros_like(x)
    return jnp.concatenate([jnp.where(m0, x, z), jnp.where(m0, z, x)], axis=0)


def _rw_scan_kernel(r_ref, lw_ref, k_ref, v_ref, kk_ref, b_ref, rk_ref, lng_ref, lnb_ref, o_ref, st_ref, os_ref,
                    *, tt, group):
    C = RW_CHUNK
    C2 = 2 * C

    @pl.when(pl.program_id(2) == 0)
    def _():
        st_ref[...] = jnp.zeros_like(st_ref)

    lane = lax.broadcasted_iota(jnp.int32, (1, LANES), 1)
    m0 = lane < HEAD_DIM
    ti = lax.broadcasted_iota(jnp.int32, (C, C), 0)
    tj = lax.broadcasted_iota(jnp.int32, (C, C), 1)
    tri = (ti >= tj).astype(F32)
    ri = lax.broadcasted_iota(jnp.int32, (2 * C2, 2 * C2), 0)
    ci = lax.broadcasted_iota(jnp.int32, (2 * C2, 2 * C2), 1)
    rt, ct = ri % C, ci % C
    causal = rt - ct + (ri >= C2).astype(jnp.int32) > 0
    eye = (lax.broadcasted_iota(jnp.int32, (C2, C2), 0) == lax.broadcasted_iota(jnp.int32, (C2, C2), 1))
    eye_f = eye.astype(F32)
    si = lax.broadcasted_iota(jnp.int32, (C2, C2), 0)
    sj = lax.broadcasted_iota(jnp.int32, (C2, C2), 1)
    merge_masks = []
    s = 1
    while s < C:
        merge_masks.append(jnp.logical_and(si // (2 * s) == sj // (2 * s),
                                           jnp.logical_and(si % (2 * s) >= s, sj % (2 * s) < s)))
        s *= 2
    zeros_sq = jnp.zeros((C2, C2), BF16)

    def chunk_terms(chunks):
        n = range(len(chunks))
        a_t, r_t, v_s, aa, lhs_top, p_col = [], [], [], [], [], []
        for i, c in enumerate(chunks):
            rows = slice(c * C, (c + 1) * C)
            lw = lw_ref[rows, :]
            r = r_ref[rows, :].astype(F32)
            k = k_ref[rows, :].astype(F32)
            kk = kk_ref[rows, :].astype(F32)
            b = b_ref[rows, :].astype(F32)
            cum = jnp.dot(tri, lw, precision=HIGHEST, preferred_element_type=F32)
            tot = cum[C - 1:C, :]
            e_neg = jnp.exp(-cum)
            r_t.append(_stack_heads(r * jnp.exp(cum), m0).astype(BF16))
            a_t.append(_stack_heads(-kk * jnp.exp(cum - lw), m0).astype(BF16))
            bk_t = jnp.concatenate([_stack_heads(b * e_neg, m0).T, _stack_heads(k * e_neg, m0).T], axis=1).astype(BF16)
            e_end = jnp.exp(tot - cum)
            b_h = _stack_heads(b * e_end, m0)
            k_h = _stack_heads(k * e_end, m0)
            lhs_top.append(jnp.concatenate([b_h.T, k_h.T], axis=1).astype(BF16))
            v_s.append(_stack_heads(v_ref[rows, :], m0))
            p_col.append(jnp.sum(jnp.where(eye, jnp.exp(tot), 0.0), axis=1, keepdims=True))
            full = _dot(jnp.concatenate([a_t[i], r_t[i]], axis=0), bk_t)
            aa.append(jnp.where(causal, full, 0.0))
        yield

        a_ab = [aa[i][:C2, :C2] for i in n]
        t_inv = [eye_f + jnp.where(merge_masks[0], a_ab[i], 0.0) for i in n]
        for mask in merge_masks[1:]:
            tb = [t_inv[i].astype(BF16) for i in n]
            half = [_dot(tb[i], jnp.where(mask, a_ab[i], 0.0).astype(BF16)).astype(BF16) for i in n]
            yield
            t_inv = [t_inv[i] + _dot(half[i], tb[i]) for i in n]
            yield
        akv = [_dot(aa[i][:C2, C2:].astype(BF16), v_s[i]).astype(BF16) for i in n]
        yield
        au = [_dot(t_inv[i].astype(BF16), jnp.concatenate([a_t[i], akv[i]], axis=1)).astype(BF16)
              for i in n]
        yield
        big = []
        for i in n:
            rhs = jnp.concatenate([au[i], jnp.concatenate([zeros_sq, v_s[i]], axis=1)], axis=0)
            lhs = jnp.concatenate([lhs_top[i], aa[i][C2:, :].astype(BF16)], axis=0)
            big.append(_dot(lhs, rhs))
        qg = [jnp.concatenate([big[i][C2:, :C2] + r_t[i].astype(F32), big[i][:C2, :C2]], axis=0).astype(BF16)
              for i in n]
        return qg, big, p_col

    blk2 = jnp.tile(((lax.broadcasted_iota(jnp.int32, (LANES, LANES), 0) < HEAD_DIM)
                     == (lax.broadcasted_iota(jnp.int32, (LANES, LANES), 1) < HEAD_DIM)).astype(BF16), (2, 1))

    def head_sum(x):
        hi = x.astype(BF16)
        lo = (x - hi.astype(F32)).astype(BF16)
        return _dot(jnp.concatenate([hi, lo], axis=1), blk2)

    def state_steps(st, chunks, terms):
        qg, big, p_col = terms
        for i, c in enumerate(chunks):
            res = _dot(qg[i], st[0].astype(BF16))
            o_st = res[:C2] + big[i][C2:, C2:]
            os_ref[c * C:(c + 1) * C, :] = o_st[:C] + o_st[C:]
            st[0] = p_col[i] * st[0] + res[C2:] + big[i][:C2, C2:]
            yield
        rows = slice(chunks[0] * C, (chunks[-1] + 1) * C)
        o = os_ref[rows, :]
        mean = head_sum(o) * (1.0 / HEAD_DIM)
        cen = o - mean
        var = head_sum(cen * cen) * (1.0 / HEAD_DIM)
        y = cen * lax.rsqrt(var + RW_LNX_EPS) * lng_ref[...] + lnb_ref[...]
        r = r_ref[rows, :].astype(F32)
        k = k_ref[rows, :].astype(F32)
        bonus = head_sum(r * k * rk_ref[...]) * v_ref[rows, :].astype(F32)
        o_ref[rows, :] = (y + bonus).astype(BF16)
        yield

    st = [st_ref[...]]
    n_chunks = tt // C
    sizes = [group] * (n_chunks // group) + ([n_chunks % group] if n_chunks % group else [])
    if sizes[-1] > 1:
        sizes[-1:] = [sizes[-1] - sizes[-1] // 2, sizes[-1] // 2]
    pending = iter(())
    g0 = 0
    for size in sizes:
        chunks = list(range(g0, g0 + size))
        g0 += size
        gen = chunk_terms(chunks)
        while True:
            try:
                next(gen)
            except StopIteration as done:
                terms = done.value
                break
            next(pending, None)
            next(pending, None)
        for _ in pending:
            pass
        pending = state_steps(st, chunks, terms)
    for _ in pending:
        pass
    st_ref[...] = st[0]


def _rw_scan(r, lw, k, v, kk, b, r_k, lnx_g, lnx_b, *, B, T, D, tt):
    hp = D // LANES
    blk = pl.BlockSpec((None, tt, LANES), lambda bi, h, t: (bi, t, h))
    vec = pl.BlockSpec((1, LANES), lambda bi, h, t: (0, h))
    r3 = lambda t: t.reshape(B, T, D)
    out = pl.pallas_call(
        functools.partial(_rw_scan_kernel, tt=tt, group=RW_GROUP),
        grid=(B, hp, T // tt),
        in_specs=[blk] * 6 + [vec] * 3,
        out_specs=blk,
        out_shape=jax.ShapeDtypeStruct((B, T, D), BF16),
        scratch_shapes=[pltpu.VMEM((LANES, LANES), F32), pltpu.VMEM((tt, LANES), F32)],
        compiler_params=_cparams("parallel", "parallel", "arbitrary"),
        name="rw_scan",
    )(r3(r), r3(lw), r3(k), r3(v), r3(kk), r3(b), r_k.reshape(1, D), lnx_g.reshape(1, D), lnx_b.reshape(1, D))
    return out.reshape(B * T, D)


def _diff_lambda_init(layer):
    return 0.8 - 0.6 * math.exp(-0.3 * layer)


def _pick_tile(n, pref):
    t = min(n, pref)
    while n % t:
        t //= 2
    return t


def kernel(x, c, norm_g, ada_w, ada_b, ffn_w_in, ffn_w_out, da_w_qkv, da_w_o, da_q_norm_g, da_k_norm_g, da_lambda, da_subln_g, rw_mu, rw_w_rkv, rw_w_o, rw_decay_w0, rw_decay_w1, rw_decay_w2, rw_iclr_a0, rw_iclr_a1, rw_iclr_a2, rw_gate_g1, rw_gate_g2, rw_k_k, rw_k_a, rw_r_k, rw_lnx_g, rw_lnx_b, sw_w_qkv, sw_w_o, sw_q_norm_g, sw_k_norm_g, sw_sinks):
    B, T, D = x.shape
    L = norm_g.shape[0]
    tm = _pick_tile(T, 2 * PROJ_ROWS)
    tpb = T // tm
    kw = dict(tm=tm, tiles_per_batch=tpb)
    tf = _pick_tile(T, 2 * FFN_ROWS)
    fkw = dict(tm=tf, tiles_per_batch=T // tf)
    scale = HEAD_DIM ** -0.5

    mod = _ada(c, ada_w, ada_b).reshape(L, B, 3, 3, D)
    x2d = x.reshape(B * T, D)
    for i in range(L):
        kind, j = i % 3, i // 3
        x2d = _ffn(x2d, mod[i, :, 0], norm_g[i, 0], ffn_w_in[i, 0].astype(BF16), ffn_w_out[i, 0].astype(BF16), **fkw)
        m2 = mod[i, :, 1]
        if kind == 0:
            qg = jnp.tile(da_q_norm_g[j] * (scale * math.log2(math.e)), D // HEAD_DIM).reshape(1, D)
            kg = jnp.tile(da_k_norm_g[j], D // HEAD_DIM).reshape(1, D)
            qt, k, vt = _da_qkv(x2d, m2, norm_g[i, 1], da_w_qkv[j].astype(BF16), qg, kg, B=B, T=T, tm=tm)
            o = _da_attn(qt, k, vt, da_lambda[j], da_subln_g[j], B=B, T=T, D=D, tq=_pick_tile(T, 512),
                         lambda_init=_diff_lambda_init(i))
            mix = (o, None, m2, da_w_o[j].astype(BF16))
        elif kind == 1:
            p = dict(mu=rw_mu[j], w_rkv=rw_w_rkv[j], decay_w0=rw_decay_w0[j], decay_w1=rw_decay_w1[j],
                     decay_w2=rw_decay_w2[j], iclr_a0=rw_iclr_a0[j], iclr_a1=rw_iclr_a1[j], iclr_a2=rw_iclr_a2[j],
                     gate_g1=rw_gate_g1[j], gate_g2=rw_gate_g2[j], k_k=rw_k_k[j], k_a=rw_k_a[j])
            r, lw, k, v, kk, b, g = _rw_prep(x2d, m2, norm_g[i, 1], p, **kw)
            y = _rw_scan(r, lw, k, v, kk, b, rw_r_k[j], rw_lnx_g[j], rw_lnx_b[j], B=B, T=T, D=D,
                         tt=_pick_tile(T, 4096))
            mix = (y, g, m2, rw_w_o[j].astype(BF16))
        else:
            kv = (sw_w_qkv.shape[-1] - D) // 2
            n_kv = kv // HEAD_DIM
            qg = jnp.tile(sw_q_norm_g[j] * (scale * math.log2(math.e)), D // HEAD_DIM).reshape(1, D)
            dup = lambda w: jnp.repeat(w.reshape(D, n_kv, 1, HEAD_DIM), 2, axis=2).reshape(D, 2 * kv)
            w = sw_w_qkv[j]
            w = jnp.concatenate([w[:, :D], dup(w[:, D:D + kv]), dup(w[:, D + kv:])], axis=1).astype(BF16)
            kg = jnp.tile(sw_k_norm_g[j], 2 * n_kv).reshape(1, 2 * kv)
            q, kt, v = _sw_qkv(x2d, m2, norm_g[i, 1], w, qg, kg, B=B, T=T, **kw)
            x2d = _sw_attn(q, kt, v, sw_sinks[j], x2d, m2, sw_w_o[j].astype(BF16), B=B, T=T,
                           tq=_pick_tile(T, 512), group=(D // HEAD_DIM) // n_kv)
            mix = None
        x2d = _ffn(x2d, mod[i, :, 2], norm_g[i, 2], ffn_w_in[i, 1].astype(BF16), ffn_w_out[i, 1].astype(BF16),
                   mix=mix, **fkw)
    return x2d.reshape(B, T, D)
```

```python
import functools
import math

import jax
import jax.numpy as jnp
from jax import lax
from jax.experimental import pallas as pl
from jax.experimental.pallas import tpu as pltpu

F32 = jnp.float32
BF16 = jnp.bfloat16
HIGHEST = lax.Precision.HIGHEST

HEAD_DIM = 64
LANES = 128
NORM_EPS = 1e-6
SUBLN_EPS = 1e-5
RW_LNX_EPS = 64e-5
SW_WINDOW = 128
N_MOD = 9
PROJ_ROWS = 512
FFN_ROWS = 512
RW_GROUP = 16
RW_CHUNK = 64
VMEM_LIMIT = 56 * 1024 * 1024


def _cparams(*sem):
    return pltpu.CompilerParams(dimension_semantics=sem, vmem_limit_bytes=VMEM_LIMIT)


def _sigmoid(x):
    return 1.0 / (1.0 + jnp.exp(-x))


def _dot(a, b):
    return jnp.dot(a, b, preferred_element_type=F32)


def _dot_nt(a, b):
    return lax.dot_general(a, b, (((1,), (1,)), ((), ())), preferred_element_type=F32)


def _norm_mod(x, g, shift, scale):
    ms = jnp.mean(x * x, axis=-1, keepdims=True)
    return (x * lax.rsqrt(ms + NORM_EPS) * g) * (1.0 + scale) + shift


def _sub_tiles(n):
    step = min(n, PROJ_ROWS)
    return [slice(r0, r0 + step) for r0 in range(0, n, step)]


def _const_spec(shape):
    nd = len(shape)
    return pl.BlockSpec(shape, lambda *_: (0,) * nd, pipeline_mode=pl.Buffered(1))


def _group_mats(d):
    grp = jnp.arange(d) // HEAD_DIM
    gsum = (grp[:, None] == jnp.arange(LANES)[None, :]).astype(BF16)
    return gsum, jnp.concatenate([gsum.T, gsum.T], axis=0)


def _group_bcast(v, gexp):
    hi = v.astype(BF16)
    lo = (v - hi.astype(F32)).astype(BF16)
    return _dot(jnp.concatenate([hi, lo], axis=1), gexp)


def _ada_kernel(c_ref, w_ref, b_ref, o_ref):
    c = c_ref[...]
    cond = c * _sigmoid(c)
    o_ref[...] = jnp.dot(cond, w_ref[...], precision=HIGHEST, preferred_element_type=F32) + b_ref[...]


def _ada(c, ada_w, ada_b):
    L, D, ND = ada_w.shape
    B = c.shape[0]
    cols = 3 * D if ND % (3 * D) == 0 else D
    return pl.pallas_call(
        _ada_kernel,
        grid=(L, ND // cols),
        in_specs=[pl.BlockSpec((B, D), lambda l, j: (0, 0)),
                  pl.BlockSpec((None, D, cols), lambda l, j: (l, 0, j)),
                  pl.BlockSpec((None, 1, cols), lambda l, j: (l, 0, j))],
        out_specs=pl.BlockSpec((None, B, cols), lambda l, j: (l, 0, j)),
        out_shape=jax.ShapeDtypeStruct((L, B, ND), F32),
        compiler_params=_cparams("parallel", "parallel"),
        name="ada_mod",
    )(c, ada_w, ada_b.reshape(L, 1, ND))


def _ffn_kernel(*refs, d_ff, chunks, rows, mixer):
    if mixer is None:
        x_ref, mod_ref, g_ref, win_ref, wout_ref, o_ref = refs
    elif mixer == "plain":
        x_ref, mod_ref, g_ref, win_ref, wout_ref, y_ref, mixmod_ref, wo_ref, o_ref = refs
    else:
        x_ref, mod_ref, g_ref, win_ref, wout_ref, y_ref, ygate_ref, mixmod_ref, wo_ref, o_ref = refs
    tm = x_ref.shape[0]
    subs = list(range(0, tm, rows))

    def residual_in(r0):
        x = x_ref[r0:r0 + rows, :]
        if mixer is None:
            return x
        y = y_ref[r0:r0 + rows, :]
        if mixer == "gated":
            y = (y.astype(F32) * ygate_ref[r0:r0 + rows, :].astype(F32)).astype(BF16)
        return x + mixmod_ref[2:3, :] * _dot(y, wo_ref[...])

    xs, hidden = {}, {}

    def prepare(r0):
        xs[r0] = residual_in(r0)
        hidden[r0] = _norm_mod(xs[r0], g_ref[...], mod_ref[0:1, :], mod_ref[1:2, :]).astype(BF16)

    def gate_up(r0, c0, c1):
        h = hidden[r0]
        return _dot(h, win_ref[:, c0:c1]), _dot(h, win_ref[:, d_ff + c0:d_ff + c1])

    jobs = [(r0, c0, c1) for r0 in subs for c0, c1 in chunks]
    prepare(subs[0])
    pending = gate_up(*jobs[0])
    acc = None
    for n, (r0, c0, c1) in enumerate(jobs):
        gate, up = pending
        if n + 1 < len(jobs):
            if jobs[n + 1][0] not in hidden:
                prepare(jobs[n + 1][0])
            pending = gate_up(*jobs[n + 1])
        if n == 0:
            for r in subs[1:]:
                if r not in hidden:
                    prepare(r)
        act = (gate * _sigmoid(gate) * up).astype(BF16)
        y = _dot(act, wout_ref[c0:c1, :])
        acc = y if c0 == 0 else acc + y
        if c1 == d_ff:
            o_ref[r0:r0 + rows, :] = xs[r0] + (0.5 * mod_ref[2:3, :]) * acc


def _ffn_chunks(d_ff):
    last = 2 * LANES
    if d_ff % LANES or d_ff < 4 * last:
        return [(0, d_ff)]
    mid = (d_ff // 2) // LANES * LANES
    return [(0, mid), (mid, d_ff - last), (d_ff - last, d_ff)]


def _ffn(x2d, mod3, g, w_in, w_out, *, tm, tiles_per_batch, mix=None):
    N, D = x2d.shape
    d_ff = w_out.shape[0]
    tile = pl.BlockSpec((tm, D), lambda i: (i, 0))
    modspec = pl.BlockSpec((None, 3, D), lambda i: (i // tiles_per_batch, 0, 0))
    ins = [x2d, mod3, g.reshape(1, D), w_in, w_out]
    specs = [tile, modspec, _const_spec((1, D)), _const_spec(w_in.shape), _const_spec(w_out.shape)]
    mixer = None
    if mix is not None:
        y, y_gate, mixmod, w_o = mix
        mixer = "plain" if y_gate is None else "gated"
        ins += [y] + ([] if y_gate is None else [y_gate]) + [mixmod, w_o]
        specs += [tile] + ([] if y_gate is None else [tile]) + [modspec, _const_spec(w_o.shape)]
    kern = functools.partial(_ffn_kernel, d_ff=d_ff, chunks=_ffn_chunks(d_ff), rows=min(tm, FFN_ROWS), mixer=mixer)
    return pl.pallas_call(
        kern,
        grid=(N // tm,),
        in_specs=specs,
        out_specs=tile,
        out_shape=jax.ShapeDtypeStruct((N, D), F32),
        compiler_params=_cparams("parallel"),
        name="ffn_half" if mixer is None else "ffn_half_" + mixer,
    )(*ins)


def _group_rms_scale(y, gsum, gexp):
    ss = _dot((y * y).astype(BF16), gsum)
    return _group_bcast(lax.rsqrt(ss * (1.0 / HEAD_DIM) + NORM_EPS), gexp)


def _da_qkv_kernel(x_ref, mod_ref, g_ref, w_ref, gsum_ref, gexp_ref, qg_ref, kg_ref, qt_ref, k_ref, vt_ref):
    D = x_ref.shape[1]
    for rows in _sub_tiles(x_ref.shape[0]):
        h = _norm_mod(x_ref[rows, :], g_ref[...], mod_ref[0:1, :], mod_ref[1:2, :]).astype(BF16)
        q = _dot(h, w_ref[:, 0:D])
        k = _dot(h, w_ref[:, D:2 * D])
        v = _dot(h, w_ref[:, 2 * D:3 * D])
        vt_ref[:, rows] = v.T.astype(BF16)
        qt_ref[:, rows] = (q * _group_rms_scale(q, gsum_ref[...], gexp_ref[...]) * qg_ref[...]).T.astype(BF16)
        k_ref[rows, :] = (k * _group_rms_scale(k, gsum_ref[...], gexp_ref[...]) * kg_ref[...]).astype(BF16)


def _da_qkv(x2d, mod3, g, w_qkv, q_gain, k_gain, *, B, T, tm):
    N, D = x2d.shape
    tpb = T // tm
    gsum, gexp = _group_mats(D)
    tile = pl.BlockSpec((tm, D), lambda i: (i, 0))
    transposed = pl.BlockSpec((None, D, tm), lambda i: (i // tpb, 0, i % tpb))
    return pl.pallas_call(
        _da_qkv_kernel,
        grid=(N // tm,),
        in_specs=[tile,
                  pl.BlockSpec((None, 3, D), lambda i: (i // tpb, 0, 0)),
                  _const_spec((1, D)), _const_spec(w_qkv.shape), _const_spec(gsum.shape),
                  _const_spec(gexp.shape), _const_spec((1, D)), _const_spec((1, D))],
        out_specs=[transposed, tile, transposed],
        out_shape=[jax.ShapeDtypeStruct((B, D, T), BF16), jax.ShapeDtypeStruct((N, D), BF16),
                   jax.ShapeDtypeStruct((B, D, T), BF16)],
        compiler_params=_cparams("parallel"),
        name="da_qkv",
    )(x2d, mod3, g.reshape(1, D), w_qkv, gsum, gexp, q_gain, k_gain)


def _da_attn_kernel(lam_ref, sg_ref, qt_ref, k_ref, vt_ref, o_ref, acc_ref, sa_ref, sb_ref, *, tq):
    nq = k_ref.shape[0] // tq
    dim = lax.broadcasted_iota(jnp.int32, (LANES, 1), 0)
    lam = lam_ref[...]
    lambda_init = lam[4:5, 0:1]
    lam_full = (jnp.exp(jnp.sum(lam[0:1] * lam[1:2], axis=1, keepdims=True))
                - jnp.exp(jnp.sum(lam[2:3] * lam[3:4], axis=1, keepdims=True)) + lambda_init)
    kpos = lax.broadcasted_iota(jnp.int32, (tq, tq), 0)
    qpos = lax.broadcasted_iota(jnp.int32, (tq, tq), 1)
    causal = kpos <= qpos
    bufs = (sa_ref, sb_ref)
    q_maps = {}

    def q_masked(qi):
        if qi not in q_maps:
            qt = qt_ref[:, qi * tq:(qi + 1) * tq]
            zero = jnp.zeros_like(qt)
            q_maps[qi] = (jnp.where(dim < HEAD_DIM, qt, zero), jnp.where(dim >= HEAD_DIM, qt, zero))
        return q_maps[qi]

    def scores_into(s_ref, qi, j):
        k = k_ref[j * tq:(j + 1) * tq, :]
        for c in range(2):
            s_ref[c] = _dot(k, q_masked(qi)[c])

    def consume(s_ref, acc, qi, j, carry):
        vt = vt_ref[:, j * tq:(j + 1) * tq]
        out = []
        for c in range(2):
            s = s_ref[c]
            if j == qi:
                s = jnp.where(causal, s, -jnp.inf)
            m_cur = jnp.max(s, axis=0, keepdims=True)
            if j == 0:
                m_new = m_cur
                p = jnp.exp2(s - m_new)
                l_new = jnp.sum(p, axis=0, keepdims=True)
                acc[c] = _dot(vt, p.astype(BF16))
            else:
                m_prev, l_prev = carry[2 * c], carry[2 * c + 1]
                m_new = jnp.maximum(m_prev, m_cur)
                alpha = jnp.exp2(m_prev - m_new)
                p = jnp.exp2(s - m_new)
                l_new = alpha * l_prev + jnp.sum(p, axis=0, keepdims=True)
                acc[c] = alpha * acc[c] + _dot(vt, p.astype(BF16))
            out += [m_new, l_new]
        return tuple(out)

    def finish(acc, qi, carry):
        ot = acc[0] / carry[1] - lam_full * (acc[1] / carry[3])
        ms = jnp.mean(ot * ot, axis=0, keepdims=True)
        ot = ot * lax.rsqrt(ms + SUBLN_EPS)
        o_ref[qi * tq:(qi + 1) * tq, :] = (ot.T * sg_ref[...] * (1.0 - lambda_init)).astype(BF16)

    tasks = [(qi, j) for qi in range(nq) for j in range(qi + 1)]
    scores_into(bufs[0], *tasks[0])
    carry = None
    for t, (qi, j) in enumerate(tasks):
        if t + 1 < len(tasks):
            scores_into(bufs[(t + 1) % 2], *tasks[t + 1])
        acc = acc_ref.at[qi % 2]
        carry = consume(bufs[t % 2], acc, qi, j, carry)
        if j == qi:
            finish(acc, qi, carry)


def _da_attn(qt, k, vt, lambdas, subln_g, *, B, T, D, tq, lambda_init):
    H = D // LANES
    k3 = k.reshape(B, T, D)
    seq = pl.BlockSpec((None, T, LANES), lambda b, h: (b, 0, h))
    seq_t = pl.BlockSpec((None, LANES, T), lambda b, h: (b, h, 0))
    out = pl.pallas_call(
        functools.partial(_da_attn_kernel, tq=tq),
        grid=(B, H),
        in_specs=[pl.BlockSpec((5, HEAD_DIM), lambda b, h: (0, 0)),
                  pl.BlockSpec((1, LANES), lambda b, h: (0, 0)),
                  seq_t, seq, seq_t],
        out_specs=seq,
        out_shape=jax.ShapeDtypeStruct((B, T, D), BF16),
        scratch_shapes=[pltpu.VMEM((2, 2, LANES, tq), F32),
                        pltpu.VMEM((2, tq, tq), F32), pltpu.VMEM((2, tq, tq), F32)],
        compiler_params=_cparams("parallel", "parallel"),
        name="da_attn",
    )(jnp.concatenate([lambdas, jnp.full((1, HEAD_DIM), lambda_init, F32)], axis=0),
      subln_g.reshape(1, LANES), qt, k3, vt)
    return out.reshape(B * T, D)


def _sw_qkv_kernel(x_ref, mod_ref, g_ref, w_ref, gsum_ref, gexp_ref, ksum_ref, kexp_ref, qg_ref, kg_ref,
                   q_ref, k_ref, v_ref):
    D = x_ref.shape[1]
    kv = k_ref.shape[1]
    for rows in _sub_tiles(x_ref.shape[0]):
        h = _norm_mod(x_ref[rows, :], g_ref[...], mod_ref[0:1, :], mod_ref[1:2, :]).astype(BF16)
        q = _dot(h, w_ref[:, 0:D])
        k = _dot(h, w_ref[:, D:D + kv])
        v_ref[rows, :] = _dot(h, w_ref[:, D + kv:D + 2 * kv]).astype(BF16)
        q_ref[rows, :] = (q * _group_rms_scale(q, gsum_ref[...], gexp_ref[...]) * qg_ref[...]).astype(BF16)
        k_ref[rows, :] = (k * _group_rms_scale(k, ksum_ref[...], kexp_ref[...]) * kg_ref[...]).astype(BF16)


def _sw_qkv(x2d, mod3, g, w_qkv, q_gain, k_gain, *, tm, tiles_per_batch):
    N, D = x2d.shape
    kv = (w_qkv.shape[1] - D) // 2
    gsum, gexp = _group_mats(D)
    ksum, kexp = _group_mats(kv)
    tile = pl.BlockSpec((tm, D), lambda i: (i, 0))
    kvtile = pl.BlockSpec((tm, kv), lambda i: (i, 0))
    return pl.pallas_call(
        _sw_qkv_kernel,
        grid=(N // tm,),
        in_specs=[tile, pl.BlockSpec((None, 3, D), lambda i: (i // tiles_per_batch, 0, 0)),
                  _const_spec((1, D)), _const_spec(w_qkv.shape), _const_spec(gsum.shape),
                  _const_spec(gexp.shape), _const_spec(ksum.shape), _const_spec(kexp.shape),
                  _const_spec((1, D)), _const_spec((1, kv))],
        out_specs=[tile, kvtile, kvtile],
        out_shape=[jax.ShapeDtypeStruct((N, D), BF16), jax.ShapeDtypeStruct((N, kv), BF16),
                   jax.ShapeDtypeStruct((N, kv), BF16)],
        compiler_params=_cparams("parallel"),
        name="sw_qkv",
    )(x2d, mod3, g.reshape(1, D), w_qkv, gsum, gexp, ksum, kexp, q_gain, k_gain)


def _sw_attn_kernel(sink_ref, q_ref, k_ref, v_ref, x_ref, mod_ref, wo_ref, o_ref, att_ref, *, tq, group):
    i = pl.program_id(1)
    D = q_ref.shape[1]
    W = SW_WINDOW
    n_kv = (D // HEAD_DIM) // group
    tiles_per_kv = group // 2
    lane = lax.broadcasted_iota(jnp.int32, (1, LANES), 1)
    first = lane < HEAD_DIM
    jobs = [(sub, kvh) for sub in range(tq // W) for kvh in range(n_kv)]

    def band_start(sub):
        return pl.multiple_of(jnp.maximum(i * tq + (sub - 1) * W, 0), W)

    def scores(sub, kvh):
        k = k_ref[pl.ds(band_start(sub), 2 * W), kvh * LANES:(kvh + 1) * LANES]
        parts = []
        for b in range(tiles_per_kv):
            blk = kvh * tiles_per_kv + b
            q = q_ref[sub * W:(sub + 1) * W, blk * LANES:(blk + 1) * LANES]
            zero = jnp.zeros_like(q)
            parts += [jnp.where(first, q, zero), jnp.where(first, zero, q)]
        return _dot_nt(jnp.concatenate(parts, axis=0), k)

    pending = scores(*jobs[0])
    for n, (sub, kvh) in enumerate(jobs):
        s_all = pending
        if n + 1 < len(jobs):
            pending = scores(*jobs[n + 1])
        qpos = i * tq + sub * W + lax.broadcasted_iota(jnp.int32, (W, 2 * W), 0)
        kpos = band_start(sub) + lax.broadcasted_iota(jnp.int32, (W, 2 * W), 1)
        valid = jnp.logical_and(kpos <= qpos, kpos > qpos - W)
        probs, inv = [], []
        for g in range(group):
            s = jnp.where(valid, s_all[g * W:(g + 1) * W], -jnp.inf)
            snk = sink_ref[kvh * group + g] * math.log2(math.e)
            m = jnp.maximum(jnp.max(s, axis=1, keepdims=True), snk)
            p = jnp.exp2(s - m)
            inv.append(1.0 / (jnp.sum(p, axis=1, keepdims=True) + jnp.exp2(snk - m)))
            probs.append(p.astype(BF16))
        v = v_ref[pl.ds(band_start(sub), 2 * W), kvh * LANES:(kvh + 1) * LANES]
        o_all = _dot(jnp.concatenate(probs, axis=0), v)
        for b in range(tiles_per_kv):
            blk = kvh * tiles_per_kv + b
            o0 = o_all[2 * b * W:(2 * b + 1) * W] * inv[2 * b]
            o1 = o_all[(2 * b + 1) * W:(2 * b + 2) * W] * inv[2 * b + 1]
            att_ref[sub * W:(sub + 1) * W, blk * LANES:(blk + 1) * LANES] = jnp.where(first, o0, o1).astype(BF16)
    o_ref[...] = x_ref[...] + mod_ref[2:3, :] * _dot(att_ref[...], wo_ref[...])


def _sw_attn(q, kd, vd, sinks, x2d, mod3, w_o, *, B, T, tq, group):
    N, D = x2d.shape
    nq = T // tq
    kvd = kd.shape[1]
    tile = pl.BlockSpec((tq, D), lambda b, i: (b * nq + i, 0))
    seq = pl.BlockSpec((None, T, kvd), lambda b, i: (b, 0, 0))
    return pl.pallas_call(
        functools.partial(_sw_attn_kernel, tq=tq, group=group),
        grid=(B, nq),
        in_specs=[pl.BlockSpec(memory_space=pltpu.SMEM), tile, seq, seq, tile,
                  pl.BlockSpec((None, 3, D), lambda b, i: (b, 0, 0)), _const_spec(w_o.shape)],
        out_specs=tile,
        out_shape=jax.ShapeDtypeStruct((N, D), F32),
        scratch_shapes=[pltpu.VMEM((tq, D), BF16)],
        compiler_params=_cparams("parallel", "arbitrary"),
        name="sw_attn",
    )(sinks, q, kd.reshape(B, T, kvd), vd.reshape(B, T, kvd), x2d, mod3, w_o)


def _rw_prep_kernel(x_ref, xp_ref, mod_ref, g_ref, mu_ref, wrkv_ref, w0_ref, w1_ref, w2_ref, a0_ref, a1_ref,
                    a2_ref, g1_ref, g2_ref, kk_ref, ka_ref, gsum_ref, gexp_ref,
                    r_out, lw_out, k_out, v_out, kk_out, b_out, g_out, *, tiles_per_batch):
    shift, scale, gn = mod_ref[0:1, :], mod_ref[1:2, :], g_ref[...]
    for rows in _sub_tiles(x_ref.shape[0]):
        h = _norm_mod(x_ref[rows, :], gn, shift, scale)
        if rows.start == 0:
            hp = _norm_mod(xp_ref[...], gn, shift, scale)[7:8, :]
            hp = jnp.where(pl.program_id(0) % tiles_per_batch == 0, jnp.zeros_like(hp), hp)
        else:
            hp = _norm_mod(x_ref[rows.start - 8:rows.start, :], gn, shift, scale)[7:8, :]
        row = lax.broadcasted_iota(jnp.int32, (rows.stop - rows.start, 1), 0)
        xx = jnp.where(row == 0, hp, pltpu.roll(h, 1, axis=0)) - h

        def mixed(n):
            return (h + xx * mu_ref[n:n + 1, :]).astype(BF16)

        d1 = _dot(mixed(3), w1_ref[...])
        a1 = _dot(mixed(4), a1_ref[...])
        g1 = _dot(mixed(5), g1_ref[...])
        r = _dot(mixed(0), wrkv_ref[0])
        k = _dot(mixed(1), wrkv_ref[1])
        v = _dot(mixed(2), wrkv_ref[2])
        d = _dot(jnp.tanh(d1).astype(BF16), w2_ref[...]) + w0_ref[...]
        a = _sigmoid(a0_ref[...] + _dot(a1.astype(BF16), a2_ref[...]))
        g_out[rows, :] = _dot(_sigmoid(g1).astype(BF16), g2_ref[...]).astype(BF16)
        z = -d
        softplus = jnp.maximum(z, 0.0) + jnp.log(1.0 + jnp.exp(-jnp.abs(z)))
        lw_out[rows, :] = -jnp.exp(-softplus - 0.5)
        kk = k * kk_ref[...]
        ss = _dot((kk * kk).astype(BF16), gsum_ref[...])
        kk = kk * _group_bcast(1.0 / jnp.maximum(jnp.sqrt(ss), 1e-12), gexp_ref[...])
        r_out[rows, :] = r.astype(BF16)
        k_out[rows, :] = (k * (1.0 + (a - 1.0) * ka_ref[...])).astype(BF16)
        v_out[rows, :] = v.astype(BF16)
        kk_out[rows, :] = kk.astype(BF16)
        b_out[rows, :] = (kk * a).astype(BF16)


def _pad_to(w, axis, n):
    pad = [(0, 0)] * w.ndim
    pad[axis] = (0, n - w.shape[axis])
    return jnp.pad(w, pad)


def _round_up(n, m):
    return -(-n // m) * m


def _rw_prep(x2d, mod3, g, p, *, tm, tiles_per_batch):
    N, D = x2d.shape
    gsum, gexp = _group_mats(D)
    row = lambda t: t.reshape(1, D)
    lora = lambda w1, w2: (_pad_to(w1, 1, _round_up(w1.shape[1], LANES)).astype(BF16),
                           _pad_to(w2, 0, _round_up(w2.shape[0], LANES)).astype(BF16))
    w1, w2 = lora(p["decay_w1"], p["decay_w2"])
    a1, a2 = lora(p["iclr_a1"], p["iclr_a2"])
    g1, g2 = lora(p["gate_g1"], p["gate_g2"])
    consts = [row(g), p["mu"], p["w_rkv"].astype(BF16), row(p["decay_w0"]), w1, w2, row(p["iclr_a0"]), a1, a2,
              g1, g2, row(p["k_k"]), row(p["k_a"]), gsum, gexp]
    tile = pl.BlockSpec((tm, D), lambda i: (i, 0))
    outs = [jax.ShapeDtypeStruct((N, D), BF16)] * 7
    outs[1] = jax.ShapeDtypeStruct((N, D), F32)
    return pl.pallas_call(
        functools.partial(_rw_prep_kernel, tiles_per_batch=tiles_per_batch),
        grid=(N // tm,),
        in_specs=[tile,
                  pl.BlockSpec((8, D), lambda i: (jnp.maximum(i * (tm // 8) - 1, 0), 0)),
                  pl.BlockSpec((None, 3, D), lambda i: (i // tiles_per_batch, 0, 0))]
                 + [_const_spec(c.shape) for c in consts],
        out_specs=[tile] * 7,
        out_shape=outs,
        compiler_params=_cparams("parallel"),
        name="rw_prep",
    )(x2d, x2d, mod3, *consts)


def _stack_heads(x, m0):
    z = jnp.zeros_like(x)
    return jnp.concatenate([jnp.where(m0, x, z), jnp.where(m0, z, x)], axis=0)


def _rw_scan_kernel(r_ref, lw_ref, k_ref, v_ref, kk_ref, b_ref, rk_ref, lng_ref, lnb_ref, o_ref, st_ref, os_ref,
                    *, tt, group):
    C = RW_CHUNK
    C2 = 2 * C

    @pl.when(pl.program_id(2) == 0)
    def _():
        st_ref[...] = jnp.zeros_like(st_ref)

    lane = lax.broadcasted_iota(jnp.int32, (1, LANES), 1)
    m0 = lane < HEAD_DIM
    ti = lax.broadcasted_iota(jnp.int32, (C, C), 0)
    tj = lax.broadcasted_iota(jnp.int32, (C, C), 1)
    tri = (ti >= tj).astype(F32)
    ri = lax.broadcasted_iota(jnp.int32, (2 * C2, 2 * C2), 0)
    ci = lax.broadcasted_iota(jnp.int32, (2 * C2, 2 * C2), 1)
    rt, ct = ri % C, ci % C
    causal = rt - ct + (ri >= C2).astype(jnp.int32) > 0
    eye = (lax.broadcasted_iota(jnp.int32, (C2, C2), 0) == lax.broadcasted_iota(jnp.int32, (C2, C2), 1))
    eye_f = eye.astype(F32)
    si = lax.broadcasted_iota(jnp.int32, (C2, C2), 0)
    sj = lax.broadcasted_iota(jnp.int32, (C2, C2), 1)
    merge_masks = []
    s = 1
    while s < C:
        merge_masks.append(jnp.logical_and(si // (2 * s) == sj // (2 * s),
                                           jnp.logical_and(si % (2 * s) >= s, sj % (2 * s) < s)))
        s *= 2
    zeros_sq = jnp.zeros((C2, C2), BF16)

    def chunk_terms(chunks):
        n = range(len(chunks))
        a_t, r_t, v_s, aa, lhs_top, p_col = [], [], [], [], [], []
        for i, c in enumerate(chunks):
            rows = slice(c * C, (c + 1) * C)
            lw = lw_ref[rows, :]
            r = r_ref[rows, :].astype(F32)
            k = k_ref[rows, :].astype(F32)
            kk = kk_ref[rows, :].astype(F32)
            b = b_ref[rows, :].astype(F32)
            cum = jnp.dot(tri, lw, precision=HIGHEST, preferred_element_type=F32)
            tot = cum[C - 1:C, :]
            e_neg = jnp.exp(-cum)
            r_t.append(_stack_heads(r * jnp.exp(cum), m0).astype(BF16))
            a_t.append(_stack_heads(-kk * jnp.exp(cum - lw), m0).astype(BF16))
            b_t = _stack_heads(b * e_neg, m0).astype(BF16)
            k_t = _stack_heads(k * e_neg, m0).astype(BF16)
            e_end = jnp.exp(tot - cum)
            b_h = _stack_heads(b * e_end, m0)
            k_h = _stack_heads(k * e_end, m0)
            lhs_top.append(jnp.concatenate([b_h.T, k_h.T], axis=1).astype(BF16))
            v_s.append(_stack_heads(v_ref[rows, :], m0))
            p_col.append(jnp.sum(jnp.where(eye, jnp.exp(tot), 0.0), axis=1, keepdims=True))
            full = _dot_nt(jnp.concatenate([a_t[i], r_t[i]], axis=0), jnp.concatenate([b_t, k_t], axis=0))
            aa.append(jnp.where(causal, full, 0.0))
        yield

        a_ab = [aa[i][:C2, :C2] for i in n]
        t_inv = [eye_f + jnp.where(merge_masks[0], a_ab[i], 0.0) for i in n]
        for mask in merge_masks[1:]:
            tb = [t_inv[i].astype(BF16) for i in n]
            half = [_dot(tb[i], jnp.where(mask, a_ab[i], 0.0).astype(BF16)).astype(BF16) for i in n]
            yield
            t_inv = [t_inv[i] + _dot(half[i], tb[i]) for i in n]
            yield
        akv = [_dot(aa[i][:C2, C2:].astype(BF16), v_s[i]).astype(BF16) for i in n]
        yield
        au = [_dot(t_inv[i].astype(BF16), jnp.concatenate([a_t[i], akv[i]], axis=1)).astype(BF16)
              for i in n]
        yield
        big = []
        for i in n:
            rhs = jnp.concatenate([au[i], jnp.concatenate([zeros_sq, v_s[i]], axis=1)], axis=0)
            lhs = jnp.concatenate([lhs_top[i], aa[i][C2:, :].astype(BF16)], axis=0)
            big.append(_dot(lhs, rhs))
        qg = [jnp.concatenate([big[i][C2:, :C2] + r_t[i].astype(F32), big[i][:C2, :C2]], axis=0).astype(BF16)
              for i in n]
        return qg, big, p_col

    blk2 = jnp.tile(((lax.broadcasted_iota(jnp.int32, (LANES, LANES), 0) < HEAD_DIM)
                     == (lax.broadcasted_iota(jnp.int32, (LANES, LANES), 1) < HEAD_DIM)).astype(BF16), (2, 1))

    def head_sum(x):
        hi = x.astype(BF16)
        lo = (x - hi.astype(F32)).astype(BF16)
        return _dot(jnp.concatenate([hi, lo], axis=1), blk2)

    def state_steps(st, chunks, terms):
        qg, big, p_col = terms
        for i, c in enumerate(chunks):
            res = _dot(qg[i], st[0].astype(BF16))
            o_st = res[:C2] + big[i][C2:, C2:]
            os_ref[c * C:(c + 1) * C, :] = o_st[:C] + o_st[C:]
            st[0] = p_col[i] * st[0] + res[C2:] + big[i][:C2, C2:]
            yield
        rows = slice(chunks[0] * C, (chunks[-1] + 1) * C)
        o = os_ref[rows, :]
        mean = head_sum(o) * (1.0 / HEAD_DIM)
        cen = o - mean
        var = head_sum(cen * cen) * (1.0 / HEAD_DIM)
        y = cen * lax.rsqrt(var + RW_LNX_EPS) * lng_ref[...] + lnb_ref[...]
        r = r_ref[rows, :].astype(F32)
        k = k_ref[rows, :].astype(F32)
        bonus = head_sum(r * k * rk_ref[...]) * v_ref[rows, :].astype(F32)
        o_ref[rows, :] = (y + bonus).astype(BF16)
        yield

    st = [st_ref[...]]
    n_chunks = tt // C
    sizes = [group] * (n_chunks // group) + ([n_chunks % group] if n_chunks % group else [])
    if sizes[-1] > 1:
        sizes[-1:] = [sizes[-1] - sizes[-1] // 2, sizes[-1] // 2]
    pending = iter(())
    g0 = 0
    for size in sizes:
        chunks = list(range(g0, g0 + size))
        g0 += size
        gen = chunk_terms(chunks)
        while True:
            try:
                next(gen)
            except StopIteration as done:
                terms = done.value
                break
            next(pending, None)
            next(pending, None)
        for _ in pending:
            pass
        pending = state_steps(st, chunks, terms)
    for _ in pending:
        pass
    st_ref[...] = st[0]


def _rw_scan(r, lw, k, v, kk, b, r_k, lnx_g, lnx_b, *, B, T, D, tt):
    hp = D // LANES
    blk = pl.BlockSpec((None, tt, LANES), lambda bi, h, t: (bi, t, h))
    vec = pl.BlockSpec((1, LANES), lambda bi, h, t: (0, h))
    r3 = lambda t: t.reshape(B, T, D)
    out = pl.pallas_call(
        functools.partial(_rw_scan_kernel, tt=tt, group=RW_GROUP),
        grid=(B, hp, T // tt),
        in_specs=[blk] * 6 + [vec] * 3,
        out_specs=blk,
        out_shape=jax.ShapeDtypeStruct((B, T, D), BF16),
        scratch_shapes=[pltpu.VMEM((LANES, LANES), F32), pltpu.VMEM((tt, LANES), F32)],
        compiler_params=_cparams("parallel", "parallel", "arbitrary"),
        name="rw_scan",
    )(r3(r), r3(lw), r3(k), r3(v), r3(kk), r3(b), r_k.reshape(1, D), lnx_g.reshape(1, D), lnx_b.reshape(1, D))
    return out.reshape(B * T, D)


def _diff_lambda_init(layer):
    return 0.8 - 0.6 * math.exp(-0.3 * layer)


def _pick_tile(n, pref):
    t = min(n, pref)
    while n % t:
        t //= 2
    return t


def kernel(x, c, norm_g, ada_w, ada_b, ffn_w_in, ffn_w_out, da_w_qkv, da_w_o, da_q_norm_g, da_k_norm_g, da_lambda, da_subln_g, rw_mu, rw_w_rkv, rw_w_o, rw_decay_w0, rw_decay_w1, rw_decay_w2, rw_iclr_a0, rw_iclr_a1, rw_iclr_a2, rw_gate_g1, rw_gate_g2, rw_k_k, rw_k_a, rw_r_k, rw_lnx_g, rw_lnx_b, sw_w_qkv, sw_w_o, sw_q_norm_g, sw_k_norm_g, sw_sinks):
    B, T, D = x.shape
    L = norm_g.shape[0]
    tm = _pick_tile(T, 2 * PROJ_ROWS)
    tpb = T // tm
    kw = dict(tm=tm, tiles_per_batch=tpb)
    tf = _pick_tile(T, 2 * FFN_ROWS)
    fkw = dict(tm=tf, tiles_per_batch=T // tf)
    scale = HEAD_DIM ** -0.5

    mod = _ada(c, ada_w, ada_b).reshape(L, B, 3, 3, D)
    x2d = x.reshape(B * T, D)
    for i in range(L):
        kind, j = i % 3, i // 3
        x2d = _ffn(x2d, mod[i, :, 0], norm_g[i, 0], ffn_w_in[i, 0].astype(BF16), ffn_w_out[i, 0].astype(BF16), **fkw)
        m2 = mod[i, :, 1]
        if kind == 0:
            qg = jnp.tile(da_q_norm_g[j] * (scale * math.log2(math.e)), D // HEAD_DIM).reshape(1, D)
            kg = jnp.tile(da_k_norm_g[j], D // HEAD_DIM).reshape(1, D)
            qt, k, vt = _da_qkv(x2d, m2, norm_g[i, 1], da_w_qkv[j].astype(BF16), qg, kg, B=B, T=T, tm=tm)
            o = _da_attn(qt, k, vt, da_lambda[j], da_subln_g[j], B=B, T=T, D=D, tq=_pick_tile(T, 512),
                         lambda_init=_diff_lambda_init(i))
            mix = (o, None, m2, da_w_o[j].astype(BF16))
        elif kind == 1:
            p = dict(mu=rw_mu[j], w_rkv=rw_w_rkv[j], decay_w0=rw_decay_w0[j], decay_w1=rw_decay_w1[j],
                     decay_w2=rw_decay_w2[j], iclr_a0=rw_iclr_a0[j], iclr_a1=rw_iclr_a1[j], iclr_a2=rw_iclr_a2[j],
                     gate_g1=rw_gate_g1[j], gate_g2=rw_gate_g2[j], k_k=rw_k_k[j], k_a=rw_k_a[j])
            r, lw, k, v, kk, b, g = _rw_prep(x2d, m2, norm_g[i, 1], p, **kw)
            y = _rw_scan(r, lw, k, v, kk, b, rw_r_k[j], rw_lnx_g[j], rw_lnx_b[j], B=B, T=T, D=D,
                         tt=_pick_tile(T, 4096))
            mix = (y, g, m2, rw_w_o[j].astype(BF16))
        else:
            kv = (sw_w_qkv.shape[-1] - D) // 2
            n_kv = kv // HEAD_DIM
            qg = jnp.tile(sw_q_norm_g[j] * (scale * math.log2(math.e)), D // HEAD_DIM).reshape(1, D)
            dup = lambda w: jnp.repeat(w.reshape(D, n_kv, 1, HEAD_DIM), 2, axis=2).reshape(D, 2 * kv)
            w = sw_w_qkv[j]
            w = jnp.concatenate([w[:, :D], dup(w[:, D:D + kv]), dup(w[:, D + kv:])], axis=1).astype(BF16)
            kg = jnp.tile(sw_k_norm_g[j], 2 * n_kv).reshape(1, 2 * kv)
            q, k, v = _sw_qkv(x2d, m2, norm_g[i, 1], w, qg, kg, **kw)
            x2d = _sw_attn(q, k, v, sw_sinks[j], x2d, m2, sw_w_o[j].astype(BF16), B=B, T=T,
                           tq=_pick_tile(T, 1024), group=(D // HEAD_DIM) // n_kv)
            mix = None
        x2d = _ffn(x2d, mod[i, :, 2], norm_g[i, 2], ffn_w_in[i, 1].astype(BF16), ffn_w_out[i, 1].astype(BF16),
                   mix=mix, **fkw)
    return x2d.reshape(B, T, D)
```

```python
import functools
import math

import jax
import jax.numpy as jnp
from jax import lax
from jax.experimental import pallas as pl
from jax.experimental.pallas import tpu as pltpu

F32 = jnp.float32
BF16 = jnp.bfloat16
HIGHEST = lax.Precision.HIGHEST

HEAD_DIM = 64
LANES = 128
NORM_EPS = 1e-6
SUBLN_EPS = 1e-5
RW_LNX_EPS = 64e-5
SW_WINDOW = 128
N_MOD = 9
PROJ_ROWS = 512
FFN_ROWS = 512
RW_GROUP = 32
RW_CHUNK = 64
VMEM_LIMIT = 56 * 1024 * 1024


def _cparams(*sem):
    return pltpu.CompilerParams(dimension_semantics=sem, vmem_limit_bytes=VMEM_LIMIT)


def _sigmoid(x):
    return 1.0 / (1.0 + jnp.exp(-x))


def _dot(a, b):
    return jnp.dot(a, b, preferred_element_type=F32)


def _dot_nt(a, b):
    return lax.dot_general(a, b, (((1,), (1,)), ((), ())), preferred_element_type=F32)


def _norm_mod(x, g, shift, scale):
    ms = jnp.mean(x * x, axis=-1, keepdims=True)
    return (x * lax.rsqrt(ms + NORM_EPS) * g) * (1.0 + scale) + shift


def _sub_tiles(n):
    step = min(n, PROJ_ROWS)
    return [slice(r0, r0 + step) for r0 in range(0, n, step)]


def _const_spec(shape):
    nd = len(shape)
    return pl.BlockSpec(shape, lambda *_: (0,) * nd, pipeline_mode=pl.Buffered(1))


def _group_mats(d):
    grp = jnp.arange(d) // HEAD_DIM
    gsum = (grp[:, None] == jnp.arange(LANES)[None, :]).astype(BF16)
    return gsum, jnp.concatenate([gsum.T, gsum.T], axis=0)


def _group_bcast(v, gexp):
    hi = v.astype(BF16)
    lo = (v - hi.astype(F32)).astype(BF16)
    return _dot(jnp.concatenate([hi, lo], axis=1), gexp)


def _ada_kernel(c_ref, w_ref, b_ref, o_ref):
    c = c_ref[...]
    cond = c * _sigmoid(c)
    o_ref[...] = jnp.dot(cond, w_ref[...], precision=HIGHEST, preferred_element_type=F32) + b_ref[...]


def _ada(c, ada_w, ada_b):
    L, D, ND = ada_w.shape
    B = c.shape[0]
    cols = 3 * D if ND % (3 * D) == 0 else D
    return pl.pallas_call(
        _ada_kernel,
        grid=(L, ND // cols),
        in_specs=[pl.BlockSpec((B, D), lambda l, j: (0, 0)),
                  pl.BlockSpec((None, D, cols), lambda l, j: (l, 0, j)),
                  pl.BlockSpec((None, 1, cols), lambda l, j: (l, 0, j))],
        out_specs=pl.BlockSpec((None, B, cols), lambda l, j: (l, 0, j)),
        out_shape=jax.ShapeDtypeStruct((L, B, ND), F32),
        compiler_params=_cparams("parallel", "parallel"),
        name="ada_mod",
    )(c, ada_w, ada_b.reshape(L, 1, ND))


def _ffn_kernel(*refs, d_ff, chunks, rows, mixer):
    if mixer is None:
        x_ref, mod_ref, g_ref, win_ref, wout_ref, o_ref = refs
    elif mixer == "plain":
        x_ref, mod_ref, g_ref, win_ref, wout_ref, y_ref, mixmod_ref, wo_ref, o_ref = refs
    else:
        x_ref, mod_ref, g_ref, win_ref, wout_ref, y_ref, ygate_ref, mixmod_ref, wo_ref, o_ref = refs
    tm = x_ref.shape[0]
    subs = list(range(0, tm, rows))

    def residual_in(r0):
        x = x_ref[r0:r0 + rows, :]
        if mixer is None:
            return x
        y = y_ref[r0:r0 + rows, :]
        if mixer == "gated":
            y = (y.astype(F32) * ygate_ref[r0:r0 + rows, :].astype(F32)).astype(BF16)
        return x + mixmod_ref[2:3, :] * _dot(y, wo_ref[...])

    xs, hidden = {}, {}

    def prepare(r0):
        xs[r0] = residual_in(r0)
        hidden[r0] = _norm_mod(xs[r0], g_ref[...], mod_ref[0:1, :], mod_ref[1:2, :]).astype(BF16)

    def gate_up(r0, c0, c1):
        h = hidden[r0]
        return _dot(h, win_ref[:, c0:c1]), _dot(h, win_ref[:, d_ff + c0:d_ff + c1])

    jobs = [(r0, c0, c1) for r0 in subs for c0, c1 in chunks]
    prepare(subs[0])
    pending = gate_up(*jobs[0])
    acc = None
    for n, (r0, c0, c1) in enumerate(jobs):
        gate, up = pending
        if n + 1 < len(jobs):
            if jobs[n + 1][0] not in hidden:
                prepare(jobs[n + 1][0])
            pending = gate_up(*jobs[n + 1])
        if n == 0:
            for r in subs[1:]:
                if r not in hidden:
                    prepare(r)
        act = (gate * _sigmoid(gate) * up).astype(BF16)
        y = _dot(act, wout_ref[c0:c1, :])
        acc = y if c0 == 0 else acc + y
        if c1 == d_ff:
            o_ref[r0:r0 + rows, :] = xs[r0] + (0.5 * mod_ref[2:3, :]) * acc


def _ffn_chunks(d_ff):
    last = 2 * LANES
    if d_ff % LANES or d_ff < 4 * last:
        return [(0, d_ff)]
    mid = (d_ff // 2) // LANES * LANES
    return [(0, mid), (mid, d_ff - last), (d_ff - last, d_ff)]


def _ffn(x2d, mod3, g, w_in, w_out, *, tm, tiles_per_batch, mix=None):
    N, D = x2d.shape
    d_ff = w_out.shape[0]
    tile = pl.BlockSpec((tm, D), lambda i: (i, 0))
    modspec = pl.BlockSpec((None, 3, D), lambda i: (i // tiles_per_batch, 0, 0))
    ins = [x2d, mod3, g.reshape(1, D), w_in, w_out]
    specs = [tile, modspec, _const_spec((1, D)), _const_spec(w_in.shape), _const_spec(w_out.shape)]
    mixer = None
    if mix is not None:
        y, y_gate, mixmod, w_o = mix
        mixer = "plain" if y_gate is None else "gated"
        ins += [y] + ([] if y_gate is None else [y_gate]) + [mixmod, w_o]
        specs += [tile] + ([] if y_gate is None else [tile]) + [modspec, _const_spec(w_o.shape)]
    kern = functools.partial(_ffn_kernel, d_ff=d_ff, chunks=_ffn_chunks(d_ff), rows=min(tm, FFN_ROWS), mixer=mixer)
    return pl.pallas_call(
        kern,
        grid=(N // tm,),
        in_specs=specs,
        out_specs=tile,
        out_shape=jax.ShapeDtypeStruct((N, D), F32),
        compiler_params=_cparams("parallel"),
        name="ffn_half" if mixer is None else "ffn_half_" + mixer,
    )(*ins)


def _group_rms_scale(y, gsum, gexp):
    ss = _dot((y * y).astype(BF16), gsum)
    return _group_bcast(lax.rsqrt(ss * (1.0 / HEAD_DIM) + NORM_EPS), gexp)


def _da_qkv_kernel(x_ref, mod_ref, g_ref, w_ref, gsum_ref, gexp_ref, qg_ref, kg_ref, qt_ref, k_ref, vt_ref):
    D = x_ref.shape[1]
    for rows in _sub_tiles(x_ref.shape[0]):
        h = _norm_mod(x_ref[rows, :], g_ref[...], mod_ref[0:1, :], mod_ref[1:2, :]).astype(BF16)
        q = _dot(h, w_ref[:, 0:D])
        k = _dot(h, w_ref[:, D:2 * D])
        v = _dot(h, w_ref[:, 2 * D:3 * D])
        vt_ref[:, rows] = v.T.astype(BF16)
        qt_ref[:, rows] = (q * _group_rms_scale(q, gsum_ref[...], gexp_ref[...]) * qg_ref[...]).T.astype(BF16)
        k_ref[rows, :] = (k * _group_rms_scale(k, gsum_ref[...], gexp_ref[...]) * kg_ref[...]).astype(BF16)


def _da_qkv(x2d, mod3, g, w_qkv, q_gain, k_gain, *, B, T, tm):
    N, D = x2d.shape
    tpb = T // tm
    gsum, gexp = _group_mats(D)
    tile = pl.BlockSpec((tm, D), lambda i: (i, 0))
    transposed = pl.BlockSpec((None, D, tm), lambda i: (i // tpb, 0, i % tpb))
    return pl.pallas_call(
        _da_qkv_kernel,
        grid=(N // tm,),
        in_specs=[tile,
                  pl.BlockSpec((None, 3, D), lambda i: (i // tpb, 0, 0)),
                  _const_spec((1, D)), _const_spec(w_qkv.shape), _const_spec(gsum.shape),
                  _const_spec(gexp.shape), _const_spec((1, D)), _const_spec((1, D))],
        out_specs=[transposed, tile, transposed],
        out_shape=[jax.ShapeDtypeStruct((B, D, T), BF16), jax.ShapeDtypeStruct((N, D), BF16),
                   jax.ShapeDtypeStruct((B, D, T), BF16)],
        compiler_params=_cparams("parallel"),
        name="da_qkv",
    )(x2d, mod3, g.reshape(1, D), w_qkv, gsum, gexp, q_gain, k_gain)


def _da_attn_kernel(lam_ref, sg_ref, qt_ref, k_ref, vt_ref, o_ref, acc_ref, sa_ref, sb_ref, *, tq):
    nq = k_ref.shape[0] // tq
    dim = lax.broadcasted_iota(jnp.int32, (LANES, 1), 0)
    lam = lam_ref[...]
    lambda_init = lam[4:5, 0:1]
    lam_full = (jnp.exp(jnp.sum(lam[0:1] * lam[1:2], axis=1, keepdims=True))
                - jnp.exp(jnp.sum(lam[2:3] * lam[3:4], axis=1, keepdims=True)) + lambda_init)
    kpos = lax.broadcasted_iota(jnp.int32, (tq, tq), 0)
    qpos = lax.broadcasted_iota(jnp.int32, (tq, tq), 1)
    causal = kpos <= qpos
    bufs = (sa_ref, sb_ref)
    q_maps = {}

    def q_masked(qi):
        if qi not in q_maps:
            qt = qt_ref[:, qi * tq:(qi + 1) * tq]
            zero = jnp.zeros_like(qt)
            q_maps[qi] = (jnp.where(dim < HEAD_DIM, qt, zero), jnp.where(dim >= HEAD_DIM, qt, zero))
        return q_maps[qi]

    def scores_into(s_ref, qi, j):
        k = k_ref[j * tq:(j + 1) * tq, :]
        for c in range(2):
            s_ref[c] = _dot(k, q_masked(qi)[c])

    def consume(s_ref, acc, qi, j, carry):
        vt = vt_ref[:, j * tq:(j + 1) * tq]
        out = []
        for c in range(2):
            s = s_ref[c]
            if j == qi:
                s = jnp.where(causal, s, -jnp.inf)
            m_cur = jnp.max(s, axis=0, keepdims=True)
            if j == 0:
                m_new = m_cur
                p = jnp.exp2(s - m_new)
                l_new = jnp.sum(p, axis=0, keepdims=True)
                acc[c] = _dot(vt, p.astype(BF16))
            else:
                m_prev, l_prev = carry[2 * c], carry[2 * c + 1]
                m_new = jnp.maximum(m_prev, m_cur)
                alpha = jnp.exp2(m_prev - m_new)
                p = jnp.exp2(s - m_new)
                l_new = alpha * l_prev + jnp.sum(p, axis=0, keepdims=True)
                acc[c] = alpha * acc[c] + _dot(vt, p.astype(BF16))
            out += [m_new, l_new]
        return tuple(out)

    def finish(acc, qi, carry):
        ot = acc[0] / carry[1] - lam_full * (acc[1] / carry[3])
        ms = jnp.mean(ot * ot, axis=0, keepdims=True)
        ot = ot * lax.rsqrt(ms + SUBLN_EPS)
        o_ref[qi * tq:(qi + 1) * tq, :] = (ot.T * sg_ref[...] * (1.0 - lambda_init)).astype(BF16)

    tasks = [(qi, j) for qi in range(nq) for j in range(qi + 1)]
    scores_into(bufs[0], *tasks[0])
    carry = None
    for t, (qi, j) in enumerate(tasks):
        if t + 1 < len(tasks):
            scores_into(bufs[(t + 1) % 2], *tasks[t + 1])
        acc = acc_ref.at[qi % 2]
        carry = consume(bufs[t % 2], acc, qi, j, carry)
        if j == qi:
            finish(acc, qi, carry)


def _da_attn(qt, k, vt, lambdas, subln_g, *, B, T, D, tq, lambda_init):
    H = D // LANES
    k3 = k.reshape(B, T, D)
    seq = pl.BlockSpec((None, T, LANES), lambda b, h: (b, 0, h))
    seq_t = pl.BlockSpec((None, LANES, T), lambda b, h: (b, h, 0))
    out = pl.pallas_call(
        functools.partial(_da_attn_kernel, tq=tq),
        grid=(B, H),
        in_specs=[pl.BlockSpec((5, HEAD_DIM), lambda b, h: (0, 0)),
                  pl.BlockSpec((1, LANES), lambda b, h: (0, 0)),
                  seq_t, seq, seq_t],
        out_specs=seq,
        out_shape=jax.ShapeDtypeStruct((B, T, D), BF16),
        scratch_shapes=[pltpu.VMEM((2, 2, LANES, tq), F32),
                        pltpu.VMEM((2, tq, tq), F32), pltpu.VMEM((2, tq, tq), F32)],
        compiler_params=_cparams("parallel", "parallel"),
        name="da_attn",
    )(jnp.concatenate([lambdas, jnp.full((1, HEAD_DIM), lambda_init, F32)], axis=0),
      subln_g.reshape(1, LANES), qt, k3, vt)
    return out.reshape(B * T, D)


def _sw_qkv_kernel(x_ref, mod_ref, g_ref, w_ref, gsum_ref, gexp_ref, ksum_ref, kexp_ref, qg_ref, kg_ref,
                   q_ref, k_ref, v_ref):
    D = x_ref.shape[1]
    kv = k_ref.shape[1]
    for rows in _sub_tiles(x_ref.shape[0]):
        h = _norm_mod(x_ref[rows, :], g_ref[...], mod_ref[0:1, :], mod_ref[1:2, :]).astype(BF16)
        q = _dot(h, w_ref[:, 0:D])
        k = _dot(h, w_ref[:, D:D + kv])
        v_ref[rows, :] = _dot(h, w_ref[:, D + kv:D + 2 * kv]).astype(BF16)
        q_ref[rows, :] = (q * _group_rms_scale(q, gsum_ref[...], gexp_ref[...]) * qg_ref[...]).astype(BF16)
        k_ref[rows, :] = (k * _group_rms_scale(k, ksum_ref[...], kexp_ref[...]) * kg_ref[...]).astype(BF16)


def _sw_qkv(x2d, mod3, g, w_qkv, q_gain, k_gain, *, tm, tiles_per_batch):
    N, D = x2d.shape
    kv = (w_qkv.shape[1] - D) // 2
    gsum, gexp = _group_mats(D)
    ksum, kexp = _group_mats(kv)
    tile = pl.BlockSpec((tm, D), lambda i: (i, 0))
    kvtile = pl.BlockSpec((tm, kv), lambda i: (i, 0))
    return pl.pallas_call(
        _sw_qkv_kernel,
        grid=(N // tm,),
        in_specs=[tile, pl.BlockSpec((None, 3, D), lambda i: (i // tiles_per_batch, 0, 0)),
                  _const_spec((1, D)), _const_spec(w_qkv.shape), _const_spec(gsum.shape),
                  _const_spec(gexp.shape), _const_spec(ksum.shape), _const_spec(kexp.shape),
                  _const_spec((1, D)), _const_spec((1, kv))],
        out_specs=[tile, kvtile, kvtile],
        out_shape=[jax.ShapeDtypeStruct((N, D), BF16), jax.ShapeDtypeStruct((N, kv), BF16),
                   jax.ShapeDtypeStruct((N, kv), BF16)],
        compiler_params=_cparams("parallel"),
        name="sw_qkv",
    )(x2d, mod3, g.reshape(1, D), w_qkv, gsum, gexp, ksum, kexp, q_gain, k_gain)


def _sw_attn_kernel(sink_ref, q_ref, k_ref, v_ref, x_ref, mod_ref, wo_ref, o_ref, att_ref, *, tq, group):
    i = pl.program_id(1)
    D = q_ref.shape[1]
    W = SW_WINDOW
    n_kv = (D // HEAD_DIM) // group
    tiles_per_kv = group // 2
    lane = lax.broadcasted_iota(jnp.int32, (1, LANES), 1)
    first = lane < HEAD_DIM
    jobs = [(sub, kvh) for sub in range(tq // W) for kvh in range(n_kv)]

    def band_start(sub):
        return pl.multiple_of(jnp.maximum(i * tq + (sub - 1) * W, 0), W)

    def scores(sub, kvh):
        k = k_ref[pl.ds(band_start(sub), 2 * W), kvh * LANES:(kvh + 1) * LANES]
        parts = []
        for b in range(tiles_per_kv):
            blk = kvh * tiles_per_kv + b
            q = q_ref[sub * W:(sub + 1) * W, blk * LANES:(blk + 1) * LANES]
            zero = jnp.zeros_like(q)
            parts += [jnp.where(first, q, zero), jnp.where(first, zero, q)]
        return _dot_nt(jnp.concatenate(parts, axis=0), k)

    pending = scores(*jobs[0])
    for n, (sub, kvh) in enumerate(jobs):
        s_all = pending
        if n + 1 < len(jobs):
            pending = scores(*jobs[n + 1])
        qpos = i * tq + sub * W + lax.broadcasted_iota(jnp.int32, (W, 2 * W), 0)
        kpos = band_start(sub) + lax.broadcasted_iota(jnp.int32, (W, 2 * W), 1)
        valid = jnp.logical_and(kpos <= qpos, kpos > qpos - W)
        probs, inv = [], []
        for g in range(group):
            s = jnp.where(valid, s_all[g * W:(g + 1) * W], -jnp.inf)
            snk = sink_ref[kvh * group + g] * math.log2(math.e)
            m = jnp.maximum(jnp.max(s, axis=1, keepdims=True), snk)
            p = jnp.exp2(s - m)
            inv.append(1.0 / (jnp.sum(p, axis=1, keepdims=True) + jnp.exp2(snk - m)))
            probs.append(p.astype(BF16))
        v = v_ref[pl.ds(band_start(sub), 2 * W), kvh * LANES:(kvh + 1) * LANES]
        o_all = _dot(jnp.concatenate(probs, axis=0), v)
        for b in range(tiles_per_kv):
            blk = kvh * tiles_per_kv + b
            o0 = o_all[2 * b * W:(2 * b + 1) * W] * inv[2 * b]
            o1 = o_all[(2 * b + 1) * W:(2 * b + 2) * W] * inv[2 * b + 1]
            att_ref[sub * W:(sub + 1) * W, blk * LANES:(blk + 1) * LANES] = jnp.where(first, o0, o1).astype(BF16)
    o_ref[...] = x_ref[...] + mod_ref[2:3, :] * _dot(att_ref[...], wo_ref[...])


def _sw_attn(q, kd, vd, sinks, x2d, mod3, w_o, *, B, T, tq, group):
    N, D = x2d.shape
    nq = T // tq
    kvd = kd.shape[1]
    tile = pl.BlockSpec((tq, D), lambda b, i: (b * nq + i, 0))
    seq = pl.BlockSpec((None, T, kvd), lambda b, i: (b, 0, 0))
    return pl.pallas_call(
        functools.partial(_sw_attn_kernel, tq=tq, group=group),
        grid=(B, nq),
        in_specs=[pl.BlockSpec(memory_space=pltpu.SMEM), tile, seq, seq, tile,
                  pl.BlockSpec((None, 3, D), lambda b, i: (b, 0, 0)), _const_spec(w_o.shape)],
        out_specs=tile,
        out_shape=jax.ShapeDtypeStruct((N, D), F32),
        scratch_shapes=[pltpu.VMEM((tq, D), BF16)],
        compiler_params=_cparams("parallel", "arbitrary"),
        name="sw_attn",
    )(sinks, q, kd.reshape(B, T, kvd), vd.reshape(B, T, kvd), x2d, mod3, w_o)


def _rw_prep_kernel(x_ref, xp_ref, mod_ref, g_ref, mu_ref, wrkv_ref, w0_ref, w1_ref, w2_ref, a0_ref, a1_ref,
                    a2_ref, g1_ref, g2_ref, kk_ref, ka_ref, gsum_ref, gexp_ref,
                    r_out, lw_out, k_out, v_out, kk_out, b_out, g_out, *, tiles_per_batch):
    shift, scale, gn = mod_ref[0:1, :], mod_ref[1:2, :], g_ref[...]
    for rows in _sub_tiles(x_ref.shape[0]):
        h = _norm_mod(x_ref[rows, :], gn, shift, scale)
        if rows.start == 0:
            hp = _norm_mod(xp_ref[...], gn, shift, scale)[7:8, :]
            hp = jnp.where(pl.program_id(0) % tiles_per_batch == 0, jnp.zeros_like(hp), hp)
        else:
            hp = _norm_mod(x_ref[rows.start - 8:rows.start, :], gn, shift, scale)[7:8, :]
        row = lax.broadcasted_iota(jnp.int32, (rows.stop - rows.start, 1), 0)
        xx = jnp.where(row == 0, hp, pltpu.roll(h, 1, axis=0)) - h

        def mixed(n):
            return (h + xx * mu_ref[n:n + 1, :]).astype(BF16)

        d1 = _dot(mixed(3), w1_ref[...])
        a1 = _dot(mixed(4), a1_ref[...])
        g1 = _dot(mixed(5), g1_ref[...])
        r = _dot(mixed(0), wrkv_ref[0])
        k = _dot(mixed(1), wrkv_ref[1])
        v = _dot(mixed(2), wrkv_ref[2])
        d = _dot(jnp.tanh(d1).astype(BF16), w2_ref[...]) + w0_ref[...]
        a = _sigmoid(a0_ref[...] + _dot(a1.astype(BF16), a2_ref[...]))
        g_out[rows, :] = _dot(_sigmoid(g1).astype(BF16), g2_ref[...]).astype(BF16)
        z = -d
        softplus = jnp.maximum(z, 0.0) + jnp.log(1.0 + jnp.exp(-jnp.abs(z)))
        lw_out[rows, :] = -jnp.exp(-softplus - 0.5)
        kk = k * kk_ref[...]
        ss = _dot((kk * kk).astype(BF16), gsum_ref[...])
        kk = kk * _group_bcast(1.0 / jnp.maximum(jnp.sqrt(ss), 1e-12), gexp_ref[...])
        r_out[rows, :] = r.astype(BF16)
        k_out[rows, :] = (k * (1.0 + (a - 1.0) * ka_ref[...])).astype(BF16)
        v_out[rows, :] = v.astype(BF16)
        kk_out[rows, :] = kk.astype(BF16)
        b_out[rows, :] = (kk * a).astype(BF16)


def _pad_to(w, axis, n):
    pad = [(0, 0)] * w.ndim
    pad[axis] = (0, n - w.shape[axis])
    return jnp.pad(w, pad)


def _round_up(n, m):
    return -(-n // m) * m


def _rw_prep(x2d, mod3, g, p, *, tm, tiles_per_batch):
    N, D = x2d.shape
    gsum, gexp = _group_mats(D)
    row = lambda t: t.reshape(1, D)
    lora = lambda w1, w2: (_pad_to(w1, 1, _round_up(w1.shape[1], LANES)).astype(BF16),
                           _pad_to(w2, 0, _round_up(w2.shape[0], LANES)).astype(BF16))
    w1, w2 = lora(p["decay_w1"], p["decay_w2"])
    a1, a2 = lora(p["iclr_a1"], p["iclr_a2"])
    g1, g2 = lora(p["gate_g1"], p["gate_g2"])
    consts = [row(g), p["mu"], p["w_rkv"].astype(BF16), row(p["decay_w0"]), w1, w2, row(p["iclr_a0"]), a1, a2,
              g1, g2, row(p["k_k"]), row(p["k_a"]), gsum, gexp]
    tile = pl.BlockSpec((tm, D), lambda i: (i, 0))
    outs = [jax.ShapeDtypeStruct((N, D), BF16)] * 7
    outs[1] = jax.ShapeDtypeStruct((N, D), F32)
    return pl.pallas_call(
        functools.partial(_rw_prep_kernel, tiles_per_batch=tiles_per_batch),
        grid=(N // tm,),
        in_specs=[tile,
                  pl.BlockSpec((8, D), lambda i: (jnp.maximum(i * (tm // 8) - 1, 0), 0)),
                  pl.BlockSpec((None, 3, D), lambda i: (i // tiles_per_batch, 0, 0))]
                 + [_const_spec(c.shape) for c in consts],
        out_specs=[tile] * 7,
        out_shape=outs,
        compiler_params=_cparams("parallel"),
        name="rw_prep",
    )(x2d, x2d, mod3, *consts)


def _stack_heads(x, m0):
    z = jnp.zeros_like(x)
    return jnp.concatenate([jnp.where(m0, x, z), jnp.where(m0, z, x)], axis=0)


def _rw_scan_kernel(r_ref, lw_ref, k_ref, v_ref, kk_ref, b_ref, rk_ref, lng_ref, lnb_ref, o_ref, st_ref, os_ref,
                    *, tt, group):
    C = RW_CHUNK
    C2 = 2 * C

    @pl.when(pl.program_id(2) == 0)
    def _():
        st_ref[...] = jnp.zeros_like(st_ref)

    lane = lax.broadcasted_iota(jnp.int32, (1, LANES), 1)
    m0 = lane < HEAD_DIM
    ti = lax.broadcasted_iota(jnp.int32, (C, C), 0)
    tj = lax.broadcasted_iota(jnp.int32, (C, C), 1)
    tri = (ti >= tj).astype(F32)
    ri = lax.broadcasted_iota(jnp.int32, (2 * C2, 2 * C2), 0)
    ci = lax.broadcasted_iota(jnp.int32, (2 * C2, 2 * C2), 1)
    rt, ct = ri % C, ci % C
    causal = rt - ct + (ri >= C2).astype(jnp.int32) > 0
    eye = (lax.broadcasted_iota(jnp.int32, (C2, C2), 0) == lax.broadcasted_iota(jnp.int32, (C2, C2), 1))
    eye_f = eye.astype(F32)
    si = lax.broadcasted_iota(jnp.int32, (C2, C2), 0)
    sj = lax.broadcasted_iota(jnp.int32, (C2, C2), 1)
    merge_masks = []
    s = 1
    while s < C:
        merge_masks.append(jnp.logical_and(si // (2 * s) == sj // (2 * s),
                                           jnp.logical_and(si % (2 * s) >= s, sj % (2 * s) < s)))
        s *= 2
    zeros_sq = jnp.zeros((C2, C2), BF16)

    def chunk_terms(chunks):
        n = range(len(chunks))
        a_t, r_t, v_s, aa, lhs_top, p_col = [], [], [], [], [], []
        for i, c in enumerate(chunks):
            rows = slice(c * C, (c + 1) * C)
            lw = lw_ref[rows, :]
            r = r_ref[rows, :].astype(F32)
            k = k_ref[rows, :].astype(F32)
            kk = kk_ref[rows, :].astype(F32)
            b = b_ref[rows, :].astype(F32)
            cum = jnp.dot(tri, lw, precision=HIGHEST, preferred_element_type=F32)
            tot = cum[C - 1:C, :]
            e_neg = jnp.exp(-cum)
            r_t.append(_stack_heads(r * jnp.exp(cum), m0).astype(BF16))
            a_t.append(_stack_heads(-kk * jnp.exp(cum - lw), m0).astype(BF16))
            b_t = _stack_heads(b * e_neg, m0).astype(BF16)
            k_t = _stack_heads(k * e_neg, m0).astype(BF16)
            e_end = jnp.exp(tot - cum)
            b_h = _stack_heads(b * e_end, m0)
            k_h = _stack_heads(k * e_end, m0)
            lhs_top.append(jnp.concatenate([b_h.T, k_h.T], axis=1).astype(BF16))
            v_s.append(_stack_heads(v_ref[rows, :], m0))
            p_col.append(jnp.sum(jnp.where(eye, jnp.exp(tot), 0.0), axis=1, keepdims=True))
            full = _dot_nt(jnp.concatenate([a_t[i], r_t[i]], axis=0), jnp.concatenate([b_t, k_t], axis=0))
            aa.append(jnp.where(causal, full, 0.0))
        yield

        a_ab = [aa[i][:C2, :C2] for i in n]
        t_inv = [eye_f + jnp.where(merge_masks[0], a_ab[i], 0.0) for i in n]
        for mask in merge_masks[1:]:
            tb = [t_inv[i].astype(BF16) for i in n]
            half = [_dot(tb[i], jnp.where(mask, a_ab[i], 0.0).astype(BF16)).astype(BF16) for i in n]
            yield
            t_inv = [t_inv[i] + _dot(half[i], tb[i]) for i in n]
            yield
        akv = [_dot(aa[i][:C2, C2:].astype(BF16), v_s[i]).astype(BF16) for i in n]
        yield
        au = [_dot(t_inv[i].astype(BF16), jnp.concatenate([a_t[i], akv[i]], axis=1)).astype(BF16)
              for i in n]
        yield
        big = []
        for i in n:
            rhs = jnp.concatenate([au[i], jnp.concatenate([zeros_sq, v_s[i]], axis=1)], axis=0)
            lhs = jnp.concatenate([lhs_top[i], aa[i][C2:, :].astype(BF16)], axis=0)
            big.append(_dot(lhs, rhs))
        qg = [jnp.concatenate([big[i][C2:, :C2] + r_t[i].astype(F32), big[i][:C2, :C2]], axis=0).astype(BF16)
              for i in n]
        return qg, big, p_col

    blk2 = jnp.tile(((lax.broadcasted_iota(jnp.int32, (LANES, LANES), 0) < HEAD_DIM)
                     == (lax.broadcasted_iota(jnp.int32, (LANES, LANES), 1) < HEAD_DIM)).astype(BF16), (2, 1))

    def head_sum(x):
        hi = x.astype(BF16)
        lo = (x - hi.astype(F32)).astype(BF16)
        return _dot(jnp.concatenate([hi, lo], axis=1), blk2)

    def state_steps(st, chunks, terms):
        qg, big, p_col = terms
        for i, c in enumerate(chunks):
            res = _dot(qg[i], st[0].astype(BF16))
            o_st = res[:C2] + big[i][C2:, C2:]
            os_ref[c * C:(c + 1) * C, :] = o_st[:C] + o_st[C:]
            st[0] = p_col[i] * st[0] + res[C2:] + big[i][:C2, C2:]
            yield
        rows = slice(chunks[0] * C, (chunks[-1] + 1) * C)
        o = os_ref[rows, :]
        mean = head_sum(o) * (1.0 / HEAD_DIM)
        cen = o - mean
        var = head_sum(cen * cen) * (1.0 / HEAD_DIM)
        y = cen * lax.rsqrt(var + RW_LNX_EPS) * lng_ref[...] + lnb_ref[...]
        r = r_ref[rows, :].astype(F32)
        k = k_ref[rows, :].astype(F32)
        bonus = head_sum(r * k * rk_ref[...]) * v_ref[rows, :].astype(F32)
        o_ref[rows, :] = (y + bonus).astype(BF16)
        yield

    st = [st_ref[...]]
    n_chunks = tt // C
    sizes = [group] * (n_chunks // group) + ([n_chunks % group] if n_chunks % group else [])
    if sizes[-1] > 1:
        sizes[-1:] = [sizes[-1] - sizes[-1] // 2, sizes[-1] // 2]
    n_stages = 2 * len(merge_masks) + 1
    per_stage = -(-(max(sizes) + 1) // n_stages)
    pending = iter(())
    g0 = 0
    for size in sizes:
        chunks = list(range(g0, g0 + size))
        g0 += size
        gen = chunk_terms(chunks)
        while True:
            try:
                next(gen)
            except StopIteration as done:
                terms = done.value
                break
            for _ in range(per_stage):
                next(pending, None)
        for _ in pending:
            pass
        pending = state_steps(st, chunks, terms)
    for _ in pending:
        pass
    st_ref[...] = st[0]


def _rw_scan(r, lw, k, v, kk, b, r_k, lnx_g, lnx_b, *, B, T, D, tt):
    hp = D // LANES
    blk = pl.BlockSpec((None, tt, LANES), lambda bi, h, t: (bi, t, h))
    vec = pl.BlockSpec((1, LANES), lambda bi, h, t: (0, h))
    r3 = lambda t: t.reshape(B, T, D)
    out = pl.pallas_call(
        functools.partial(_rw_scan_kernel, tt=tt, group=RW_GROUP),
        grid=(B, hp, T // tt),
        in_specs=[blk] * 6 + [vec] * 3,
        out_specs=blk,
        out_shape=jax.ShapeDtypeStruct((B, T, D), BF16),
        scratch_shapes=[pltpu.VMEM((LANES, LANES), F32), pltpu.VMEM((tt, LANES), F32)],
        compiler_params=_cparams("parallel", "parallel", "arbitrary"),
        name="rw_scan",
    )(r3(r), r3(lw), r3(k), r3(v), r3(kk), r3(b), r_k.reshape(1, D), lnx_g.reshape(1, D), lnx_b.reshape(1, D))
    return out.reshape(B * T, D)


def _diff_lambda_init(layer):
    return 0.8 - 0.6 * math.exp(-0.3 * layer)


def _pick_tile(n, pref):
    t = min(n, pref)
    while n % t:
        t //= 2
    return t


def kernel(x, c, norm_g, ada_w, ada_b, ffn_w_in, ffn_w_out, da_w_qkv, da_w_o, da_q_norm_g, da_k_norm_g, da_lambda, da_subln_g, rw_mu, rw_w_rkv, rw_w_o, rw_decay_w0, rw_decay_w1, rw_decay_w2, rw_iclr_a0, rw_iclr_a1, rw_iclr_a2, rw_gate_g1, rw_gate_g2, rw_k_k, rw_k_a, rw_r_k, rw_lnx_g, rw_lnx_b, sw_w_qkv, sw_w_o, sw_q_norm_g, sw_k_norm_g, sw_sinks):
    B, T, D = x.shape
    L = norm_g.shape[0]
    tm = _pick_tile(T, 2 * PROJ_ROWS)
    tpb = T // tm
    kw = dict(tm=tm, tiles_per_batch=tpb)
    tf = _pick_tile(T, 2 * FFN_ROWS)
    fkw = dict(tm=tf, tiles_per_batch=T // tf)
    scale = HEAD_DIM ** -0.5

    mod = _ada(c, ada_w, ada_b).reshape(L, B, 3, 3, D)
    x2d = x.reshape(B * T, D)
    for i in range(L):
        kind, j = i % 3, i // 3
        x2d = _ffn(x2d, mod[i, :, 0], norm_g[i, 0], ffn_w_in[i, 0].astype(BF16), ffn_w_out[i, 0].astype(BF16), **fkw)
        m2 = mod[i, :, 1]
        if kind == 0:
            qg = jnp.tile(da_q_norm_g[j] * (scale * math.log2(math.e)), D // HEAD_DIM).reshape(1, D)
            kg = jnp.tile(da_k_norm_g[j], D // HEAD_DIM).reshape(1, D)
            qt, k, vt = _da_qkv(x2d, m2, norm_g[i, 1], da_w_qkv[j].astype(BF16), qg, kg, B=B, T=T, tm=tm)
            o = _da_attn(qt, k, vt, da_lambda[j], da_subln_g[j], B=B, T=T, D=D, tq=_pick_tile(T, 512),
                         lambda_init=_diff_lambda_init(i))
            mix = (o, None, m2, da_w_o[j].astype(BF16))
        elif kind == 1:
            p = dict(mu=rw_mu[j], w_rkv=rw_w_rkv[j], decay_w0=rw_decay_w0[j], decay_w1=rw_decay_w1[j],
                     decay_w2=rw_decay_w2[j], iclr_a0=rw_iclr_a0[j], iclr_a1=rw_iclr_a1[j], iclr_a2=rw_iclr_a2[j],
                     gate_g1=rw_gate_g1[j], gate_g2=rw_gate_g2[j], k_k=rw_k_k[j], k_a=rw_k_a[j])
            r, lw, k, v, kk, b, g = _rw_prep(x2d, m2, norm_g[i, 1], p, **kw)
            y = _rw_scan(r, lw, k, v, kk, b, rw_r_k[j], rw_lnx_g[j], rw_lnx_b[j], B=B, T=T, D=D,
                         tt=_pick_tile(T, 4096))
            mix = (y, g, m2, rw_w_o[j].astype(BF16))
        else:
            kv = (sw_w_qkv.shape[-1] - D) // 2
            n_kv = kv // HEAD_DIM
            qg = jnp.tile(sw_q_norm_g[j] * (scale * math.log2(math.e)), D // HEAD_DIM).reshape(1, D)
            dup = lambda w: jnp.repeat(w.reshape(D, n_kv, 1, HEAD_DIM), 2, axis=2).reshape(D, 2 * kv)
            w = sw_w_qkv[j]
            w = jnp.concatenate([w[:, :D], dup(w[:, D:D + kv]), dup(w[:, D + kv:])], axis=1).astype(BF16)
            kg = jnp.tile(sw_k_norm_g[j], 2 * n_kv).reshape(1, 2 * kv)
            q, k, v = _sw_qkv(x2d, m2, norm_g[i, 1], w, qg, kg, **kw)
            x2d = _sw_attn(q, k, v, sw_sinks[j], x2d, m2, sw_w_o[j].astype(BF16), B=B, T=T,
                           tq=_pick_tile(T, 1024), group=(D // HEAD_DIM) // n_kv)
            mix = None
        x2d = _ffn(x2d, mod[i, :, 2], norm_g[i, 2], ffn_w_in[i, 1].astype(BF16), ffn_w_out[i, 1].astype(BF16),
                   mix=mix, **fkw)
    return x2d.reshape(B, T, D)
```

```python
import functools
import math

import jax
import jax.numpy as jnp
from jax import lax
from jax.experimental import pallas as pl
from jax.experimental.pallas import tpu as pltpu

F32 = jnp.float32
BF16 = jnp.bfloat16
HIGHEST = lax.Precision.HIGHEST

HEAD_DIM = 64
LANES = 128
NORM_EPS = 1e-6
SUBLN_EPS = 1e-5
RW_LNX_EPS = 64e-5
SW_WINDOW = 128
N_MOD = 9
PROJ_ROWS = 512
FFN_ROWS = 512
RW_GROUP = 16
RW_CHUNK = 64
VMEM_LIMIT = 56 * 1024 * 1024


def _cparams(*sem):
    return pltpu.CompilerParams(dimension_semantics=sem, vmem_limit_bytes=VMEM_LIMIT)


def _sigmoid(x):
    return 1.0 / (1.0 + jnp.exp(-x))


def _dot(a, b):
    return jnp.dot(a, b, preferred_element_type=F32)


def _dot_nt(a, b):
    return lax.dot_general(a, b, (((1,), (1,)), ((), ())), preferred_element_type=F32)


def _norm_mod(x, g, shift, scale):
    ms = jnp.mean(x * x, axis=-1, keepdims=True)
    return (x * lax.rsqrt(ms + NORM_EPS) * g) * (1.0 + scale) + shift


def _sub_tiles(n):
    step = min(n, PROJ_ROWS)
    return [slice(r0, r0 + step) for r0 in range(0, n, step)]


def _const_spec(shape):
    nd = len(shape)
    return pl.BlockSpec(shape, lambda *_: (0,) * nd, pipeline_mode=pl.Buffered(1))


def _group_mats(d):
    grp = jnp.arange(d) // HEAD_DIM
    gsum = (grp[:, None] == jnp.arange(LANES)[None, :]).astype(BF16)
    return gsum, jnp.concatenate([gsum.T, gsum.T], axis=0)


def _group_bcast(v, gexp):
    hi = v.astype(BF16)
    lo = (v - hi.astype(F32)).astype(BF16)
    return _dot(jnp.concatenate([hi, lo], axis=1), gexp)


def _ada_kernel(c_ref, w_ref, b_ref, o_ref):
    c = c_ref[...]
    cond = c * _sigmoid(c)
    o_ref[...] = jnp.dot(cond, w_ref[...], precision=HIGHEST, preferred_element_type=F32) + b_ref[...]


def _ada(c, ada_w, ada_b):
    L, D, ND = ada_w.shape
    B = c.shape[0]
    cols = 3 * D if ND % (3 * D) == 0 else D
    return pl.pallas_call(
        _ada_kernel,
        grid=(L, ND // cols),
        in_specs=[pl.BlockSpec((B, D), lambda l, j: (0, 0)),
                  pl.BlockSpec((None, D, cols), lambda l, j: (l, 0, j)),
                  pl.BlockSpec((None, 1, cols), lambda l, j: (l, 0, j))],
        out_specs=pl.BlockSpec((None, B, cols), lambda l, j: (l, 0, j)),
        out_shape=jax.ShapeDtypeStruct((L, B, ND), F32),
        compiler_params=_cparams("parallel", "parallel"),
        name="ada_mod",
    )(c, ada_w, ada_b.reshape(L, 1, ND))


def _ffn_kernel(*refs, d_ff, chunks, rows, mixer):
    if mixer is None:
        x_ref, mod_ref, g_ref, win_ref, wout_ref, o_ref = refs
    elif mixer == "plain":
        x_ref, mod_ref, g_ref, win_ref, wout_ref, y_ref, mixmod_ref, wo_ref, o_ref = refs
    else:
        x_ref, mod_ref, g_ref, win_ref, wout_ref, y_ref, ygate_ref, mixmod_ref, wo_ref, o_ref = refs
    tm = x_ref.shape[0]
    subs = list(range(0, tm, rows))

    def residual_in(r0):
        x = x_ref[r0:r0 + rows, :]
        if mixer is None:
            return x
        y = y_ref[r0:r0 + rows, :]
        if mixer == "gated":
            y = (y.astype(F32) * ygate_ref[r0:r0 + rows, :].astype(F32)).astype(BF16)
        return x + mixmod_ref[2:3, :] * _dot(y, wo_ref[...])

    xs, hidden = {}, {}

    def prepare(r0):
        xs[r0] = residual_in(r0)
        hidden[r0] = _norm_mod(xs[r0], g_ref[...], mod_ref[0:1, :], mod_ref[1:2, :]).astype(BF16)

    def gate_up(r0, c0, c1):
        h = hidden[r0]
        return _dot(h, win_ref[:, c0:c1]), _dot(h, win_ref[:, d_ff + c0:d_ff + c1])

    jobs = [(r0, c0, c1) for r0 in subs for c0, c1 in chunks]
    prepare(subs[0])
    pending = gate_up(*jobs[0])
    acc = None
    for n, (r0, c0, c1) in enumerate(jobs):
        gate, up = pending
        if n + 1 < len(jobs):
            if jobs[n + 1][0] not in hidden:
                prepare(jobs[n + 1][0])
            pending = gate_up(*jobs[n + 1])
        if n == 0:
            for r in subs[1:]:
                if r not in hidden:
                    prepare(r)
        act = (gate * _sigmoid(gate) * up).astype(BF16)
        y = _dot(act, wout_ref[c0:c1, :])
        acc = y if c0 == 0 else acc + y
        if c1 == d_ff:
            o_ref[r0:r0 + rows, :] = xs[r0] + (0.5 * mod_ref[2:3, :]) * acc


def _ffn_chunks(d_ff):
    last = LANES
    if d_ff % LANES or d_ff < 4 * last:
        return [(0, d_ff)]
    mid = (d_ff // 2) // LANES * LANES
    return [(0, mid), (mid, d_ff - last), (d_ff - last, d_ff)]


def _ffn(x2d, mod3, g, w_in, w_out, *, tm, tiles_per_batch, mix=None):
    N, D = x2d.shape
    d_ff = w_out.shape[0]
    tile = pl.BlockSpec((tm, D), lambda i: (i, 0))
    modspec = pl.BlockSpec((None, 3, D), lambda i: (i // tiles_per_batch, 0, 0))
    ins = [x2d, mod3, g.reshape(1, D), w_in, w_out]
    specs = [tile, modspec, _const_spec((1, D)), _const_spec(w_in.shape), _const_spec(w_out.shape)]
    mixer = None
    if mix is not None:
        y, y_gate, mixmod, w_o = mix
        mixer = "plain" if y_gate is None else "gated"
        ins += [y] + ([] if y_gate is None else [y_gate]) + [mixmod, w_o]
        specs += [tile] + ([] if y_gate is None else [tile]) + [modspec, _const_spec(w_o.shape)]
    kern = functools.partial(_ffn_kernel, d_ff=d_ff, chunks=_ffn_chunks(d_ff), rows=min(tm, FFN_ROWS), mixer=mixer)
    return pl.pallas_call(
        kern,
        grid=(N // tm,),
        in_specs=specs,
        out_specs=tile,
        out_shape=jax.ShapeDtypeStruct((N, D), F32),
        compiler_params=_cparams("parallel"),
        name="ffn_half" if mixer is None else "ffn_half_" + mixer,
    )(*ins)


def _group_rms_scale(y, gsum, gexp):
    ss = _dot((y * y).astype(BF16), gsum)
    return _group_bcast(lax.rsqrt(ss * (1.0 / HEAD_DIM) + NORM_EPS), gexp)


def _da_qkv_kernel(x_ref, mod_ref, g_ref, w_ref, gsum_ref, gexp_ref, qg_ref, kg_ref, qt_ref, k_ref, vt_ref):
    D = x_ref.shape[1]
    for rows in _sub_tiles(x_ref.shape[0]):
        h = _norm_mod(x_ref[rows, :], g_ref[...], mod_ref[0:1, :], mod_ref[1:2, :]).astype(BF16)
        q = _dot(h, w_ref[:, 0:D])
        k = _dot(h, w_ref[:, D:2 * D])
        v = _dot(h, w_ref[:, 2 * D:3 * D])
        vt_ref[:, rows] = v.T.astype(BF16)
        qt_ref[:, rows] = (q * _group_rms_scale(q, gsum_ref[...], gexp_ref[...]) * qg_ref[...]).T.astype(BF16)
        k_ref[rows, :] = (k * _group_rms_scale(k, gsum_ref[...], gexp_ref[...]) * kg_ref[...]).astype(BF16)


def _da_qkv(x2d, mod3, g, w_qkv, q_gain, k_gain, *, B, T, tm):
    N, D = x2d.shape
    tpb = T // tm
    gsum, gexp = _group_mats(D)
    tile = pl.BlockSpec((tm, D), lambda i: (i, 0))
    transposed = pl.BlockSpec((None, D, tm), lambda i: (i // tpb, 0, i % tpb))
    return pl.pallas_call(
        _da_qkv_kernel,
        grid=(N // tm,),
        in_specs=[tile,
                  pl.BlockSpec((None, 3, D), lambda i: (i // tpb, 0, 0)),
                  _const_spec((1, D)), _const_spec(w_qkv.shape), _const_spec(gsum.shape),
                  _const_spec(gexp.shape), _const_spec((1, D)), _const_spec((1, D))],
        out_specs=[transposed, tile, transposed],
        out_shape=[jax.ShapeDtypeStruct((B, D, T), BF16), jax.ShapeDtypeStruct((N, D), BF16),
                   jax.ShapeDtypeStruct((B, D, T), BF16)],
        compiler_params=_cparams("parallel"),
        name="da_qkv",
    )(x2d, mod3, g.reshape(1, D), w_qkv, gsum, gexp, q_gain, k_gain)


def _da_attn_kernel(lam_ref, sg_ref, qt_ref, k_ref, vt_ref, o_ref, acc_ref, sa_ref, sb_ref, *, tq):
    nq = k_ref.shape[0] // tq
    dim = lax.broadcasted_iota(jnp.int32, (LANES, 1), 0)
    lam = lam_ref[...]
    lambda_init = lam[4:5, 0:1]
    lam_full = (jnp.exp(jnp.sum(lam[0:1] * lam[1:2], axis=1, keepdims=True))
                - jnp.exp(jnp.sum(lam[2:3] * lam[3:4], axis=1, keepdims=True)) + lambda_init)
    kpos = lax.broadcasted_iota(jnp.int32, (tq, tq), 0)
    qpos = lax.broadcasted_iota(jnp.int32, (tq, tq), 1)
    causal = kpos <= qpos
    bufs = (sa_ref, sb_ref)
    q_maps = {}

    def q_masked(qi):
        if qi not in q_maps:
            qt = qt_ref[:, qi * tq:(qi + 1) * tq]
            zero = jnp.zeros_like(qt)
            q_maps[qi] = (jnp.where(dim < HEAD_DIM, qt, zero), jnp.where(dim >= HEAD_DIM, qt, zero))
        return q_maps[qi]

    def scores_into(s_ref, qi, j):
        k = k_ref[j * tq:(j + 1) * tq, :]
        for c in range(2):
            s_ref[c] = _dot(k, q_masked(qi)[c])

    def consume(s_ref, acc, qi, j, carry):
        vt = vt_ref[:, j * tq:(j + 1) * tq]
        out = []
        for c in range(2):
            s = s_ref[c]
            if j == qi:
                s = jnp.where(causal, s, -jnp.inf)
            m_cur = jnp.max(s, axis=0, keepdims=True)
            if j == 0:
                m_new = m_cur
                p = jnp.exp2(s - m_new)
                l_new = jnp.sum(p, axis=0, keepdims=True)
                acc[c] = _dot(vt, p.astype(BF16))
            else:
                m_prev, l_prev = carry[2 * c], carry[2 * c + 1]
                m_new = jnp.maximum(m_prev, m_cur)
                alpha = jnp.exp2(m_prev - m_new)
                p = jnp.exp2(s - m_new)
                l_new = alpha * l_prev + jnp.sum(p, axis=0, keepdims=True)
                acc[c] = alpha * acc[c] + _dot(vt, p.astype(BF16))
            out += [m_new, l_new]
        return tuple(out)

    def finish(acc, qi, carry):
        ot = acc[0] / carry[1] - lam_full * (acc[1] / carry[3])
        ms = jnp.mean(ot * ot, axis=0, keepdims=True)
        ot = ot * lax.rsqrt(ms + SUBLN_EPS)
        o_ref[qi * tq:(qi + 1) * tq, :] = (ot.T * sg_ref[...] * (1.0 - lambda_init)).astype(BF16)

    tasks = [(qi, j) for qi in range(nq) for j in range(qi + 1)]
    scores_into(bufs[0], *tasks[0])
    carry = None
    for t, (qi, j) in enumerate(tasks):
        if t + 1 < len(tasks):
            scores_into(bufs[(t + 1) % 2], *tasks[t + 1])
        acc = acc_ref.at[qi % 2]
        carry = consume(bufs[t % 2], acc, qi, j, carry)
        if j == qi:
            finish(acc, qi, carry)


def _da_attn(qt, k, vt, lambdas, subln_g, *, B, T, D, tq, lambda_init):
    H = D // LANES
    k3 = k.reshape(B, T, D)
    seq = pl.BlockSpec((None, T, LANES), lambda b, h: (b, 0, h))
    seq_t = pl.BlockSpec((None, LANES, T), lambda b, h: (b, h, 0))
    out = pl.pallas_call(
        functools.partial(_da_attn_kernel, tq=tq),
        grid=(B, H),
        in_specs=[pl.BlockSpec((5, HEAD_DIM), lambda b, h: (0, 0)),
                  pl.BlockSpec((1, LANES), lambda b, h: (0, 0)),
                  seq_t, seq, seq_t],
        out_specs=seq,
        out_shape=jax.ShapeDtypeStruct((B, T, D), BF16),
        scratch_shapes=[pltpu.VMEM((2, 2, LANES, tq), F32),
                        pltpu.VMEM((2, tq, tq), F32), pltpu.VMEM((2, tq, tq), F32)],
        compiler_params=_cparams("parallel", "parallel"),
        name="da_attn",
    )(jnp.concatenate([lambdas, jnp.full((1, HEAD_DIM), lambda_init, F32)], axis=0),
      subln_g.reshape(1, LANES), qt, k3, vt)
    return out.reshape(B * T, D)


def _sw_qkv_kernel(x_ref, mod_ref, g_ref, w_ref, gsum_ref, gexp_ref, ksum_ref, kexp_ref, qg_ref, kg_ref,
                   q_ref, k_ref, v_ref):
    D = x_ref.shape[1]
    kv = k_ref.shape[1]
    for rows in _sub_tiles(x_ref.shape[0]):
        h = _norm_mod(x_ref[rows, :], g_ref[...], mod_ref[0:1, :], mod_ref[1:2, :]).astype(BF16)
        q = _dot(h, w_ref[:, 0:D])
        k = _dot(h, w_ref[:, D:D + kv])
        v_ref[rows, :] = _dot(h, w_ref[:, D + kv:D + 2 * kv]).astype(BF16)
        q_ref[rows, :] = (q * _group_rms_scale(q, gsum_ref[...], gexp_ref[...]) * qg_ref[...]).astype(BF16)
        k_ref[rows, :] = (k * _group_rms_scale(k, ksum_ref[...], kexp_ref[...]) * kg_ref[...]).astype(BF16)


def _sw_qkv(x2d, mod3, g, w_qkv, q_gain, k_gain, *, tm, tiles_per_batch):
    N, D = x2d.shape
    kv = (w_qkv.shape[1] - D) // 2
    gsum, gexp = _group_mats(D)
    ksum, kexp = _group_mats(kv)
    tile = pl.BlockSpec((tm, D), lambda i: (i, 0))
    kvtile = pl.BlockSpec((tm, kv), lambda i: (i, 0))
    return pl.pallas_call(
        _sw_qkv_kernel,
        grid=(N // tm,),
        in_specs=[tile, pl.BlockSpec((None, 3, D), lambda i: (i // tiles_per_batch, 0, 0)),
                  _const_spec((1, D)), _const_spec(w_qkv.shape), _const_spec(gsum.shape),
                  _const_spec(gexp.shape), _const_spec(ksum.shape), _const_spec(kexp.shape),
                  _const_spec((1, D)), _const_spec((1, kv))],
        out_specs=[tile, kvtile, kvtile],
        out_shape=[jax.ShapeDtypeStruct((N, D), BF16), jax.ShapeDtypeStruct((N, kv), BF16),
                   jax.ShapeDtypeStruct((N, kv), BF16)],
        compiler_params=_cparams("parallel"),
        name="sw_qkv",
    )(x2d, mod3, g.reshape(1, D), w_qkv, gsum, gexp, ksum, kexp, q_gain, k_gain)


def _sw_attn_kernel(sink_ref, q_ref, k_ref, v_ref, x_ref, mod_ref, wo_ref, o_ref, att_ref, *, tq, group):
    i = pl.program_id(1)
    D = q_ref.shape[1]
    W = SW_WINDOW
    n_kv = (D // HEAD_DIM) // group
    tiles_per_kv = group // 2
    lane = lax.broadcasted_iota(jnp.int32, (1, LANES), 1)
    first = lane < HEAD_DIM
    jobs = [(sub, kvh) for sub in range(tq // W) for kvh in range(n_kv)]

    def band_start(sub):
        return pl.multiple_of(jnp.maximum(i * tq + (sub - 1) * W, 0), W)

    def scores(sub, kvh):
        k = k_ref[pl.ds(band_start(sub), 2 * W), kvh * LANES:(kvh + 1) * LANES]
        parts = []
        for b in range(tiles_per_kv):
            blk = kvh * tiles_per_kv + b
            q = q_ref[sub * W:(sub + 1) * W, blk * LANES:(blk + 1) * LANES]
            zero = jnp.zeros_like(q)
            parts += [jnp.where(first, q, zero), jnp.where(first, zero, q)]
        return _dot_nt(jnp.concatenate(parts, axis=0), k)

    pending = scores(*jobs[0])
    for n, (sub, kvh) in enumerate(jobs):
        s_all = pending
        if n + 1 < len(jobs):
            pending = scores(*jobs[n + 1])
        qpos = i * tq + sub * W + lax.broadcasted_iota(jnp.int32, (W, 2 * W), 0)
        kpos = band_start(sub) + lax.broadcasted_iota(jnp.int32, (W, 2 * W), 1)
        valid = jnp.logical_and(kpos <= qpos, kpos > qpos - W)
        probs, inv = [], []
        for g in range(group):
            s = jnp.where(valid, s_all[g * W:(g + 1) * W], -jnp.inf)
            snk = sink_ref[kvh * group + g] * math.log2(math.e)
            m = jnp.maximum(jnp.max(s, axis=1, keepdims=True), snk)
            p = jnp.exp2(s - m)
            inv.append(1.0 / (jnp.sum(p, axis=1, keepdims=True) + jnp.exp2(snk - m)))
            probs.append(p.astype(BF16))
        v = v_ref[pl.ds(band_start(sub), 2 * W), kvh * LANES:(kvh + 1) * LANES]
        o_all = _dot(jnp.concatenate(probs, axis=0), v)
        for b in range(tiles_per_kv):
            blk = kvh * tiles_per_kv + b
            o0 = o_all[2 * b * W:(2 * b + 1) * W] * inv[2 * b]
            o1 = o_all[(2 * b + 1) * W:(2 * b + 2) * W] * inv[2 * b + 1]
            att_ref[sub * W:(sub + 1) * W, blk * LANES:(blk + 1) * LANES] = jnp.where(first, o0, o1).astype(BF16)
    o_ref[...] = x_ref[...] + mod_ref[2:3, :] * _dot(att_ref[...], wo_ref[...])


def _sw_attn(q, kd, vd, sinks, x2d, mod3, w_o, *, B, T, tq, group):
    N, D = x2d.shape
    nq = T // tq
    kvd = kd.shape[1]
    tile = pl.BlockSpec((tq, D), lambda b, i: (b * nq + i, 0))
    seq = pl.BlockSpec((None, T, kvd), lambda b, i: (b, 0, 0))
    return pl.pallas_call(
        functools.partial(_sw_attn_kernel, tq=tq, group=group),
        grid=(B, nq),
        in_specs=[pl.BlockSpec(memory_space=pltpu.SMEM), tile, seq, seq, tile,
                  pl.BlockSpec((None, 3, D), lambda b, i: (b, 0, 0)), _const_spec(w_o.shape)],
        out_specs=tile,
        out_shape=jax.ShapeDtypeStruct((N, D), F32),
        scratch_shapes=[pltpu.VMEM((tq, D), BF16)],
        compiler_params=_cparams("parallel", "arbitrary"),
        name="sw_attn",
    )(sinks, q, kd.reshape(B, T, kvd), vd.reshape(B, T, kvd), x2d, mod3, w_o)


def _rw_prep_kernel(x_ref, xp_ref, mod_ref, g_ref, mu_ref, wrkv_ref, w0_ref, w1_ref, w2_ref, a0_ref, a1_ref,
                    a2_ref, g1_ref, g2_ref, kk_ref, ka_ref, gsum_ref, gexp_ref,
                    r_out, lw_out, k_out, v_out, kk_out, b_out, g_out, *, tiles_per_batch):
    shift, scale, gn = mod_ref[0:1, :], mod_ref[1:2, :], g_ref[...]
    for rows in _sub_tiles(x_ref.shape[0]):
        h = _norm_mod(x_ref[rows, :], gn, shift, scale)
        if rows.start == 0:
            hp = _norm_mod(xp_ref[...], gn, shift, scale)[7:8, :]
            hp = jnp.where(pl.program_id(0) % tiles_per_batch == 0, jnp.zeros_like(hp), hp)
        else:
            hp = _norm_mod(x_ref[rows.start - 8:rows.start, :], gn, shift, scale)[7:8, :]
        row = lax.broadcasted_iota(jnp.int32, (rows.stop - rows.start, 1), 0)
        xx = jnp.where(row == 0, hp, pltpu.roll(h, 1, axis=0)) - h

        def mixed(n):
            return (h + xx * mu_ref[n:n + 1, :]).astype(BF16)

        d1 = _dot(mixed(3), w1_ref[...])
        a1 = _dot(mixed(4), a1_ref[...])
        g1 = _dot(mixed(5), g1_ref[...])
        r = _dot(mixed(0), wrkv_ref[0])
        k = _dot(mixed(1), wrkv_ref[1])
        v = _dot(mixed(2), wrkv_ref[2])
        d = _dot(jnp.tanh(d1).astype(BF16), w2_ref[...]) + w0_ref[...]
        a = _sigmoid(a0_ref[...] + _dot(a1.astype(BF16), a2_ref[...]))
        g_out[rows, :] = _dot(_sigmoid(g1).astype(BF16), g2_ref[...]).astype(BF16)
        z = -d
        softplus = jnp.maximum(z, 0.0) + jnp.log(1.0 + jnp.exp(-jnp.abs(z)))
        lw_out[rows, :] = -jnp.exp(-softplus - 0.5)
        kk = k * kk_ref[...]
        ss = _dot((kk * kk).astype(BF16), gsum_ref[...])
        kk = kk * _group_bcast(1.0 / jnp.maximum(jnp.sqrt(ss), 1e-12), gexp_ref[...])
        r_out[rows, :] = r.astype(BF16)
        k_out[rows, :] = (k * (1.0 + (a - 1.0) * ka_ref[...])).astype(BF16)
        v_out[rows, :] = v.astype(BF16)
        kk_out[rows, :] = kk.astype(BF16)
        b_out[rows, :] = (kk * a).astype(BF16)


def _pad_to(w, axis, n):
    pad = [(0, 0)] * w.ndim
    pad[axis] = (0, n - w.shape[axis])
    return jnp.pad(w, pad)


def _round_up(n, m):
    return -(-n // m) * m


def _rw_prep(x2d, mod3, g, p, *, tm, tiles_per_batch):
    N, D = x2d.shape
    gsum, gexp = _group_mats(D)
    row = lambda t: t.reshape(1, D)
    lora = lambda w1, w2: (_pad_to(w1, 1, _round_up(w1.shape[1], LANES)).astype(BF16),
                           _pad_to(w2, 0, _round_up(w2.shape[0], LANES)).astype(BF16))
    w1, w2 = lora(p["decay_w1"], p["decay_w2"])
    a1, a2 = lora(p["iclr_a1"], p["iclr_a2"])
    g1, g2 = lora(p["gate_g1"], p["gate_g2"])
    consts = [row(g), p["mu"], p["w_rkv"].astype(BF16), row(p["decay_w0"]), w1, w2, row(p["iclr_a0"]), a1, a2,
              g1, g2, row(p["k_k"]), row(p["k_a"]), gsum, gexp]
    tile = pl.BlockSpec((tm, D), lambda i: (i, 0))
    outs = [jax.ShapeDtypeStruct((N, D), BF16)] * 7
    outs[1] = jax.ShapeDtypeStruct((N, D), F32)
    return pl.pallas_call(
        functools.partial(_rw_prep_kernel, tiles_per_batch=tiles_per_batch),
        grid=(N // tm,),
        in_specs=[tile,
                  pl.BlockSpec((8, D), lambda i: (jnp.maximum(i * (tm // 8) - 1, 0), 0)),
                  pl.BlockSpec((None, 3, D), lambda i: (i // tiles_per_batch, 0, 0))]
                 + [_const_spec(c.shape) for c in consts],
        out_specs=[tile] * 7,
        out_shape=outs,
        compiler_params=_cparams("parallel"),
        name="rw_prep",
    )(x2d, x2d, mod3, *consts)


def _stack_heads(x, m0):
    z = jnp.zeros_like(x)
    return jnp.concatenate([jnp.where(m0, x, z), jnp.where(m0, z, x)], axis=0)


def _rw_scan_kernel(r_ref, lw_ref, k_ref, v_ref, kk_ref, b_ref, rk_ref, lng_ref, lnb_ref, o_ref, st_ref, os_ref,
                    *, tt, group):
    C = RW_CHUNK
    C2 = 2 * C

    @pl.when(pl.program_id(2) == 0)
    def _():
        st_ref[...] = jnp.zeros_like(st_ref)

    lane = lax.broadcasted_iota(jnp.int32, (1, LANES), 1)
    m0 = lane < HEAD_DIM
    ti = lax.broadcasted_iota(jnp.int32, (C, C), 0)
    tj = lax.broadcasted_iota(jnp.int32, (C, C), 1)
    tri = (ti >= tj).astype(F32)
    ri = lax.broadcasted_iota(jnp.int32, (2 * C2, 2 * C2), 0)
    ci = lax.broadcasted_iota(jnp.int32, (2 * C2, 2 * C2), 1)
    rt, ct = ri % C, ci % C
    causal = rt - ct + (ri >= C2).astype(jnp.int32) > 0
    eye = (lax.broadcasted_iota(jnp.int32, (C2, C2), 0) == lax.broadcasted_iota(jnp.int32, (C2, C2), 1))
    eye_f = eye.astype(F32)
    si = lax.broadcasted_iota(jnp.int32, (C2, C2), 0)
    sj = lax.broadcasted_iota(jnp.int32, (C2, C2), 1)
    merge_masks = []
    s = 1
    while s < C:
        merge_masks.append(jnp.logical_and(si // (2 * s) == sj // (2 * s),
                                           jnp.logical_and(si % (2 * s) >= s, sj % (2 * s) < s)))
        s *= 2
    zeros_sq = jnp.zeros((C2, C2), BF16)

    def chunk_terms(chunks):
        n = range(len(chunks))
        a_t, r_t, v_s, aa, lhs_top, p_col = [], [], [], [], [], []
        for i, c in enumerate(chunks):
            rows = slice(c * C, (c + 1) * C)
            lw = lw_ref[rows, :]
            r = r_ref[rows, :].astype(F32)
            k = k_ref[rows, :].astype(F32)
            kk = kk_ref[rows, :].astype(F32)
            b = b_ref[rows, :].astype(F32)
            cum = jnp.dot(tri, lw, precision=HIGHEST, preferred_element_type=F32)
            tot = cum[C - 1:C, :]
            e_neg = jnp.exp(-cum)
            r_t.append(_stack_heads(r * jnp.exp(cum), m0).astype(BF16))
            a_t.append(_stack_heads(-kk * jnp.exp(cum - lw), m0).astype(BF16))
            b_t = _stack_heads(b * e_neg, m0).astype(BF16)
            k_t = _stack_heads(k * e_neg, m0).astype(BF16)
            e_end = jnp.exp(tot - cum)
            b_h = _stack_heads(b * e_end, m0)
            k_h = _stack_heads(k * e_end, m0)
            lhs_top.append(jnp.concatenate([b_h.T, k_h.T], axis=1).astype(BF16))
            v_s.append(_stack_heads(v_ref[rows, :], m0))
            p_col.append(jnp.sum(jnp.where(eye, jnp.exp(tot), 0.0), axis=1, keepdims=True))
            full = _dot_nt(jnp.concatenate([a_t[i], r_t[i]], axis=0), jnp.concatenate([b_t, k_t], axis=0))
            aa.append(jnp.where(causal, full, 0.0))
        yield

        a_ab = [aa[i][:C2, :C2] for i in n]
        t_inv = [eye_f + jnp.where(merge_masks[0], a_ab[i], 0.0) for i in n]
        for mask in merge_masks[1:]:
            tb = [t_inv[i].astype(BF16) for i in n]
            half = [_dot(tb[i], jnp.where(mask, a_ab[i], 0.0).astype(BF16)).astype(BF16) for i in n]
            yield
            t_inv = [t_inv[i] + _dot(half[i], tb[i]) for i in n]
            yield
        akv = [_dot(aa[i][:C2, C2:].astype(BF16), v_s[i]).astype(BF16) for i in n]
        yield
        au = [_dot(t_inv[i].astype(BF16), jnp.concatenate([a_t[i], akv[i]], axis=1)).astype(BF16)
              for i in n]
        yield
        big = []
        for i in n:
            rhs = jnp.concatenate([au[i], jnp.concatenate([zeros_sq, v_s[i]], axis=1)], axis=0)
            lhs = jnp.concatenate([lhs_top[i], aa[i][C2:, :].astype(BF16)], axis=0)
            big.append(_dot(lhs, rhs))
        qg = [jnp.concatenate([big[i][C2:, :C2] + r_t[i].astype(F32), big[i][:C2, :C2]], axis=0).astype(BF16)
              for i in n]
        return qg, big, p_col

    blk2 = jnp.tile(((lax.broadcasted_iota(jnp.int32, (LANES, LANES), 0) < HEAD_DIM)
                     == (lax.broadcasted_iota(jnp.int32, (LANES, LANES), 1) < HEAD_DIM)).astype(BF16), (2, 1))

    def head_sum(x):
        hi = x.astype(BF16)
        lo = (x - hi.astype(F32)).astype(BF16)
        return _dot(jnp.concatenate([hi, lo], axis=1), blk2)

    def state_steps(st, chunks, terms):
        qg, big, p_col = terms
        for i, c in enumerate(chunks):
            res = _dot(qg[i], st[0].astype(BF16))
            o_st = res[:C2] + big[i][C2:, C2:]
            os_ref[c * C:(c + 1) * C, :] = o_st[:C] + o_st[C:]
            st[0] = p_col[i] * st[0] + res[C2:] + big[i][:C2, C2:]
            yield
        rows = slice(chunks[0] * C, (chunks[-1] + 1) * C)
        o = os_ref[rows, :]
        mean = head_sum(o) * (1.0 / HEAD_DIM)
        cen = o - mean
        var = head_sum(cen * cen) * (1.0 / HEAD_DIM)
        y = cen * lax.rsqrt(var + RW_LNX_EPS) * lng_ref[...] + lnb_ref[...]
        r = r_ref[rows, :].astype(F32)
        k = k_ref[rows, :].astype(F32)
        bonus = head_sum(r * k * rk_ref[...]) * v_ref[rows, :].astype(F32)
        o_ref[rows, :] = (y + bonus).astype(BF16)
        yield

    st = [st_ref[...]]
    n_chunks = tt // C
    sizes = [group] * (n_chunks // group) + ([n_chunks % group] if n_chunks % group else [])
    for _ in range(2):
        if sizes[-1] > 2:
            sizes[-1:] = [sizes[-1] - sizes[-1] // 2, sizes[-1] // 2]
    pending = iter(())
    g0 = 0
    for size in sizes:
        chunks = list(range(g0, g0 + size))
        g0 += size
        gen = chunk_terms(chunks)
        while True:
            try:
                next(gen)
            except StopIteration as done:
                terms = done.value
                break
            next(pending, None)
            next(pending, None)
        for _ in pending:
            pass
        pending = state_steps(st, chunks, terms)
    for _ in pending:
        pass
    st_ref[...] = st[0]


def _rw_scan(r, lw, k, v, kk, b, r_k, lnx_g, lnx_b, *, B, T, D, tt):
    hp = D // LANES
    blk = pl.BlockSpec((None, tt, LANES), lambda bi, h, t: (bi, t, h))
    vec = pl.BlockSpec((1, LANES), lambda bi, h, t: (0, h))
    r3 = lambda t: t.reshape(B, T, D)
    out = pl.pallas_call(
        functools.partial(_rw_scan_kernel, tt=tt, group=RW_GROUP),
        grid=(B, hp, T // tt),
        in_specs=[blk] * 6 + [vec] * 3,
        out_specs=blk,
        out_shape=jax.ShapeDtypeStruct((B, T, D), BF16),
        scratch_shapes=[pltpu.VMEM((LANES, LANES), F32), pltpu.VMEM((tt, LANES), F32)],
        compiler_params=_cparams("parallel", "parallel", "arbitrary"),
        name="rw_scan",
    )(r3(r), r3(lw), r3(k), r3(v), r3(kk), r3(b), r_k.reshape(1, D), lnx_g.reshape(1, D), lnx_b.reshape(1, D))
    return out.reshape(B * T, D)


def _diff_lambda_init(layer):
    return 0.8 - 0.6 * math.exp(-0.3 * layer)


def _pick_tile(n, pref):
    t = min(n, pref)
    while n % t:
        t //= 2
    return t


def kernel(x, c, norm_g, ada_w, ada_b, ffn_w_in, ffn_w_out, da_w_qkv, da_w_o, da_q_norm_g, da_k_norm_g, da_lambda, da_subln_g, rw_mu, rw_w_rkv, rw_w_o, rw_decay_w0, rw_decay_w1, rw_decay_w2, rw_iclr_a0, rw_iclr_a1, rw_iclr_a2, rw_gate_g1, rw_gate_g2, rw_k_k, rw_k_a, rw_r_k, rw_lnx_g, rw_lnx_b, sw_w_qkv, sw_w_o, sw_q_norm_g, sw_k_norm_g, sw_sinks):
    B, T, D = x.shape
    L = norm_g.shape[0]
    tm = _pick_tile(T, 2 * PROJ_ROWS)
    tpb = T // tm
    kw = dict(tm=tm, tiles_per_batch=tpb)
    tf = _pick_tile(T, 2 * FFN_ROWS)
    fkw = dict(tm=tf, tiles_per_batch=T // tf)
    scale = HEAD_DIM ** -0.5

    mod = _ada(c, ada_w, ada_b).reshape(L, B, 3, 3, D)
    x2d = x.reshape(B * T, D)
    for i in range(L):
        kind, j = i % 3, i // 3
        x2d = _ffn(x2d, mod[i, :, 0], norm_g[i, 0], ffn_w_in[i, 0].astype(BF16), ffn_w_out[i, 0].astype(BF16), **fkw)
        m2 = mod[i, :, 1]
        if kind == 0:
            qg = jnp.tile(da_q_norm_g[j] * (scale * math.log2(math.e)), D // HEAD_DIM).reshape(1, D)
            kg = jnp.tile(da_k_norm_g[j], D // HEAD_DIM).reshape(1, D)
            qt, k, vt = _da_qkv(x2d, m2, norm_g[i, 1], da_w_qkv[j].astype(BF16), qg, kg, B=B, T=T, tm=tm)
            o = _da_attn(qt, k, vt, da_lambda[j], da_subln_g[j], B=B, T=T, D=D, tq=_pick_tile(T, 512),
                         lambda_init=_diff_lambda_init(i))
            mix = (o, None, m2, da_w_o[j].astype(BF16))
        elif kind == 1:
            p = dict(mu=rw_mu[j], w_rkv=rw_w_rkv[j], decay_w0=rw_decay_w0[j], decay_w1=rw_decay_w1[j],
                     decay_w2=rw_decay_w2[j], iclr_a0=rw_iclr_a0[j], iclr_a1=rw_iclr_a1[j], iclr_a2=rw_iclr_a2[j],
                     gate_g1=rw_gate_g1[j], gate_g2=rw_gate_g2[j], k_k=rw_k_k[j], k_a=rw_k_a[j])
            r, lw, k, v, kk, b, g = _rw_prep(x2d, m2, norm_g[i, 1], p, **kw)
            y = _rw_scan(r, lw, k, v, kk, b, rw_r_k[j], rw_lnx_g[j], rw_lnx_b[j], B=B, T=T, D=D,
                         tt=_pick_tile(T, 4096))
            mix = (y, g, m2, rw_w_o[j].astype(BF16))
        else:
            kv = (sw_w_qkv.shape[-1] - D) // 2
            n_kv = kv // HEAD_DIM
            qg = jnp.tile(sw_q_norm_g[j] * (scale * math.log2(math.e)), D // HEAD_DIM).reshape(1, D)
            dup = lambda w: jnp.repeat(w.reshape(D, n_kv, 1, HEAD_DIM), 2, axis=2).reshape(D, 2 * kv)
            w = sw_w_qkv[j]
            w = jnp.concatenate([w[:, :D], dup(w[:, D:D + kv]), dup(w[:, D + kv:])], axis=1).astype(BF16)
            kg = jnp.tile(sw_k_norm_g[j], 2 * n_kv).reshape(1, 2 * kv)
            q, k, v = _sw_qkv(x2d, m2, norm_g[i, 1], w, qg, kg, **kw)
            x2d = _sw_attn(q, k, v, sw_sinks[j], x2d, m2, sw_w_o[j].astype(BF16), B=B, T=T,
                           tq=_pick_tile(T, 1024), group=(D // HEAD_DIM) // n_kv)
            mix = None
        x2d = _ffn(x2d, mod[i, :, 2], norm_g[i, 2], ffn_w_in[i, 1].astype(BF16), ffn_w_out[i, 1].astype(BF16),
                   mix=mix, **fkw)
    return x2d.reshape(B, T, D)
```
